```python
import math
import jax
import jax.numpy as jnp
from jax import lax
import numpy as np

D_MODEL = 2048
BATCH = 2
SEQ = 4096
DEPTH = 2
DEC_BATCH = 32
DEC_SEQ = 64
PAST_LEN = 1024

CHUNK = 64
Q_BLOCK = 128
ROPE_THETA = 500000.0
EPS = 1e-6
NEG_INF = -1e30
FORGET_BIAS = 3.0

A_HEADS = D_MODEL // 256
A_NOPE = 128
A_ROPE = 64
A_V = 128
A_QK = A_NOPE + A_ROPE
A_KV_RANK = D_MODEL // 4
B_HEADS = D_MODEL // 512
B_DIM = 128
C_HEADS = D_MODEL // 512
C_QK = 64
C_V = 2 * C_QK
C_ROT = C_QK // 4

MIX_WIDTH = A_HEADS * A_V + B_HEADS * B_DIM + C_HEADS * C_V
IN_SIZES = (A_HEADS * A_QK, A_KV_RANK, A_ROPE,
            B_HEADS * B_DIM, B_HEADS * B_DIM, B_HEADS * B_DIM, B_HEADS,
            C_HEADS * 2 * C_QK, C_HEADS * 2 * C_QK, C_HEADS * C_V)
IN_COLS = A_HEADS * A_QK + A_KV_RANK + A_ROPE + 3 * B_HEADS * B_DIM + B_HEADS + 2 * C_HEADS * 2 * C_QK + C_HEADS * C_V

D_FF = 5632
CONV_W = 3

kernel_name = 'hybrid_mla_fox_diff_convffn_stream_step'


def _rms(x, g):
    xf = x.astype(jnp.float32)
    y = xf * lax.rsqrt(jnp.mean(xf * xf, axis=-1, keepdims=True) + EPS)
    return (y * g.astype(jnp.float32)).astype(x.dtype)


def _rope(x, pos, n_rot):
    half = n_rot // 2
    inv = jnp.power(jnp.float32(ROPE_THETA), -jnp.arange(half, dtype=jnp.float32) * (2.0 / n_rot))
    ang = pos.astype(jnp.float32)[:, None] * inv[None, :]
    cos = jnp.cos(ang)[:, None, :]
    sin = jnp.sin(ang)[:, None, :]
    xf = x.astype(jnp.float32)
    x1, x2, rest = xf[..., :half], xf[..., half:n_rot], xf[..., n_rot:]
    out = jnp.concatenate([x1 * cos - x2 * sin, x2 * cos + x1 * sin, rest], axis=-1)
    return out.astype(x.dtype)


def _chunk_mask(qpos, kpos):
    return (kpos[None, :] // CHUNK) <= (qpos[:, None] // CHUNK)


def _causal_mask(qpos, kpos):
    return kpos[None, :] <= qpos[:, None]


def _probs(q, k, mask, scale, bias=None):
    s = jnp.einsum('bqhd,bkhd->bhqk', q, k, preferred_element_type=jnp.float32) * scale
    if bias is not None:
        s = s + bias
    s = jnp.where(mask[None, None], s, NEG_INF)
    return jax.nn.softmax(s, axis=-1)


def _attend_blocks(fn, q_tree, qpos):
    T = qpos.shape[0]
    if T > Q_BLOCK and T % Q_BLOCK == 0:
        nblk = T // Q_BLOCK

        def split(a):
            a = a.reshape((a.shape[0], nblk, Q_BLOCK) + a.shape[2:])
            return jnp.moveaxis(a, 1, 0)

        xs = (jax.tree_util.tree_map(split, q_tree), qpos.reshape(nblk, Q_BLOCK))
        out = lax.map(lambda a: fn(a[0], a[1]), xs)
        out = jnp.moveaxis(out, 0, 1)
        return out.reshape((out.shape[0], T) + out.shape[3:])
    return fn(q_tree, qpos)


def _split_cols(z):
    offs, acc = [], 0
    for s in IN_SIZES[:-1]:
        acc += s
        offs.append(acc)
    return jnp.split(z, offs, axis=-1)


def _layer(x, pos, past, p, lam_init):
    pa_ckv, pa_kr, pb_k, pb_v, pb_lf, pc_k, pc_v, p_conv = past
    bsz, T, _ = x.shape
    P = pa_ckv.shape[1]
    kpos = jnp.arange(P + T, dtype=jnp.int32)

    h = _rms(x, p['attn_norm'])
    z = h @ p['w_in']
    aq, ackv, akr, bq, bk, bv, bf, cq, ck, cv = _split_cols(z)

    aq = aq.reshape(bsz, T, A_HEADS, A_QK)
    q_nope = _rms(aq[..., :A_NOPE], p['a_qn_nope'])
    q_rope = _rope(_rms(aq[..., A_NOPE:], p['a_qn_rope']), pos, A_ROPE)
    a_q = jnp.concatenate([q_nope, q_rope], axis=-1)
    ckv_new = _rms(ackv, p['a_kv_norm'])
    kr_new = _rope(_rms(akr, p['a_kn_rope'])[:, :, None, :], pos, A_ROPE)[:, :, 0]
    ckv_all = jnp.concatenate([pa_ckv.astype(ckv_new.dtype), ckv_new], axis=1)
    kr_all = jnp.concatenate([pa_kr.astype(kr_new.dtype), kr_new], axis=1)
    kv = (ckv_all @ p['a_w_ukv']).reshape(bsz, P + T, A_HEADS, A_NOPE + A_V)
    k_nope = _rms(kv[..., :A_NOPE], p['a_kn_nope'])
    a_v = kv[..., A_NOPE:]
    a_k = jnp.concatenate([k_nope, jnp.broadcast_to(kr_all[:, :, None, :], (bsz, P + T, A_HEADS, A_ROPE))], axis=-1)

    def mla(qb, qp):
        pr = _probs(qb, a_k, _chunk_mask(qp, kpos), A_QK ** -0.5)
        return jnp.einsum('bhqk,bkhd->bqhd', pr.astype(a_v.dtype), a_v)

    out_a = _attend_blocks(mla, a_q, pos)

    b_q = _rms(bq.reshape(bsz, T, B_HEADS, B_DIM), p['b_qn'])
    bk_new = _rms(bk.reshape(bsz, T, B_HEADS, B_DIM), p['b_kn'])
    bv_new = bv.reshape(bsz, T, B_HEADS, B_DIM)
    lf_new = jax.nn.log_sigmoid(bf.astype(jnp.float32) + p['b_forget'].astype(jnp.float32))
    bk_all = jnp.concatenate([pb_k.astype(bk_new.dtype), bk_new], axis=1)
    bv_all = jnp.concatenate([pb_v.astype(bv_new.dtype), bv_new], axis=1)
    lf_all = jnp.concatenate([pb_lf.astype(jnp.float32), lf_new], axis=1)
    cum = jnp.cumsum(lf_all, axis=1)
    cum_k = jnp.transpose(cum, (0, 2, 1))
    cum_q = cum[:, P:]

    def fox(qb, qp):
        q, cq_b = qb
        bias = jnp.transpose(cq_b, (0, 2, 1))[..., None] - cum_k[:, :, None, :]
        pr = _probs(q, bk_all, _causal_mask(qp, kpos), B_DIM ** -0.5, bias)
        return jnp.einsum('bhqk,bkhd->bqhd', pr.astype(bv_all.dtype), bv_all)

    out_b = _attend_blocks(fox, (b_q, cum_q), pos)

    c_q = _rope(_rms(cq.reshape(bsz, T, 2 * C_HEADS, C_QK), p['c_qn']), pos, C_ROT).reshape(bsz, T, C_HEADS, 2 * C_QK)
    ck_new = _rope(_rms(ck.reshape(bsz, T, 2 * C_HEADS, C_QK), p['c_kn']), pos, C_ROT).reshape(bsz, T, C_HEADS, 2 * C_QK)
    cv_new = cv.reshape(bsz, T, C_HEADS, C_V)
    ck_all = jnp.concatenate([pc_k.astype(ck_new.dtype), ck_new], axis=1)
    cv_all = jnp.concatenate([pc_v.astype(cv_new.dtype), cv_new], axis=1)
    lam = (jnp.exp(jnp.sum(p['c_lq1'].astype(jnp.float32) * p['c_lk1'].astype(jnp.float32)))
           - jnp.exp(jnp.sum(p['c_lq2'].astype(jnp.float32) * p['c_lk2'].astype(jnp.float32))) + lam_init)

    def diff(qb, qp):
        m = _chunk_mask(qp, kpos)
        p1 = _probs(qb[..., :C_QK], ck_all[..., :C_QK], m, C_QK ** -0.5)
        p2 = _probs(qb[..., C_QK:], ck_all[..., C_QK:], m, C_QK ** -0.5)
        return jnp.einsum('bhqk,bkhd->bqhd', (p1 - lam * p2).astype(cv_all.dtype), cv_all)

    out_c = _rms(_attend_blocks(diff, c_q, pos), p['c_out_norm']) * (1.0 - lam_init)

    mix = jnp.concatenate([out_a.reshape(bsz, T, -1), out_b.reshape(bsz, T, -1), out_c.reshape(bsz, T, -1)], axis=-1)
    x = x + mix @ p['w_out']

    u = _rms(x, p['ffn_norm']) @ p['w_up']
    u_all = jnp.concatenate([p_conv.astype(u.dtype), u], axis=1)
    y = p['conv_b']
    for i in range(CONV_W):
        y = y + u_all[:, i:i + T] * p['conv_w'][i]
    g, v = jnp.split(y, 2, axis=-1)
    x = x + (jax.nn.silu(g) * v) @ p['w_down']
    conv_new = u_all[:, T:]

    return x, (ckv_new, kr_new, bk_new, bv_new, lf_new, ck_new, cv_new, conv_new)


def setup_inputs(seed: int = 0) -> dict:
    key = jax.random.key(seed)
    ks = iter(jax.random.split(key, 48))

    def nrm(shape, s):
        return jax.random.normal(next(ks), shape, jnp.float32) * s

    def gain(n):
        return 1.0 + nrm((DEPTH, n), 0.02)

    F2 = 2 * D_FF
    return {
        'x_prompt': nrm((BATCH, SEQ, D_MODEL), 1.0),
        'x_sample': nrm((DEC_BATCH, DEC_SEQ, D_MODEL), 1.0),
        'cache_a_ckv': nrm((DEPTH, DEC_BATCH, PAST_LEN, A_KV_RANK), 1.0),
        'cache_a_krope': nrm((DEPTH, DEC_BATCH, PAST_LEN, A_ROPE), 1.0),
        'cache_b_k': nrm((DEPTH, DEC_BATCH, PAST_LEN, B_HEADS, B_DIM), 1.0),
        'cache_b_v': nrm((DEPTH, DEC_BATCH, PAST_LEN, B_HEADS, B_DIM), 1.0),
        'cache_b_logf': jax.nn.log_sigmoid(FORGET_BIAS + nrm((DEPTH, DEC_BATCH, PAST_LEN, B_HEADS), 1.0)),
        'cache_c_k': nrm((DEPTH, DEC_BATCH, PAST_LEN, C_HEADS, 2 * C_QK), 1.0),
        'cache_c_v': nrm((DEPTH, DEC_BATCH, PAST_LEN, C_HEADS, C_V), 1.0),
        'state_ffn_conv': nrm((DEPTH, DEC_BATCH, CONV_W - 1, F2), 1.0),
        'attn_norm': gain(D_MODEL),
        'w_in': nrm((DEPTH, D_MODEL, IN_COLS), D_MODEL ** -0.5),
        'b_forget': FORGET_BIAS + nrm((DEPTH, B_HEADS), 0.1),
        'a_kv_norm': gain(A_KV_RANK),
        'a_w_ukv': nrm((DEPTH, A_KV_RANK, A_HEADS * (A_NOPE + A_V)), A_KV_RANK ** -0.5),
        'a_qn_nope': gain(A_NOPE),
        'a_qn_rope': gain(A_ROPE),
        'a_kn_nope': gain(A_NOPE),
        'a_kn_rope': gain(A_ROPE),
        'b_qn': gain(B_DIM),
        'b_kn': gain(B_DIM),
        'c_qn': gain(C_QK),
        'c_kn': gain(C_QK),
        'c_lq1': nrm((DEPTH, C_QK), 0.1),
        'c_lk1': nrm((DEPTH, C_QK), 0.1),
        'c_lq2': nrm((DEPTH, C_QK), 0.1),
        'c_lk2': nrm((DEPTH, C_QK), 0.1),
        'c_out_norm': gain(C_V),
        'w_out': nrm((DEPTH, MIX_WIDTH, D_MODEL), MIX_WIDTH ** -0.5),
        'ffn_norm': gain(D_MODEL),
        'w_up': nrm((DEPTH, D_MODEL, F2), D_MODEL ** -0.5),
        'conv_w': nrm((DEPTH, CONV_W, F2), CONV_W ** -0.5),
        'conv_b': nrm((DEPTH, F2), 0.02),
        'w_down': nrm((DEPTH, D_FF, D_MODEL), D_FF ** -0.5),
    }


def reference(x_prompt, x_sample, cache_a_ckv, cache_a_krope, cache_b_k, cache_b_v, cache_b_logf,
              cache_c_k, cache_c_v, state_ffn_conv, attn_norm, w_in, b_forget, a_kv_norm, a_w_ukv,
              a_qn_nope, a_qn_rope, a_kn_nope, a_kn_rope, b_qn, b_kn, c_qn, c_kn,
              c_lq1, c_lk1, c_lq2, c_lk2, c_out_norm, w_out, ffn_norm, w_up, conv_w, conv_b, w_down):
    bp, tp = x_prompt.shape[0], x_prompt.shape[1]
    ts = x_sample.shape[1]
    past_len = cache_a_ckv.shape[2]
    pos_p = jnp.arange(tp, dtype=jnp.int32)
    pos_s = past_len + jnp.arange(ts, dtype=jnp.int32)
    dt = x_prompt.dtype
    yp, ys = x_prompt, x_sample
    states_p, states_s = [], []
    for l in range(DEPTH):
        prm = dict(attn_norm=attn_norm[l], w_in=w_in[l], b_forget=b_forget[l], a_kv_norm=a_kv_norm[l],
                   a_w_ukv=a_w_ukv[l], a_qn_nope=a_qn_nope[l], a_qn_rope=a_qn_rope[l],
                   a_kn_nope=a_kn_nope[l], a_kn_rope=a_kn_rope[l], b_qn=b_qn[l], b_kn=b_kn[l],
                   c_qn=c_qn[l], c_kn=c_kn[l], c_lq1=c_lq1[l], c_lk1=c_lk1[l], c_lq2=c_lq2[l],
                   c_lk2=c_lk2[l], c_out_norm=c_out_norm[l], w_out=w_out[l], ffn_norm=ffn_norm[l],
                   w_up=w_up[l], conv_w=conv_w[l], conv_b=conv_b[l], w_down=w_down[l])
        lam_init = 0.8 - 0.6 * math.exp(-0.3 * l)
        past_p = (jnp.zeros((bp, 0, A_KV_RANK), dt), jnp.zeros((bp, 0, A_ROPE), dt),
                  jnp.zeros((bp, 0, B_HEADS, B_DIM), dt), jnp.zeros((bp, 0, B_HEADS, B_DIM), dt),
                  jnp.zeros((bp, 0, B_HEADS), jnp.float32),
                  jnp.zeros((bp, 0, C_HEADS, 2 * C_QK), dt), jnp.zeros((bp, 0, C_HEADS, C_V), dt),
                  jnp.zeros((bp, CONV_W - 1, 2 * D_FF), dt))
        yp, st_p = _layer(yp, pos_p, past_p, prm, lam_init)
        past_s = (cache_a_ckv[l], cache_a_krope[l], cache_b_k[l], cache_b_v[l], cache_b_logf[l],
                  cache_c_k[l], cache_c_v[l], state_ffn_conv[l])
        ys, st_s = _layer(ys, pos_s, past_s, prm, lam_init)
        states_p.append(st_p)
        states_s.append(st_s)
    p_ckv, p_krope, p_bk, p_bv, p_logf, p_ck, p_cv, p_conv = [jnp.stack(s) for s in zip(*states_p)]
    s_ckv, s_krope, s_bk, s_bv, s_logf, s_ck, s_cv, s_conv = [jnp.stack(s) for s in zip(*states_s)]
    return (yp, ys, p_ckv, p_krope, p_bk, p_bv, p_logf, p_ck, p_cv, p_conv,
            s_ckv, s_krope, s_bk, s_bv, s_logf, s_ck, s_cv, s_conv)
```

```python
import functools
import math

import jax
import jax.numpy as jnp
from jax import lax
from jax.experimental import pallas as pl
from jax.experimental.pallas import tpu as pltpu

F32 = jnp.float32
BF16 = jnp.bfloat16

D_MODEL = 2048
CHUNK = 64
ROPE_THETA = 500000.0
EPS = 1e-6
NEG_INF = -1e30

A_HEADS = 8
A_NOPE = 128
A_ROPE = 64
A_V = 128
A_QK = A_NOPE + A_ROPE
A_KV_RANK = 512
B_HEADS = 4
B_DIM = 128
C_HEADS = 4
C_QK = 64
C_V = 128
C_ROT = 16
D_FF = 5632
CONV_W = 3

LANES = 128
HALO_ROW = 8
VMEM_LIMIT = 56 * 1024 * 1024

Z_QN, Z_QR, Z_CKV, Z_BQ, Z_BK, Z_BV, Z_CQ, Z_CK, Z_CV, Z_KR, Z_BF, Z_COLS = (
    0, 1024, 1536, 2048, 2560, 3072, 3584, 4096, 4608, 5120, 5248, 5376)
Z_TN = 896


def _params(sem):
    return pltpu.CompilerParams(dimension_semantics=sem, vmem_limit_bytes=VMEM_LIMIT)


def _dot(a, b):
    return jnp.dot(a, b, preferred_element_type=F32)


def _dot_nt(a, b):
    return lax.dot_general(a, b, (((1,), (1,)), ((), ())), preferred_element_type=F32)


def _lane_iota(shape):
    return lax.broadcasted_iota(jnp.int32, shape, len(shape) - 1)


def _norm_mm_kernel(x_ref, g_ref, w_ref, o_ref, h_scr):
    @pl.when(pl.program_id(1) == 0)
    def _():
        x = x_ref[...]
        ms = jnp.mean(x * x, axis=-1, keepdims=True)
        h_scr[...] = (x * lax.rsqrt(ms + EPS) * g_ref[...]).astype(BF16)

    o_ref[...] = _dot(h_scr[...], w_ref[...])


def _norm_mm(x, g, w, tm, tn):
    m, d = x.shape
    n = w.shape[1]
    return pl.pallas_call(
        _norm_mm_kernel,
        grid=(m // tm, n // tn),
        in_specs=[pl.BlockSpec((tm, d), lambda i, j: (i, 0)),
                  pl.BlockSpec((1, d), lambda i, j: (0, 0)),
                  pl.BlockSpec((d, tn), lambda i, j: (0, j))],
        out_specs=pl.BlockSpec((tm, tn), lambda i, j: (i, j)),
        out_shape=jax.ShapeDtypeStruct((m, n), F32),
        scratch_shapes=[pltpu.VMEM((tm, d), BF16)],
        compiler_params=_params(("arbitrary", "arbitrary")),
        name="norm_mm",
    )(x, g, w)


def _seg_sumsq(x, bd):
    sq = x * x
    hi = sq.astype(BF16)
    lo = (sq - hi.astype(F32)).astype(BF16)
    return _dot(hi, bd) + _dot(lo, bd)


def _seg_norm(x, bd, seg, g):
    return x * lax.rsqrt(_seg_sumsq(x, bd) * (1.0 / seg) + EPS) * g


def _rope(x, cos, sin, half):
    n = x.shape[-1]
    first = (_lane_iota(x.shape) % 64) < half
    partner = jnp.where(first, pltpu.roll(x, n - half, 1), pltpu.roll(x, half, 1))
    return x * cos + partner * sin


def _post_kernel(z_ref, cosa_ref, sina_ref, cosc_ref, sinc_ref,
                 gqn_ref, gqr_ref, gckv_ref, gkr_ref, gkn_ref, gbq_ref, gbk_ref, gcq_ref, gck_ref,
                 bfg_ref, bd64_ref, bd128_ref, ones_ref, bdkr_ref, wk_ref, wv_ref,
                 aq_ref, kf_ref, va_ref, ckv_ref, kr_ref, bq_ref, bk32_ref, bk16_ref,
                 bv32_ref, bv16_ref, lf_ref, cq_ref, ck32_ref, ck16_ref, cv32_ref, cv16_ref):
    bd64 = bd64_ref[...]
    bd128 = bd128_ref[...]
    tm = z_ref.shape[0]
    lane = _lane_iota((tm, LANES))
    low = lane < 64

    cosa4 = jnp.concatenate([cosa_ref[...]] * 4, axis=1)
    sina4 = jnp.concatenate([sina_ref[...]] * 4, axis=1)
    cosc4 = jnp.concatenate([cosc_ref[...]] * 4, axis=1)
    sinc4 = jnp.concatenate([sinc_ref[...]] * 4, axis=1)

    a_scale = A_QK ** -0.5
    qr = _seg_norm(z_ref[:, Z_QR:Z_QR + 512], bd64, 64, gqr_ref[...])
    qr = _rope(qr, cosa4, sina4, A_ROPE // 2) * a_scale
    for half in range(2):
        c0 = Z_QN + half * 512
        qn = _seg_norm(z_ref[:, c0:c0 + 512], bd128, 128, gqn_ref[:, half * 512:(half + 1) * 512]) * a_scale
        for hh in range(4):
            h = half * 4 + hh
            aq_ref[:, h * 256:h * 256 + 128] = qn[:, hh * 128:(hh + 1) * 128].astype(BF16)
    for h in range(A_HEADS):
        pair = qr[:, (h // 2) * 128:(h // 2 + 1) * 128]
        keep = low if h % 2 == 0 else jnp.logical_not(low)
        aq_ref[:, h * 256 + 128:h * 256 + 256] = jnp.where(keep, pair, 0.0).astype(BF16)

    ckv = _seg_norm(z_ref[:, Z_CKV:Z_CKV + 512], ones_ref[...], 512, gckv_ref[...])
    ckv_ref[...] = ckv
    ckv16 = ckv.astype(BF16)
    va_ref[...] = _dot(ckv16, wv_ref[...]).astype(BF16)

    kr = _seg_norm(z_ref[:, Z_KR:Z_KR + 128], bdkr_ref[...], 64, gkr_ref[...])
    kr = _rope(kr, cosa_ref[...], sina_ref[...], A_ROPE // 2)
    kr_ref[...] = kr[:, :A_ROPE]
    kr2 = kr + pltpu.roll(kr, 64, 1)
    kr_even = jnp.where(low, kr2, 0.0).astype(BF16)
    kr_odd = jnp.where(low, 0.0, kr2).astype(BF16)
    for half in range(2):
        kk = _dot(ckv16, wk_ref[:, half * 512:(half + 1) * 512])
        kn = _seg_norm(kk, bd128, 128, gkn_ref[:, half * 512:(half + 1) * 512])
        for hh in range(4):
            h = half * 4 + hh
            kf_ref[:, h * 256:h * 256 + 128] = kn[:, hh * 128:(hh + 1) * 128].astype(BF16)
            kf_ref[:, h * 256 + 128:h * 256 + 256] = kr_even if h % 2 == 0 else kr_odd

    bq = _seg_norm(z_ref[:, Z_BQ:Z_BQ + 512], bd128, 128, gbq_ref[...]) * (B_DIM ** -0.5)
    bq_ref[...] = bq.astype(BF16)
    bk = _seg_norm(z_ref[:, Z_BK:Z_BK + 512], bd128, 128, gbk_ref[...])
    bk32_ref[...] = bk
    bk16_ref[...] = bk.astype(BF16)
    bv = z_ref[:, Z_BV:Z_BV + 512]
    bv32_ref[...] = bv
    bv16_ref[...] = bv.astype(BF16)
    f = z_ref[:, Z_BF:Z_BF + 128][:, 0:B_HEADS] + bfg_ref[:, 0:B_HEADS]
    lf_ref[...] = jnp.minimum(f, 0.0) - jnp.log1p(jnp.exp(-jnp.abs(f)))

    cq = _seg_norm(z_ref[:, Z_CQ:Z_CQ + 512], bd64, 64, gcq_ref[...])
    cq_ref[...] = (_rope(cq, cosc4, sinc4, C_ROT // 2) * (C_QK ** -0.5)).astype(BF16)
    ck = _seg_norm(z_ref[:, Z_CK:Z_CK + 512], bd64, 64, gck_ref[...])
    ck = _rope(ck, cosc4, sinc4, C_ROT // 2)
    ck32_ref[...] = ck
    ck16_ref[...] = ck.astype(BF16)
    cv = z_ref[:, Z_CV:Z_CV + 512]
    cv32_ref[...] = cv
    cv16_ref[...] = cv.astype(BF16)


def _post(z, tabs, consts, tm):
    m = z.shape[0]
    tab_rows = tabs[0].shape[0]
    tab_blocks = tab_rows // tm

    def row(w):
        return pl.BlockSpec((tm, w), lambda i: (i, 0))

    def full(a):
        return pl.BlockSpec(a.shape, lambda i: (0, 0))

    tab_spec = pl.BlockSpec((tm, LANES), lambda i: (i % tab_blocks, 0))
    widths = [(2048, BF16), (2048, BF16), (1024, BF16), (512, F32), (64, F32), (512, BF16), (512, F32),
              (512, BF16), (512, F32), (512, BF16), (B_HEADS, F32), (512, BF16), (512, F32), (512, BF16),
              (512, F32), (512, BF16)]
    return pl.pallas_call(
        _post_kernel,
        grid=(m // tm,),
        in_specs=[row(Z_COLS)] + [tab_spec] * 4 + [full(c) for c in consts],
        out_specs=[row(w) for w, _ in widths],
        out_shape=[jax.ShapeDtypeStruct((m, w), dt) for w, dt in widths],
        compiler_params=_params(("arbitrary",)),
        name="post",
    )(z, *tabs, *consts)


def _split3(c):
    hi = c.astype(BF16).astype(F32)
    r1 = c - hi
    mid = r1.astype(BF16).astype(F32)
    lo = r1 - mid
    return hi, mid, lo


def _cumsum_block(lf, carry):
    n = lf.shape[0]
    r = lax.broadcasted_iota(jnp.int32, (n, n), 0)
    c = lax.broadcasted_iota(jnp.int32, (n, n), 1)
    tri = jnp.where(c <= r, 1.0, 0.0).astype(BF16)
    hi, mid, lo = _split3(lf)
    cum = _dot(tri, hi.astype(BF16)) + _dot(tri, mid.astype(BF16)) + _dot(tri, lo.astype(BF16))
    return cum + carry


def _aug_q(c, lane):
    hi, mid, lo = _split3(c)
    return jnp.where(lane == 0, hi, jnp.where(lane == 1, mid, jnp.where(lane == 2, lo,
                     jnp.where(lane < 6, 1.0, 0.0)))).astype(BF16)


def _aug_k(c, lane):
    hi, mid, lo = _split3(c)
    return jnp.where(lane < 3, 1.0, jnp.where(lane == 3, -hi, jnp.where(lane == 4, -mid,
                     jnp.where(lane == 5, -lo, 0.0)))).astype(BF16)


def _fox_prep_kernel(lf_ref, bq_ref, bk_ref, *out_refs, past, blocks):
    carry_scr = out_refs[-1]
    if past:
        qa_ref, ka_ref, kpast_ref = out_refs[:-1]
    else:
        qa_ref, ka_ref = out_refs[:-1]

    @pl.when(pl.program_id(1) == 0)
    def _():
        carry_scr[...] = jnp.zeros(carry_scr.shape, F32)

    carry = carry_scr[...]
    for r0, n in blocks:
        cum = _cumsum_block(lf_ref[0, r0:r0 + n, :], carry)
        carry = cum[n - 1:n, :]
        lane = _lane_iota((n, LANES))
        for h in range(B_HEADS):
            c = cum[:, h:h + 1]
            if r0 < past:
                kpast_ref[0, r0:r0 + n, h * 128:(h + 1) * 128] = _aug_k(c, lane)
            else:
                t0 = r0 - past
                qa_ref[t0:t0 + n, h * 256:h * 256 + 128] = bq_ref[t0:t0 + n, h * 128:(h + 1) * 128]
                qa_ref[t0:t0 + n, h * 256 + 128:h * 256 + 256] = _aug_q(c, lane)
                ka_ref[t0:t0 + n, h * 256:h * 256 + 128] = bk_ref[t0:t0 + n, h * 128:(h + 1) * 128]
                ka_ref[t0:t0 + n, h * 256 + 128:h * 256 + 256] = _aug_k(c, lane)
    carry_scr[...] = carry


def _fox_prep(lf_all, bq, bk, t_new, rows, blocks):
    s, t_tot, _ = lf_all.shape
    past = t_tot - t_new
    nb = t_tot // rows
    new_rows = rows - past
    m = bq.shape[0]
    new_spec = lambda w: pl.BlockSpec((new_rows, w), lambda i, j: (i * nb + j, 0))
    out_specs = [new_spec(1024), new_spec(1024)]
    out_shape = [jax.ShapeDtypeStruct((m, 1024), BF16), jax.ShapeDtypeStruct((m, 1024), BF16)]
    if past:
        out_specs.append(pl.BlockSpec((1, past, 512), lambda i, j: (i, 0, 0)))
        out_shape.append(jax.ShapeDtypeStruct((s, past, 512), BF16))
    return pl.pallas_call(
        functools.partial(_fox_prep_kernel, past=past, blocks=blocks),
        grid=(s, nb),
        in_specs=[pl.BlockSpec((1, rows, B_HEADS), lambda i, j: (i, j, 0)), new_spec(512), new_spec(512)],
        out_specs=out_specs,
        out_shape=out_shape,
        scratch_shapes=[pltpu.VMEM((1, B_HEADS), F32)],
        compiler_params=_params(("arbitrary", "arbitrary")),
        name="fox_prep",
    )(lf_all, bq, bk)


def _softmax_init(m_scr, l_scr, acc_scr):
    m_scr[...] = jnp.full(m_scr.shape, NEG_INF, F32)
    l_scr[...] = jnp.zeros(l_scr.shape, F32)
    acc_scr[...] = jnp.zeros(acc_scr.shape, F32)


def _softmax_step(i, s, v, m_scr, l_scr, acc_scr):
    m_old = m_scr[i]
    m_new = jnp.maximum(m_old, jnp.max(s, axis=-1, keepdims=True))
    alpha = jnp.exp(m_old - m_new)
    p = jnp.exp(s - m_new)
    l_scr[i] = alpha * l_scr[i] + jnp.sum(p, axis=-1, keepdims=True)
    acc_scr[i] = alpha * acc_scr[i] + _dot(p.astype(BF16), v)
    m_scr[i] = m_new


def _visible(kind, q0, k0, tq, tk):
    qp = q0 + lax.broadcasted_iota(jnp.int32, (tq, tk), 0)
    kp = k0 + lax.broadcasted_iota(jnp.int32, (tq, tk), 1)
    if kind == "chunk":
        return (kp // CHUNK) <= (qp // CHUNK)
    return kp <= qp


def _split_maps(q, n_maps):
    if n_maps == 1:
        return [q]
    low = _lane_iota(q.shape) < C_QK
    zero = jnp.zeros_like(q)
    return [jnp.where(low, q, zero), jnp.where(low, zero, q)]


def _diff_lambda(lq1_ref, lk1_ref, lq2_ref, lk2_ref, lam_init):
    s1 = jnp.sum(lq1_ref[...] * lk1_ref[...], axis=-1, keepdims=True)
    s2 = jnp.sum(lq2_ref[...] * lk2_ref[...], axis=-1, keepdims=True)
    return jnp.exp(s1) - jnp.exp(s2) + lam_init


def _attn_finish(l_scr, acc_scr, extra_refs, n_maps, lam_init):
    if n_maps == 1:
        return acc_scr[0] / l_scr[0]
    lq1_ref, lk1_ref, lq2_ref, lk2_ref, gout_ref = extra_refs
    lam = _diff_lambda(lq1_ref, lk1_ref, lq2_ref, lk2_ref, lam_init)
    o = acc_scr[0] / l_scr[0] - lam * (acc_scr[1] / l_scr[1])
    ms = jnp.mean(o * o, axis=-1, keepdims=True)
    return o * lax.rsqrt(ms + EPS) * gout_ref[...] * (1.0 - lam_init)


def _attn_prompt_kernel(*refs, n_maps, mask, tq, tk, lam_init):
    q_ref, k_ref, v_ref = refs[:3]
    extra_refs = refs[3:-4]
    o_ref, m_scr, l_scr, acc_scr = refs[-4:]
    qi = pl.program_id(2)
    qs = _split_maps(q_ref[...], n_maps)
    _softmax_init(m_scr, l_scr, acc_scr)
    per_q = tq // tk

    def full_block(kb, carry):
        k0 = pl.multiple_of(kb * tk, tk)
        k = k_ref[pl.ds(k0, tk), :]
        v = v_ref[pl.ds(k0, tk), :]
        for i in range(n_maps):
            _softmax_step(i, _dot_nt(qs[i], k), v, m_scr, l_scr, acc_scr)
        return carry

    lax.fori_loop(0, qi * per_q, full_block, 0)
    for d in range(per_q):
        k0 = pl.multiple_of((qi * per_q + d) * tk, tk)
        k = k_ref[pl.ds(k0, tk), :]
        v = v_ref[pl.ds(k0, tk), :]
        vis = _visible(mask, 0, d * tk, tq, tk)
        for i in range(n_maps):
            s = jnp.where(vis, _dot_nt(qs[i], k), NEG_INF)
            _softmax_step(i, s, v, m_scr, l_scr, acc_scr)
    o_ref[...] = _attn_finish(l_scr, acc_scr, extra_refs, n_maps, lam_init).astype(o_ref.dtype)


def _attn_prompt(q, k, v, extras, *, batch, heads, t, dqk, n_maps, mask, tq, tk, lam_init=0.0):
    nq = t // tq
    dv = 128
    extra_specs = [pl.BlockSpec(e.shape, lambda b, h, i: (0, 0)) for e in extras]
    return pl.pallas_call(
        functools.partial(_attn_prompt_kernel, n_maps=n_maps, mask=mask, tq=tq, tk=tk, lam_init=lam_init),
        grid=(batch, heads, nq),
        in_specs=[pl.BlockSpec((tq, dqk), lambda b, h, i: (b * nq + i, h)),
                  pl.BlockSpec((t, dqk), lambda b, h, i: (b, h)),
                  pl.BlockSpec((t, dv), lambda b, h, i: (b, h))] + extra_specs,
        out_specs=pl.BlockSpec((tq, dv), lambda b, h, i: (b * nq + i, h)),
        out_shape=jax.ShapeDtypeStruct((batch * t, heads * dv), BF16),
        scratch_shapes=[pltpu.VMEM((n_maps, tq, 1), F32), pltpu.VMEM((n_maps, tq, 1), F32),
                        pltpu.VMEM((n_maps, tq, dv), F32)],
        compiler_params=_params(("arbitrary", "arbitrary", "arbitrary")),
        name="attn_prompt_" + mask + str(n_maps),
    )(q, k, v, *extras)


def _mla_sample_kernel(q_ref, kf_ref, va_ref, ckv_ref, kr_ref, wukv_ref, gkn_ref, bdn_ref, o_ref,
                       m_scr, l_scr, acc_scr, *, past, t):
    ckv_p = ckv_ref[0].astype(BF16)
    krp = kr_ref[0]
    low = _lane_iota(krp.shape) < 64
    kr_even = jnp.where(low, krp, 0.0).astype(BF16)
    kr_odd = jnp.where(low, 0.0, krp).astype(BF16)
    vis = _visible("chunk", past, past, t, t)
    for h in range(A_HEADS):
        kv = _dot(ckv_p, wukv_ref[:, h * 256:(h + 1) * 256])
        kk = kv[:, :A_NOPE]
        kn = (kk * lax.rsqrt(_seg_sumsq(kk, bdn_ref[...]) * (1.0 / A_NOPE) + EPS) * gkn_ref[...]).astype(BF16)
        vp = kv[:, A_NOPE:].astype(BF16)
        q = q_ref[:, h * 256:(h + 1) * 256]
        _softmax_init(m_scr, l_scr, acc_scr)
        s = _dot_nt(q[:, :128], kn) + _dot_nt(q[:, 128:], kr_even if h % 2 == 0 else kr_odd)
        _softmax_step(0, s, vp, m_scr, l_scr, acc_scr)
        s = jnp.where(vis, _dot_nt(q, kf_ref[:, h * 256:(h + 1) * 256]), NEG_INF)
        _softmax_step(0, s, va_ref[:, h * 128:(h + 1) * 128], m_scr, l_scr, acc_scr)
        o_ref[:, h * 128:(h + 1) * 128] = (acc_scr[0] / l_scr[0]).astype(o_ref.dtype)


def _mla_sample(q, kf, va, cache_ckv, cache_kr2, w_ukv, gkn, bdn, *, t):
    s, past, rank = cache_ckv.shape
    return pl.pallas_call(
        functools.partial(_mla_sample_kernel, past=past, t=t),
        grid=(s,),
        in_specs=[pl.BlockSpec((t, 2048), lambda i: (i, 0)),
                  pl.BlockSpec((t, 2048), lambda i: (i, 0)),
                  pl.BlockSpec((t, 1024), lambda i: (i, 0)),
                  pl.BlockSpec((1, past, rank), lambda i: (i, 0, 0)),
                  pl.BlockSpec((1, past, 128), lambda i: (i, 0, 0)),
                  pl.BlockSpec(w_ukv.shape, lambda i: (0, 0)),
                  pl.BlockSpec(gkn.shape, lambda i: (0, 0)),
                  pl.BlockSpec(bdn.shape, lambda i: (0, 0))],
        out_specs=pl.BlockSpec((t, 1024), lambda i: (i, 0)),
        out_shape=jax.ShapeDtypeStruct((s * t, 1024), BF16),
        scratch_shapes=[pltpu.VMEM((1, t, 1), F32), pltpu.VMEM((1, t, 1), F32), pltpu.VMEM((1, t, 128), F32)],
        compiler_params=_params(("arbitrary",)),
        name="mla_sample",
    )(q, kf, va, cache_ckv, cache_kr2, w_ukv, gkn, bdn)


def _bc_sample_kernel(*refs, n_maps, mask, heads, past, t, lam_init, with_aug):
    q_ref, kn_ref, vn_ref, kp_ref, vp_ref = refs[:5]
    rest = refs[5:-4]
    o_ref, m_scr, l_scr, acc_scr = refs[-4:]
    if with_aug:
        kaug_ref, extra_refs = rest[0], rest[1:]
    else:
        extra_refs = rest
    dq = q_ref.shape[1] // heads
    vis = _visible(mask, past, past, t, t)
    for h in range(heads):
        q = q_ref[:, h * dq:(h + 1) * dq]
        kp = kp_ref[0, :, h * 128:(h + 1) * 128].astype(BF16)
        vp = vp_ref[0, :, h * 128:(h + 1) * 128].astype(BF16)
        kn = kn_ref[:, h * dq:(h + 1) * dq]
        vn = vn_ref[:, h * 128:(h + 1) * 128]
        _softmax_init(m_scr, l_scr, acc_scr)
        if with_aug:
            s = _dot_nt(q[:, :128], kp) + _dot_nt(q[:, 128:], kaug_ref[0, :, h * 128:(h + 1) * 128])
            _softmax_step(0, s, vp, m_scr, l_scr, acc_scr)
            _softmax_step(0, jnp.where(vis, _dot_nt(q, kn), NEG_INF), vn, m_scr, l_scr, acc_scr)
        else:
            qs = _split_maps(q, n_maps)
            for i in range(n_maps):
                _softmax_step(i, _dot_nt(qs[i], kp), vp, m_scr, l_scr, acc_scr)
                _softmax_step(i, jnp.where(vis, _dot_nt(qs[i], kn), NEG_INF), vn, m_scr, l_scr, acc_scr)
        o_ref[:, h * 128:(h + 1) * 128] = _attn_finish(l_scr, acc_scr, extra_refs, n_maps, lam_init).astype(o_ref.dtype)


def _bc_sample(q, kn, vn, cache_k, cache_v, kaug, extras, *, heads, n_maps, mask, t, lam_init=0.0):
    s, past, _ = cache_k.shape
    dq = q.shape[1]
    in_specs = [pl.BlockSpec((t, dq), lambda i: (i, 0)),
                pl.BlockSpec((t, dq), lambda i: (i, 0)),
                pl.BlockSpec((t, heads * 128), lambda i: (i, 0)),
                pl.BlockSpec((1, past, heads * 128), lambda i: (i, 0, 0)),
                pl.BlockSpec((1, past, heads * 128), lambda i: (i, 0, 0))]
    args = [q, kn, vn, cache_k, cache_v]
    if kaug is not None:
        in_specs.append(pl.BlockSpec((1, past, heads * 128), lambda i: (i, 0, 0)))
        args.append(kaug)
    in_specs += [pl.BlockSpec(e.shape, lambda i: (0, 0)) for e in extras]
    return pl.pallas_call(
        functools.partial(_bc_sample_kernel, n_maps=n_maps, mask=mask, heads=heads, past=past, t=t,
                          lam_init=lam_init, with_aug=kaug is not None),
        grid=(s,),
        in_specs=in_specs,
        out_specs=pl.BlockSpec((t, heads * 128), lambda i: (i, 0)),
        out_shape=jax.ShapeDtypeStruct((s * t, heads * 128), BF16),
        scratch_shapes=[pltpu.VMEM((n_maps, t, 1), F32), pltpu.VMEM((n_maps, t, 1), F32),
                        pltpu.VMEM((n_maps, t, 128), F32)],
        compiler_params=_params(("arbitrary",)),
        name="sample_" + mask + str(n_maps),
    )(*args, *extras)


def _mm_res_kernel(*refs, n_in):
    res_ref = refs[2 * n_in]
    o_ref = refs[2 * n_in + 1]
    acc = res_ref[...]
    for i in range(n_in):
        acc = acc + _dot(refs[i][...], refs[n_in + i][...])
    o_ref[...] = acc


def _mm_res(res, a_list, w_list, tm, tn):
    m, n = res.shape
    n_in = len(a_list)
    in_specs = ([pl.BlockSpec((tm, a.shape[1]), lambda j, i: (i, 0)) for a in a_list]
                + [pl.BlockSpec((w.shape[0], tn), lambda j, i: (0, j)) for w in w_list]
                + [pl.BlockSpec((tm, tn), lambda j, i: (i, j))])
    return pl.pallas_call(
        functools.partial(_mm_res_kernel, n_in=n_in),
        grid=(n // tn, m // tm),
        in_specs=in_specs,
        out_specs=pl.BlockSpec((tm, tn), lambda j, i: (i, j)),
        out_shape=jax.ShapeDtypeStruct((m, n), F32),
        compiler_params=_params(("arbitrary", "arbitrary")),
        name="mm_res%d" % n_in,
    )(*a_list, *w_list, res)


def _ffn_up_kernel(x_ref, g_ref, wg_ref, wv_ref, cwg_ref, cwv_ref, cbg_ref, cbv_ref, sg_ref, sv_ref,
                   a_ref, ng_ref, nv_ref, h_scr, bufg, bufv, carg, carv, *, rows, tiles_per_stream):
    i = pl.program_id(0)
    j = pl.program_id(1)
    tm = x_ref.shape[0]
    n_sub = tm // rows

    @pl.when(j == 0)
    def _():
        x = x_ref[...]
        ms = jnp.mean(x * x, axis=-1, keepdims=True)
        h_scr[...] = (x * lax.rsqrt(ms + EPS) * g_ref[...]).astype(BF16)

    h = h_scr[...]
    halves = ((_dot(h, wg_ref[...]), bufg, carg, sg_ref, cwg_ref, cbg_ref, ng_ref),
              (_dot(h, wv_ref[...]), bufv, carv, sv_ref, cwv_ref, cbv_ref, nv_ref))
    ys = [[], []]
    for idx, (u, buf, car, s_ref, cw_ref, cb_ref, n_ref) in enumerate(halves):
        for s in range(n_sub):
            r0 = s * rows
            buf[HALO_ROW:HALO_ROW + rows, :] = u[r0:r0 + rows]
            if tiles_per_stream == 1:
                buf[HALO_ROW - 2:HALO_ROW, :] = s_ref[s]
            else:
                start = (i % tiles_per_stream) == 0

                @pl.when(start)
                def _():
                    buf[HALO_ROW - 2:HALO_ROW, :] = s_ref[0]

                @pl.when(jnp.logical_not(start))
                def _():
                    buf[HALO_ROW - 2:HALO_ROW, :] = car[j]

            y = (cb_ref[...] + buf[HALO_ROW - 2:HALO_ROW - 2 + rows, :] * cw_ref[0:1, :]
                 + buf[HALO_ROW - 1:HALO_ROW - 1 + rows, :] * cw_ref[1:2, :]
                 + u[r0:r0 + rows] * cw_ref[2:3, :])
            ys[idx].append(y)
            n_ref[s] = u[r0 + rows - 2:r0 + rows]
        if tiles_per_stream > 1:
            car[j] = u[tm - 2:tm]
    for s in range(n_sub):
        gate = ys[0][s]
        a_ref[s * rows:(s + 1) * rows, :] = (gate * (1.0 / (1.0 + jnp.exp(-gate))) * ys[1][s]).astype(BF16)


def _ffn_up(x, g, w_up, conv_w, conv_b, state, *, stream_len, tm, tn):
    m, d = x.shape
    nj = D_FF // tn
    n_streams = m // stream_len
    if stream_len >= tm:
        rows, tiles_per_stream, n_sub = tm, stream_len // tm, 1
        stream_of = lambda i: i // tiles_per_stream
    else:
        rows, tiles_per_stream, n_sub = stream_len, 1, tm // stream_len
        stream_of = lambda i: i
    gcol = lambda i, j: (0, j)
    vcol = lambda i, j: (0, nj + j)
    outs = pl.pallas_call(
        functools.partial(_ffn_up_kernel, rows=rows, tiles_per_stream=tiles_per_stream),
        grid=(m // tm, nj),
        in_specs=[pl.BlockSpec((tm, d), lambda i, j: (i, 0)),
                  pl.BlockSpec((1, d), lambda i, j: (0, 0)),
                  pl.BlockSpec((d, tn), gcol), pl.BlockSpec((d, tn), vcol),
                  pl.BlockSpec((CONV_W, tn), gcol), pl.BlockSpec((CONV_W, tn), vcol),
                  pl.BlockSpec((1, tn), gcol), pl.BlockSpec((1, tn), vcol),
                  pl.BlockSpec((n_sub, 2, tn), lambda i, j: (stream_of(i), 0, j)),
                  pl.BlockSpec((n_sub, 2, tn), lambda i, j: (stream_of(i), 0, nj + j))],
        out_specs=[pl.BlockSpec((tm, tn), lambda i, j: (i, j)),
                   pl.BlockSpec((n_sub, 2, tn), lambda i, j: (i, 0, j)),
                   pl.BlockSpec((n_sub, 2, tn), lambda i, j: (i, 0, j))],
        out_shape=[jax.ShapeDtypeStruct((m, D_FF), BF16),
                   jax.ShapeDtypeStruct((m // rows, 2, D_FF), F32),
                   jax.ShapeDtypeStruct((m // rows, 2, D_FF), F32)],
        scratch_shapes=[pltpu.VMEM((tm, d), BF16),
                        pltpu.VMEM((HALO_ROW + rows, tn), F32), pltpu.VMEM((HALO_ROW + rows, tn), F32),
                        pltpu.VMEM((nj, 2, tn), F32), pltpu.VMEM((nj, 2, tn), F32)],
        compiler_params=_params(("arbitrary", "arbitrary")),
        name="ffn_up",
    )(x, g, w_up, w_up, conv_w, conv_w, conv_b, conv_b, state, state)
    a, tail_g, tail_v = outs
    last = slice(tiles_per_stream - 1, None, tiles_per_stream)
    return a, tail_g[last], tail_v[last]


def _rope_tables(pos, reps):
    posf = pos.astype(F32)[:, None]

    def tab(n_rot):
        half = n_rot // 2
        inv = jnp.power(jnp.float32(ROPE_THETA), -jnp.arange(half, dtype=F32) * (2.0 / n_rot))
        ang = posf * inv[None, :]
        rest = 64 - n_rot
        cos = jnp.concatenate([jnp.cos(ang), jnp.cos(ang), jnp.ones((pos.shape[0], rest), F32)], axis=1)
        sin = jnp.concatenate([-jnp.sin(ang), jnp.sin(ang), jnp.zeros((pos.shape[0], rest), F32)], axis=1)
        return jnp.tile(cos, (reps, 2)), jnp.tile(sin, (reps, 2))

    cosa, sina = tab(A_ROPE)
    cosc, sinc = tab(C_ROT)
    return cosa, sina, cosc, sinc


def _block_diag(width, seg):
    r = jnp.arange(width)[:, None] // seg
    c = jnp.arange(width)[None, :] // seg
    return (r == c).astype(BF16)


def _layer_weights(l, attn_norm, w_in, b_forget, a_kv_norm, a_w_ukv, a_qn_nope, a_qn_rope, a_kn_nope,
                   a_kn_rope, b_qn, b_kn, c_qn, c_kn, c_lq1, c_lk1, c_lq2, c_lk2, c_out_norm, w_out,
                   ffn_norm, w_up, conv_w, conv_b, w_down):
    w = w_in[l]
    d = w.shape[0]
    aq = w[:, :A_HEADS * A_QK].reshape(d, A_HEADS, A_QK)
    o = A_HEADS * A_QK
    ckv0, kr0 = o, o + A_KV_RANK
    bq0 = kr0 + A_ROPE
    bf0 = bq0 + 3 * 512
    cq0 = bf0 + B_HEADS
    w_in_p = jnp.concatenate([
        aq[:, :, :A_NOPE].reshape(d, 1024), aq[:, :, A_NOPE:].reshape(d, 512),
        w[:, ckv0:ckv0 + 512], w[:, bq0:bq0 + 1536], w[:, cq0:cq0 + 1536],
        w[:, kr0:kr0 + 64], jnp.zeros((d, 64), F32),
        w[:, bf0:bf0 + B_HEADS], jnp.zeros((d, 128 - B_HEADS), F32)], axis=1).astype(BF16)
    ukv = a_w_ukv[l].reshape(A_KV_RANK, A_HEADS, A_NOPE + A_V)
    wk = ukv[:, :, :A_NOPE].reshape(A_KV_RANK, 1024).astype(BF16)
    wv = ukv[:, :, A_NOPE:].reshape(A_KV_RANK, 1024).astype(BF16)
    row = lambda v: v.reshape(1, -1).astype(F32)
    tile = lambda v, n: jnp.tile(v.astype(F32), n).reshape(1, -1)
    bd_kr = (jnp.arange(128)[:, None] < 64).astype(BF16) * jnp.ones((1, 128), BF16)
    post_consts = [
        tile(a_qn_nope[l], 8), tile(a_qn_rope[l], 8), row(a_kv_norm[l]),
        jnp.concatenate([a_kn_rope[l], jnp.zeros((64,), F32)]).reshape(1, 128),
        tile(a_kn_nope[l], 8), tile(b_qn[l], 4), tile(b_kn[l], 4), tile(c_qn[l], 8), tile(c_kn[l], 8),
        jnp.concatenate([b_forget[l], jnp.zeros((128 - B_HEADS,), F32)]).reshape(1, 128),
        _block_diag(512, 64), _block_diag(512, 128), jnp.ones((512, 512), BF16), bd_kr, wk, wv]
    wo = w_out[l].astype(BF16)
    return dict(
        attn_norm=row(attn_norm[l]), w_in=w_in_p, post_consts=post_consts,
        w_ukv=a_w_ukv[l].astype(BF16), gkn=row(a_kn_nope[l]), bdn=jnp.ones((128, 128), BF16),
        diff_extras=[row(c_lq1[l]), row(c_lk1[l]), row(c_lq2[l]), row(c_lk2[l]), row(c_out_norm[l])],
        wo_a=wo[:1024], wo_b=wo[1024:1536], wo_c=wo[1536:],
        ffn_norm=row(ffn_norm[l]), w_up=w_up[l].astype(BF16), conv_w=conv_w[l].astype(F32),
        conv_b=row(conv_b[l]), w_down=w_down[l].astype(BF16))


def _run_layer(x, lw, tabs, past, *, n_streams, t, lam_init, post_tm, ffn_tm):
    m = x.shape[0]
    z = _norm_mm(x, lw["attn_norm"], lw["w_in"], 512, Z_TN)
    (aq, kf, va, ckv, kr, bq, bk32, bk16, bv32, bv16, lf, cq, ck32, ck16, cv32, cv16) = _post(
        z, tabs, lw["post_consts"], post_tm)

    if past is None:
        qa, ka = _fox_prep(lf.reshape(n_streams, t, B_HEADS), bq, bk16, t, 512, ((0, 512),))
        oa = _attn_prompt(aq, kf, va, [], batch=n_streams, heads=A_HEADS, t=t, dqk=256, n_maps=1,
                          mask="chunk", tq=512, tk=512)
        ob = _attn_prompt(qa, ka, bv16, [], batch=n_streams, heads=B_HEADS, t=t, dqk=256, n_maps=1,
                          mask="causal", tq=512, tk=512)
        oc = _attn_prompt(cq, ck16, cv16, lw["diff_extras"], batch=n_streams, heads=C_HEADS, t=t, dqk=128,
                          n_maps=2, mask="chunk", tq=512, tk=512, lam_init=lam_init)
        conv_state = jnp.zeros((n_streams, CONV_W - 1, 2 * D_FF), F32)
    else:
        p_ckv, p_kr, p_bk, p_bv, p_lf, p_ck, p_cv, conv_state = past
        p_len = p_ckv.shape[1]
        lf_all = jnp.concatenate([p_lf, lf.reshape(n_streams, t, B_HEADS)], axis=1)
        blocks = tuple((r, 512) for r in range(0, p_len, 512)) + ((p_len, t),)
        qa, ka, kaug = _fox_prep(lf_all, bq, bk16, t, p_len + t, blocks)
        oa = _mla_sample(aq, kf, va, p_ckv, jnp.concatenate([p_kr, p_kr], axis=-1), lw["w_ukv"],
                         lw["gkn"], lw["bdn"], t=t)
        ob = _bc_sample(qa, ka, bv16, p_bk.reshape(n_streams, p_len, 512), p_bv.reshape(n_streams, p_len, 512),
                        kaug, [], heads=B_HEADS, n_maps=1, mask="causal", t=t)
        oc = _bc_sample(cq, ck16, cv16, p_ck.reshape(n_streams, p_len, 512), p_cv.reshape(n_streams, p_len, 512),
                        None, lw["diff_extras"], heads=C_HEADS, n_maps=2, mask="chunk", t=t, lam_init=lam_init)

    x1 = _mm_res(x, [oa, ob, oc], [lw["wo_a"], lw["wo_b"], lw["wo_c"]], 512, 1024)
    a, conv_g, conv_v = _ffn_up(x1, lw["ffn_norm"], lw["w_up"], lw["conv_w"], lw["conv_b"], conv_state,
                                stream_len=t, tm=ffn_tm, tn=512)
    x2 = _mm_res(x1, [a], [lw["w_down"]], 512, 512)
    states = (ckv.reshape(n_streams, t, A_KV_RANK), kr.reshape(n_streams, t, A_ROPE),
              bk32.reshape(n_streams, t, B_HEADS, B_DIM), bv32.reshape(n_streams, t, B_HEADS, B_DIM),
              lf.reshape(n_streams, t, B_HEADS),
              ck32.reshape(n_streams, t, C_HEADS, 2 * C_QK), cv32.reshape(n_streams, t, C_HEADS, C_V),
              jnp.concatenate([conv_g, conv_v], axis=-1))
    return x2, states


def kernel(x_prompt, x_sample, cache_a_ckv, cache_a_krope, cache_b_k, cache_b_v, cache_b_logf, cache_c_k, cache_c_v, state_ffn_conv, attn_norm, w_in, b_forget, a_kv_norm, a_w_ukv, a_qn_nope, a_qn_rope, a_kn_nope, a_kn_rope, b_qn, b_kn, c_qn, c_kn, c_lq1, c_lk1, c_lq2, c_lk2, c_out_norm, w_out, ffn_norm, w_up, conv_w, conv_b, w_down):
    bp, tp, d = x_prompt.shape
    bs, ts, _ = x_sample.shape
    depth = w_in.shape[0]
    past_len = cache_a_ckv.shape[2]
    post_tm = 256
    tabs_p = _rope_tables(jnp.arange(tp, dtype=jnp.int32), 1)
    tabs_s = _rope_tables(past_len + jnp.arange(ts, dtype=jnp.int32), post_tm // ts)

    yp = x_prompt.reshape(bp * tp, d)
    ys = x_sample.reshape(bs * ts, d)
    states_p, states_s = [], []
    for l in range(depth):
        lw = _layer_weights(l, attn_norm, w_in, b_forget, a_kv_norm, a_w_ukv, a_qn_nope, a_qn_rope,
                            a_kn_nope, a_kn_rope, b_qn, b_kn, c_qn, c_kn, c_lq1, c_lk1, c_lq2, c_lk2,
                            c_out_norm, w_out, ffn_norm, w_up, conv_w, conv_b, w_down)
        lam_init = 0.8 - 0.6 * math.exp(-0.3 * l)
        yp, st_p = _run_layer(yp, lw, tabs_p, None, n_streams=bp, t=tp, lam_init=lam_init,
                              post_tm=post_tm, ffn_tm=512)
        past_s = (cache_a_ckv[l], cache_a_krope[l], cache_b_k[l], cache_b_v[l], cache_b_logf[l],
                  cache_c_k[l], cache_c_v[l], state_ffn_conv[l])
        ys, st_s = _run_layer(ys, lw, tabs_s, past_s, n_streams=bs, t=ts, lam_init=lam_init,
                              post_tm=post_tm, ffn_tm=256)
        states_p.append(st_p)
        states_s.append(st_s)
    outs_p = [jnp.stack(s) for s in zip(*states_p)]
    outs_s = [jnp.stack(s) for s in zip(*states_s)]
    return (yp.reshape(bp, tp, d), ys.reshape(bs, ts, d), *outs_p, *outs_s)
```

```python
import functools
import math

import jax
import jax.numpy as jnp
from jax import lax
from jax.experimental import pallas as pl
from jax.experimental.pallas import tpu as pltpu

F32 = jnp.float32
BF16 = jnp.bfloat16

D_MODEL = 2048
CHUNK = 64
ROPE_THETA = 500000.0
EPS = 1e-6
NEG_INF = -1e30
LOG2E = 1.4426950408889634

A_HEADS = 8
A_NOPE = 128
A_ROPE = 64
A_V = 128
A_QK = A_NOPE + A_ROPE
A_KV_RANK = 512
B_HEADS = 4
B_DIM = 128
C_HEADS = 4
C_QK = 64
C_V = 128
C_ROT = 16
D_FF = 5632
CONV_W = 3

LANES = 128
HALO_ROW = 8
VMEM_LIMIT = 56 * 1024 * 1024

Z_QN, Z_QR, Z_CKV, Z_BQ, Z_BK, Z_BV, Z_CQ, Z_CK, Z_CV, Z_KR, Z_BF, Z_COLS = (
    0, 1024, 1536, 2048, 2560, 3072, 3584, 4096, 4608, 5120, 5248, 5376)
Z_TN = 896
ATTN_TQ = 1024
ATTN_TK = 512
ATTN_SUB = 256


def _params(sem):
    return pltpu.CompilerParams(dimension_semantics=sem, vmem_limit_bytes=VMEM_LIMIT)


def _dot(a, b):
    return jnp.dot(a, b, preferred_element_type=F32)


def _dot_nt(a, b):
    return lax.dot_general(a, b, (((1,), (1,)), ((), ())), preferred_element_type=F32)


def _lane_iota(shape):
    return lax.broadcasted_iota(jnp.int32, shape, len(shape) - 1)


def _permute_w_in_kernel(w_ref, o_ref):
    def put(dst, src, n):
        o_ref[:, dst:dst + n] = w_ref[0, :, src:src + n].astype(BF16)

    def clear(dst, n):
        o_ref[:, dst:dst + n] = jnp.zeros((o_ref.shape[0], n), BF16)

    for h in range(A_HEADS):
        put(Z_QN + h * A_NOPE, h * A_QK, A_NOPE)
        put(Z_QR + h * A_ROPE, h * A_QK + A_NOPE, A_ROPE)
    ckv0 = A_HEADS * A_QK
    kr0 = ckv0 + A_KV_RANK
    bq0 = kr0 + A_ROPE
    bf0 = bq0 + 3 * B_HEADS * B_DIM
    cq0 = bf0 + B_HEADS
    put(Z_CKV, ckv0, A_KV_RANK)
    put(Z_BQ, bq0, 3 * B_HEADS * B_DIM)
    put(Z_CQ, cq0, 3 * C_HEADS * C_V)
    put(Z_KR, kr0, A_ROPE)
    clear(Z_KR + A_ROPE, LANES - A_ROPE)
    put(Z_BF, bf0, B_HEADS)
    clear(Z_BF + B_HEADS, Z_COLS - Z_BF - B_HEADS)


def _permute_w_in(w_in, layer, tr):
    _, d, cols = w_in.shape
    return pl.pallas_call(
        _permute_w_in_kernel,
        grid=(d // tr,),
        in_specs=[pl.BlockSpec((1, tr, cols), lambda i: (layer, i, 0))],
        out_specs=pl.BlockSpec((tr, Z_COLS), lambda i: (i, 0)),
        out_shape=jax.ShapeDtypeStruct((d, Z_COLS), BF16),
        compiler_params=_params(("arbitrary",)),
        name="permute_w_in",
    )(w_in)


def _norm_mm_kernel(x_ref, g_ref, w_ref, o_ref, h_scr):
    @pl.when(pl.program_id(1) == 0)
    def _():
        x = x_ref[...]
        ms = jnp.mean(x * x, axis=-1, keepdims=True)
        h_scr[...] = (x * lax.rsqrt(ms + EPS) * g_ref[...]).astype(BF16)

    o_ref[...] = _dot(h_scr[...], w_ref[...])


def _norm_mm(x, g, w, tm, tn):
    m, d = x.shape
    n = w.shape[1]
    return pl.pallas_call(
        _norm_mm_kernel,
        grid=(m // tm, n // tn),
        in_specs=[pl.BlockSpec((tm, d), lambda i, j: (i, 0)),
                  pl.BlockSpec((1, d), lambda i, j: (0, 0)),
                  pl.BlockSpec((d, tn), lambda i, j: (0, j))],
        out_specs=pl.BlockSpec((tm, tn), lambda i, j: (i, j)),
        out_shape=jax.ShapeDtypeStruct((m, n), F32),
        scratch_shapes=[pltpu.VMEM((tm, d), BF16)],
        compiler_params=_params(("arbitrary", "arbitrary")),
        name="norm_mm",
    )(x, g, w)


def _seg_sumsq(x, bd):
    sq = x * x
    hi = sq.astype(BF16)
    lo = (sq - hi.astype(F32)).astype(BF16)
    return _dot(hi, bd) + _dot(lo, bd)


def _seg_norm(x, bd, seg, g):
    return x * lax.rsqrt(_seg_sumsq(x, bd) * (1.0 / seg) + EPS) * g


def _rope(x, cos, sin, half):
    n = x.shape[-1]
    first = (_lane_iota(x.shape) % 64) < half
    partner = jnp.where(first, pltpu.roll(x, n - half, 1), pltpu.roll(x, half, 1))
    return x * cos + partner * sin


def _post_kernel(z_ref, cosa_ref, sina_ref, cosc_ref, sinc_ref,
                 gqn_ref, gqr_ref, gckv_ref, gkr_ref, gkn_ref, gbq_ref, gbk_ref, gcq_ref, gck_ref,
                 bfg_ref, bd64_ref, bd128_ref, ones_ref, bdkr_ref, wk_ref, wv_ref,
                 aq_ref, kf_ref, va_ref, ckv_ref, kr_ref, bq_ref, bk32_ref, bk16_ref,
                 bv32_ref, bv16_ref, lf_ref, cq_ref, ck32_ref, ck16_ref, cv32_ref, cv16_ref):
    bd64 = bd64_ref[...]
    bd128 = bd128_ref[...]
    tm = z_ref.shape[0]
    lane = _lane_iota((tm, LANES))
    low = lane < 64

    cosa4 = jnp.concatenate([cosa_ref[...]] * 4, axis=1)
    sina4 = jnp.concatenate([sina_ref[...]] * 4, axis=1)
    cosc4 = jnp.concatenate([cosc_ref[...]] * 4, axis=1)
    sinc4 = jnp.concatenate([sinc_ref[...]] * 4, axis=1)

    a_scale = A_QK ** -0.5 * LOG2E
    qr = _seg_norm(z_ref[:, Z_QR:Z_QR + 512], bd64, 64, gqr_ref[...])
    qr = _rope(qr, cosa4, sina4, A_ROPE // 2) * a_scale
    for half in range(2):
        c0 = Z_QN + half * 512
        qn = _seg_norm(z_ref[:, c0:c0 + 512], bd128, 128, gqn_ref[:, half * 512:(half + 1) * 512]) * a_scale
        for hh in range(4):
            h = half * 4 + hh
            aq_ref[:, h * 256:h * 256 + 128] = qn[:, hh * 128:(hh + 1) * 128].astype(BF16)
    for h in range(A_HEADS):
        pair = qr[:, (h // 2) * 128:(h // 2 + 1) * 128]
        keep = low if h % 2 == 0 else jnp.logical_not(low)
        aq_ref[:, h * 256 + 128:h * 256 + 256] = jnp.where(keep, pair, 0.0).astype(BF16)

    ckv = _seg_norm(z_ref[:, Z_CKV:Z_CKV + 512], ones_ref[...], 512, gckv_ref[...])
    ckv_ref[...] = ckv
    ckv16 = ckv.astype(BF16)
    va_ref[...] = _dot(ckv16, wv_ref[...]).astype(BF16)

    kr = _seg_norm(z_ref[:, Z_KR:Z_KR + 128], bdkr_ref[...], 64, gkr_ref[...])
    kr = _rope(kr, cosa_ref[...], sina_ref[...], A_ROPE // 2)
    kr_ref[...] = kr[:, :A_ROPE]
    kr2 = kr + pltpu.roll(kr, 64, 1)
    kr_even = jnp.where(low, kr2, 0.0).astype(BF16)
    kr_odd = jnp.where(low, 0.0, kr2).astype(BF16)
    for half in range(2):
        kk = _dot(ckv16, wk_ref[:, half * 512:(half + 1) * 512])
        kn = _seg_norm(kk, bd128, 128, gkn_ref[:, half * 512:(half + 1) * 512])
        for hh in range(4):
            h = half * 4 + hh
            kf_ref[:, h * 256:h * 256 + 128] = kn[:, hh * 128:(hh + 1) * 128].astype(BF16)
            kf_ref[:, h * 256 + 128:h * 256 + 256] = kr_even if h % 2 == 0 else kr_odd

    bq = _seg_norm(z_ref[:, Z_BQ:Z_BQ + 512], bd128, 128, gbq_ref[...]) * (B_DIM ** -0.5 * LOG2E)
    bq_ref[...] = bq.astype(BF16)
    bk = _seg_norm(z_ref[:, Z_BK:Z_BK + 512], bd128, 128, gbk_ref[...])
    bk32_ref[...] = bk
    bk16_ref[...] = bk.astype(BF16)
    bv = z_ref[:, Z_BV:Z_BV + 512]
    bv32_ref[...] = bv
    bv16_ref[...] = bv.astype(BF16)
    f = z_ref[:, Z_BF:Z_BF + 128][:, 0:B_HEADS] + bfg_ref[:, 0:B_HEADS]
    lf_ref[...] = jnp.minimum(f, 0.0) - jnp.log1p(jnp.exp(-jnp.abs(f)))

    cq = _seg_norm(z_ref[:, Z_CQ:Z_CQ + 512], bd64, 64, gcq_ref[...])
    cq_ref[...] = (_rope(cq, cosc4, sinc4, C_ROT // 2) * (C_QK ** -0.5 * LOG2E)).astype(BF16)
    ck = _seg_norm(z_ref[:, Z_CK:Z_CK + 512], bd64, 64, gck_ref[...])
    ck = _rope(ck, cosc4, sinc4, C_ROT // 2)
    ck32_ref[...] = ck
    ck16_ref[...] = ck.astype(BF16)
    cv = z_ref[:, Z_CV:Z_CV + 512]
    cv32_ref[...] = cv
    cv16_ref[...] = cv.astype(BF16)


def _post(z, tabs, consts, tm):
    m = z.shape[0]
    tab_rows = tabs[0].shape[0]
    tab_blocks = tab_rows // tm

    def row(w):
        return pl.BlockSpec((tm, w), lambda i: (i, 0))

    def full(a):
        return pl.BlockSpec(a.shape, lambda i: (0, 0))

    tab_spec = pl.BlockSpec((tm, LANES), lambda i: (i % tab_blocks, 0))
    widths = [(2048, BF16), (2048, BF16), (1024, BF16), (512, F32), (64, F32), (512, BF16), (512, F32),
              (512, BF16), (512, F32), (512, BF16), (B_HEADS, F32), (512, BF16), (512, F32), (512, BF16),
              (512, F32), (512, BF16)]
    return pl.pallas_call(
        _post_kernel,
        grid=(m // tm,),
        in_specs=[row(Z_COLS)] + [tab_spec] * 4 + [full(c) for c in consts],
        out_specs=[row(w) for w, _ in widths],
        out_shape=[jax.ShapeDtypeStruct((m, w), dt) for w, dt in widths],
        compiler_params=_params(("arbitrary",)),
        name="post",
    )(z, *tabs, *consts)


def _split3(c):
    hi = c.astype(BF16).astype(F32)
    r1 = c - hi
    mid = r1.astype(BF16).astype(F32)
    lo = r1 - mid
    return hi, mid, lo


def _cumsum_block(lf, carry):
    n = lf.shape[0]
    r = lax.broadcasted_iota(jnp.int32, (n, n), 0)
    c = lax.broadcasted_iota(jnp.int32, (n, n), 1)
    tri = jnp.where(c <= r, 1.0, 0.0).astype(BF16)
    hi, mid, lo = _split3(lf)
    cum = _dot(tri, hi.astype(BF16)) + _dot(tri, mid.astype(BF16)) + _dot(tri, lo.astype(BF16))
    return cum + carry


def _aug_q(c, lane):
    hi, mid, lo = _split3(c)
    return jnp.where(lane == 0, hi, jnp.where(lane == 1, mid, jnp.where(lane == 2, lo,
                     jnp.where(lane < 6, 1.0, 0.0)))).astype(BF16)


def _aug_k(c, lane):
    hi, mid, lo = _split3(c)
    return jnp.where(lane < 3, 1.0, jnp.where(lane == 3, -hi, jnp.where(lane == 4, -mid,
                     jnp.where(lane == 5, -lo, 0.0)))).astype(BF16)


def _fox_prep_kernel(lf_ref, bq_ref, bk_ref, *out_refs, past, blocks):
    carry_scr = out_refs[-1]
    if past:
        qa_ref, ka_ref, kpast_ref = out_refs[:-1]
    else:
        qa_ref, ka_ref = out_refs[:-1]

    @pl.when(pl.program_id(1) == 0)
    def _():
        carry_scr[...] = jnp.zeros(carry_scr.shape, F32)

    carry = carry_scr[...]
    for r0, n in blocks:
        cum = _cumsum_block(lf_ref[0, r0:r0 + n, :], carry)
        carry = cum[n - 1:n, :]
        lane = _lane_iota((n, LANES))
        for h in range(B_HEADS):
            c = cum[:, h:h + 1] * LOG2E
            if r0 < past:
                kpast_ref[0, r0:r0 + n, h * 128:(h + 1) * 128] = _aug_k(c, lane)
            else:
                t0 = r0 - past
                qa_ref[t0:t0 + n, h * 256:h * 256 + 128] = bq_ref[t0:t0 + n, h * 128:(h + 1) * 128]
                qa_ref[t0:t0 + n, h * 256 + 128:h * 256 + 256] = _aug_q(c, lane)
                ka_ref[t0:t0 + n, h * 256:h * 256 + 128] = bk_ref[t0:t0 + n, h * 128:(h + 1) * 128]
                ka_ref[t0:t0 + n, h * 256 + 128:h * 256 + 256] = _aug_k(c, lane)
    carry_scr[...] = carry


def _fox_prep(lf_all, bq, bk, t_new, rows, blocks):
    s, t_tot, _ = lf_all.shape
    past = t_tot - t_new
    nb = t_tot // rows
    new_rows = rows - past
    m = bq.shape[0]
    new_spec = lambda w: pl.BlockSpec((new_rows, w), lambda i, j: (i * nb + j, 0))
    out_specs = [new_spec(1024), new_spec(1024)]
    out_shape = [jax.ShapeDtypeStruct((m, 1024), BF16), jax.ShapeDtypeStruct((m, 1024), BF16)]
    if past:
        out_specs.append(pl.BlockSpec((1, past, 512), lambda i, j: (i, 0, 0)))
        out_shape.append(jax.ShapeDtypeStruct((s, past, 512), BF16))
    return pl.pallas_call(
        functools.partial(_fox_prep_kernel, past=past, blocks=blocks),
        grid=(s, nb),
        in_specs=[pl.BlockSpec((1, rows, B_HEADS), lambda i, j: (i, j, 0)), new_spec(512), new_spec(512)],
        out_specs=out_specs,
        out_shape=out_shape,
        scratch_shapes=[pltpu.VMEM((1, B_HEADS), F32)],
        compiler_params=_params(("arbitrary", "arbitrary")),
        name="fox_prep",
    )(lf_all, bq, bk)


def _lanes(x, n):
    return x if n == 1 else jnp.concatenate([x] * n, axis=1)


def _with_ones(v):
    return jnp.concatenate([v, jnp.ones_like(v)], axis=1)


def _softmax_start(rows):
    return jnp.full((rows, LANES), NEG_INF, F32), jnp.zeros((rows, 2 * LANES), F32)


def _softmax_step(state, s, v1):
    m_old, acc = state
    tk = s.shape[1]
    m_new = jnp.maximum(m_old, jnp.max(s, axis=-1, keepdims=True))
    alpha = jnp.exp2(m_old - m_new)
    m_full = _lanes(m_new, tk // LANES) if tk >= LANES else m_new[:, :tk]
    p = jnp.exp2(s - m_full)
    return m_new, _lanes(alpha, 2) * acc + _dot(p.astype(BF16), v1)


def _softmax_out(state):
    acc = state[1]
    return acc[:, :LANES] / acc[:, LANES:]


def _visible(kind, q0, k0, tq, tk):
    qp = q0 + lax.broadcasted_iota(jnp.int32, (tq, tk), 0)
    kp = k0 + lax.broadcasted_iota(jnp.int32, (tq, tk), 1)
    if kind == "chunk":
        return (kp // CHUNK) <= (qp // CHUNK)
    return kp <= qp


def _split_maps(q, n_maps):
    if n_maps == 1:
        return [q]
    low = _lane_iota(q.shape) < C_QK
    zero = jnp.zeros_like(q)
    return [jnp.where(low, q, zero), jnp.where(low, zero, q)]


def _diff_lambda(lq1_ref, lk1_ref, lq2_ref, lk2_ref, lam_init):
    s1 = jnp.sum(lq1_ref[...] * lk1_ref[...], axis=-1, keepdims=True)
    s2 = jnp.sum(lq2_ref[...] * lk2_ref[...], axis=-1, keepdims=True)
    return jnp.exp(s1) - jnp.exp(s2) + lam_init


def _attn_finish(states, extra_refs, lam_init):
    if len(states) == 1:
        return _softmax_out(states[0])
    lq1_ref, lk1_ref, lq2_ref, lk2_ref, gout_ref = extra_refs
    lam = _diff_lambda(lq1_ref, lk1_ref, lq2_ref, lk2_ref, lam_init)
    o = _softmax_out(states[0]) - lam * _softmax_out(states[1])
    ms = jnp.mean(o * o, axis=-1, keepdims=True)
    return o * lax.rsqrt(ms + EPS) * gout_ref[...] * (1.0 - lam_init)


def _attn_prompt_kernel(*refs, n_maps, mask, tq, tk, sub, lam_init):
    q_ref, k_ref, v_ref = refs[:3]
    extra_refs = refs[3:-3]
    o_ref, m_scr, acc_scr = refs[-3:]
    qi = pl.program_id(2)
    n_sub = tq // sub
    m0, acc0 = _softmax_start(sub)
    for i in range(n_maps):
        for r in range(n_sub):
            m_scr[i, r] = m0
            acc_scr[i, r] = acc0

    def block(k0, diag):
        k = k_ref[pl.ds(k0, tk), :]
        v1 = _with_ones(v_ref[pl.ds(k0, tk), :])
        todo = []
        for r in range(n_sub):
            vis = None
            if diag is not None:
                if diag * tk > r * sub + sub - 1:
                    continue
                if diag * tk + tk - 1 > r * sub:
                    vis = _visible(mask, r * sub, diag * tk, sub, tk)
            todo.append((r, vis))
        scores = {r: [_dot_nt(qm, k) for qm in _split_maps(q_ref[r * sub:(r + 1) * sub, :], n_maps)]
                  for r, _ in todo}
        for r, vis in todo:
            for i in range(n_maps):
                s = scores[r][i] if vis is None else jnp.where(vis, scores[r][i], NEG_INF)
                m_scr[i, r], acc_scr[i, r] = _softmax_step((m_scr[i, r], acc_scr[i, r]), s, v1)

    def full_block(kb, carry):
        block(pl.multiple_of(kb * tk, tk), None)
        return carry

    per_q = tq // tk
    lax.fori_loop(0, qi * per_q, full_block, 0)
    for d in range(per_q):
        block(pl.multiple_of((qi * per_q + d) * tk, tk), d)
    for r in range(n_sub):
        states = [(m_scr[i, r], acc_scr[i, r]) for i in range(n_maps)]
        o_ref[r * sub:(r + 1) * sub, :] = _attn_finish(states, extra_refs, lam_init).astype(o_ref.dtype)


def _attn_prompt(q, k, v, extras, *, batch, heads, t, dqk, n_maps, mask, lam_init=0.0):
    tq, tk, sub = ATTN_TQ, ATTN_TK, ATTN_SUB
    nq = t // tq
    dv = 128
    extra_specs = [pl.BlockSpec(e.shape, lambda b, h, i: (0, 0)) for e in extras]
    return pl.pallas_call(
        functools.partial(_attn_prompt_kernel, n_maps=n_maps, mask=mask, tq=tq, tk=tk, sub=sub,
                          lam_init=lam_init),
        grid=(batch, heads, nq),
        in_specs=[pl.BlockSpec((tq, dqk), lambda b, h, i: (b * nq + i, h)),
                  pl.BlockSpec((t, dqk), lambda b, h, i: (b, h)),
                  pl.BlockSpec((t, dv), lambda b, h, i: (b, h))] + extra_specs,
        out_specs=pl.BlockSpec((tq, dv), lambda b, h, i: (b * nq + i, h)),
        out_shape=jax.ShapeDtypeStruct((batch * t, heads * dv), BF16),
        scratch_shapes=[pltpu.VMEM((n_maps, tq // sub, sub, LANES), F32),
                        pltpu.VMEM((n_maps, tq // sub, sub, 2 * LANES), F32)],
        compiler_params=_params(("arbitrary", "arbitrary", "arbitrary")),
        name="attn_prompt_" + mask + str(n_maps),
    )(q, k, v, *extras)


def _mla_sample_kernel(q_ref, kf_ref, va_ref, ckv_ref, kr_ref, wukv_ref, gkn_ref, o_ref, *, past, t):
    ckv_p = ckv_ref[0, 0].astype(BF16)
    kr_p = kr_ref[0, 0].astype(BF16)
    vis = _visible("chunk", past, past, t, t)
    up = lambda h: _dot(ckv_p, wukv_ref[:, h * 256:(h + 1) * 256])
    kv_next = up(0)
    for h in range(A_HEADS):
        kv = kv_next
        if h + 1 < A_HEADS:
            kv_next = up(h + 1)
        kk = kv[:, :A_NOPE]
        ms = jnp.mean(kk * kk, axis=-1, keepdims=True)
        kn = (kk * lax.rsqrt(ms + EPS) * gkn_ref[...]).astype(BF16)
        vp = kv[:, A_NOPE:].astype(BF16)
        q = q_ref[:, h * 256:(h + 1) * 256]
        r0 = 128 + (h % 2) * A_ROPE
        s = _dot_nt(q[:, :128], kn) + _dot_nt(q[:, r0:r0 + A_ROPE], kr_p)
        state = _softmax_step(_softmax_start(t), s, _with_ones(vp))
        s = jnp.where(vis, _dot_nt(q, kf_ref[:, h * 256:(h + 1) * 256]), NEG_INF)
        state = _softmax_step(state, s, _with_ones(va_ref[:, h * 128:(h + 1) * 128]))
        o_ref[:, h * 128:(h + 1) * 128] = _softmax_out(state).astype(o_ref.dtype)


def _mla_sample(q, kf, va, cache_ckv, cache_kr, w_ukv, gkn, *, layer, t):
    _, s, past, rank = cache_ckv.shape
    return pl.pallas_call(
        functools.partial(_mla_sample_kernel, past=past, t=t),
        grid=(s,),
        in_specs=[pl.BlockSpec((t, 2048), lambda i: (i, 0)),
                  pl.BlockSpec((t, 2048), lambda i: (i, 0)),
                  pl.BlockSpec((t, 1024), lambda i: (i, 0)),
                  pl.BlockSpec((1, 1, past, rank), lambda i: (layer, i, 0, 0)),
                  pl.BlockSpec((1, 1, past, A_ROPE), lambda i: (layer, i, 0, 0)),
                  pl.BlockSpec(w_ukv.shape, lambda i: (0, 0)),
                  pl.BlockSpec(gkn.shape, lambda i: (0, 0))],
        out_specs=pl.BlockSpec((t, 1024), lambda i: (i, 0)),
        out_shape=jax.ShapeDtypeStruct((s * t, 1024), BF16),
        compiler_params=_params(("arbitrary",)),
        name="mla_sample",
    )(q, kf, va, cache_ckv, cache_kr, w_ukv, gkn)


def _bc_sample_kernel(*refs, n_maps, mask, heads, past, t, lam_init, with_aug):
    q_ref, kn_ref, vn_ref, kp_ref, vp_ref = refs[:5]
    rest = refs[5:-1]
    o_ref = refs[-1]
    if with_aug:
        kaug_ref, extra_refs = rest[0], rest[1:]
    else:
        extra_refs = rest
    dq = q_ref.shape[1] // heads
    vis = _visible(mask, past, past, t, t)
    for h in range(heads):
        q = q_ref[:, h * dq:(h + 1) * dq]
        kp = kp_ref[0, 0, :, h * 128:(h + 1) * 128].astype(BF16)
        vp = _with_ones(vp_ref[0, 0, :, h * 128:(h + 1) * 128].astype(BF16))
        kn = kn_ref[:, h * dq:(h + 1) * dq]
        vn = _with_ones(vn_ref[:, h * 128:(h + 1) * 128])
        states = []
        if with_aug:
            s = _dot_nt(q[:, :128], kp) + _dot_nt(q[:, 128:], kaug_ref[0, :, h * 128:(h + 1) * 128])
            state = _softmax_step(_softmax_start(t), s, vp)
            states.append(_softmax_step(state, jnp.where(vis, _dot_nt(q, kn), NEG_INF), vn))
        else:
            for qm in _split_maps(q, n_maps):
                state = _softmax_step(_softmax_start(t), _dot_nt(qm, kp), vp)
                states.append(_softmax_step(state, jnp.where(vis, _dot_nt(qm, kn), NEG_INF), vn))
        o_ref[:, h * 128:(h + 1) * 128] = _attn_finish(states, extra_refs, lam_init).astype(o_ref.dtype)


def _bc_sample(q, kn, vn, cache_k, cache_v, kaug, extras, *, layer, heads, n_maps, mask, t, lam_init=0.0):
    _, s, past, _ = cache_k.shape
    dq = q.shape[1]
    cache_spec = pl.BlockSpec((1, 1, past, heads * 128), lambda i: (layer, i, 0, 0))
    in_specs = [pl.BlockSpec((t, dq), lambda i: (i, 0)),
                pl.BlockSpec((t, dq), lambda i: (i, 0)),
                pl.BlockSpec((t, heads * 128), lambda i: (i, 0)),
                cache_spec, cache_spec]
    args = [q, kn, vn, cache_k, cache_v]
    if kaug is not None:
        in_specs.append(pl.BlockSpec((1, past, heads * 128), lambda i: (i, 0, 0)))
        args.append(kaug)
    in_specs += [pl.BlockSpec(e.shape, lambda i: (0, 0)) for e in extras]
    return pl.pallas_call(
        functools.partial(_bc_sample_kernel, n_maps=n_maps, mask=mask, heads=heads, past=past, t=t,
                          lam_init=lam_init, with_aug=kaug is not None),
        grid=(s,),
        in_specs=in_specs,
        out_specs=pl.BlockSpec((t, heads * 128), lambda i: (i, 0)),
        out_shape=jax.ShapeDtypeStruct((s * t, heads * 128), BF16),
        compiler_params=_params(("arbitrary",)),
        name="sample_" + mask + str(n_maps),
    )(*args, *extras)


def _mm_res_kernel(*refs, n_in):
    res_ref = refs[2 * n_in]
    o_ref = refs[2 * n_in + 1]
    acc = res_ref[...]
    for i in range(n_in):
        acc = acc + _dot(refs[i][...], refs[n_in + i][...])
    o_ref[...] = acc


def _mm_res(res, a_list, w_list, tm, tn):
    m, n = res.shape
    n_in = len(a_list)
    in_specs = ([pl.BlockSpec((tm, a.shape[1]), lambda j, i: (i, 0)) for a in a_list]
                + [pl.BlockSpec((w.shape[0], tn), lambda j, i: (0, j)) for w in w_list]
                + [pl.BlockSpec((tm, tn), lambda j, i: (i, j))])
    return pl.pallas_call(
        functools.partial(_mm_res_kernel, n_in=n_in),
        grid=(n // tn, m // tm),
        in_specs=in_specs,
        out_specs=pl.BlockSpec((tm, tn), lambda j, i: (i, j)),
        out_shape=jax.ShapeDtypeStruct((m, n), F32),
        compiler_params=_params(("arbitrary", "arbitrary")),
        name="mm_res%d" % n_in,
    )(*a_list, *w_list, res)


def _ffn_up_kernel(x_ref, g_ref, wg_ref, wv_ref, cwg_ref, cwv_ref, cbg_ref, cbv_ref, sg_ref, sv_ref,
                   a_ref, ng_ref, nv_ref, h_scr, bufg, bufv, carg, carv, *, rows, tiles_per_stream):
    i = pl.program_id(0)
    j = pl.program_id(1)
    tm = x_ref.shape[0]
    n_sub = tm // rows

    @pl.when(j == 0)
    def _():
        x = x_ref[...]
        ms = jnp.mean(x * x, axis=-1, keepdims=True)
        h_scr[...] = (x * lax.rsqrt(ms + EPS) * g_ref[...]).astype(BF16)

    h = h_scr[...]
    halves = ((_dot(h, wg_ref[...]), bufg, carg, sg_ref, cwg_ref, cbg_ref, ng_ref),
              (_dot(h, wv_ref[...]), bufv, carv, sv_ref, cwv_ref, cbv_ref, nv_ref))
    ys = [[], []]
    for idx, (u, buf, car, s_ref, cw_ref, cb_ref, n_ref) in enumerate(halves):
        for s in range(n_sub):
            r0 = s * rows
            buf[HALO_ROW:HALO_ROW + rows, :] = u[r0:r0 + rows]
            if tiles_per_stream == 1:
                buf[HALO_ROW - 2:HALO_ROW, :] = s_ref[0, s]
            else:
                start = (i % tiles_per_stream) == 0

                @pl.when(start)
                def _():
                    buf[HALO_ROW - 2:HALO_ROW, :] = s_ref[0, 0]

                @pl.when(jnp.logical_not(start))
                def _():
                    buf[HALO_ROW - 2:HALO_ROW, :] = car[j]

            y = (cb_ref[...] + buf[HALO_ROW - 2:HALO_ROW - 2 + rows, :] * cw_ref[0:1, :]
                 + buf[HALO_ROW - 1:HALO_ROW - 1 + rows, :] * cw_ref[1:2, :]
                 + u[r0:r0 + rows] * cw_ref[2:3, :])
            ys[idx].append(y)
            n_ref[s] = u[r0 + rows - 2:r0 + rows]
        if tiles_per_stream > 1:
            car[j] = u[tm - 2:tm]
    for s in range(n_sub):
        gate = ys[0][s]
        a_ref[s * rows:(s + 1) * rows, :] = (gate * (1.0 / (1.0 + jnp.exp(-gate))) * ys[1][s]).astype(BF16)


def _ffn_up(x, g, w_up, conv_w, conv_b, state, *, layer, stream_len, tm, tn):
    m, d = x.shape
    nj = D_FF // tn
    if stream_len >= tm:
        rows, tiles_per_stream, n_sub = tm, stream_len // tm, 1
        stream_of = lambda i: i // tiles_per_stream
    else:
        rows, tiles_per_stream, n_sub = stream_len, 1, tm // stream_len
        stream_of = lambda i: i
    gcol = lambda i, j: (0, j)
    vcol = lambda i, j: (0, nj + j)
    outs = pl.pallas_call(
        functools.partial(_ffn_up_kernel, rows=rows, tiles_per_stream=tiles_per_stream),
        grid=(m // tm, nj),
        in_specs=[pl.BlockSpec((tm, d), lambda i, j: (i, 0)),
                  pl.BlockSpec((1, d), lambda i, j: (0, 0)),
                  pl.BlockSpec((d, tn), gcol), pl.BlockSpec((d, tn), vcol),
                  pl.BlockSpec((CONV_W, tn), gcol), pl.BlockSpec((CONV_W, tn), vcol),
                  pl.BlockSpec((1, tn), gcol), pl.BlockSpec((1, tn), vcol),
                  pl.BlockSpec((1, n_sub, 2, tn), lambda i, j: (layer, stream_of(i), 0, j)),
                  pl.BlockSpec((1, n_sub, 2, tn), lambda i, j: (layer, stream_of(i), 0, nj + j))],
        out_specs=[pl.BlockSpec((tm, tn), lambda i, j: (i, j)),
                   pl.BlockSpec((n_sub, 2, tn), lambda i, j: (i, 0, j)),
                   pl.BlockSpec((n_sub, 2, tn), lambda i, j: (i, 0, j))],
        out_shape=[jax.ShapeDtypeStruct((m, D_FF), BF16),
                   jax.ShapeDtypeStruct((m // rows, 2, D_FF), F32),
                   jax.ShapeDtypeStruct((m // rows, 2, D_FF), F32)],
        scratch_shapes=[pltpu.VMEM((tm, d), BF16),
                        pltpu.VMEM((HALO_ROW + rows, tn), F32), pltpu.VMEM((HALO_ROW + rows, tn), F32),
                        pltpu.VMEM((nj, 2, tn), F32), pltpu.VMEM((nj, 2, tn), F32)],
        compiler_params=_params(("arbitrary", "arbitrary")),
        name="ffn_up",
    )(x, g, w_up, w_up, conv_w, conv_w, conv_b, conv_b, state, state)
    a, tail_g, tail_v = outs
    last = slice(tiles_per_stream - 1, None, tiles_per_stream)
    return a, tail_g[last], tail_v[last]


def _rope_tables(pos, reps):
    posf = pos.astype(F32)[:, None]

    def tab(n_rot):
        half = n_rot // 2
        inv = jnp.power(jnp.float32(ROPE_THETA), -jnp.arange(half, dtype=F32) * (2.0 / n_rot))
        ang = posf * inv[None, :]
        rest = 64 - n_rot
        cos = jnp.concatenate([jnp.cos(ang), jnp.cos(ang), jnp.ones((pos.shape[0], rest), F32)], axis=1)
        sin = jnp.concatenate([-jnp.sin(ang), jnp.sin(ang), jnp.zeros((pos.shape[0], rest), F32)], axis=1)
        return jnp.tile(cos, (reps, 2)), jnp.tile(sin, (reps, 2))

    cosa, sina = tab(A_ROPE)
    cosc, sinc = tab(C_ROT)
    return cosa, sina, cosc, sinc


def _block_diag(width, seg):
    r = jnp.arange(width)[:, None] // seg
    c = jnp.arange(width)[None, :] // seg
    return (r == c).astype(BF16)


def _layer_weights(l, attn_norm, w_in, b_forget, a_kv_norm, a_w_ukv, a_qn_nope, a_qn_rope, a_kn_nope,
                   a_kn_rope, b_qn, b_kn, c_qn, c_kn, c_lq1, c_lk1, c_lq2, c_lk2, c_out_norm, w_out,
                   ffn_norm, w_up, conv_w, conv_b, w_down):
    w_in_p = _permute_w_in(w_in, l, 256)
    ukv = a_w_ukv[l].reshape(A_KV_RANK, A_HEADS, A_NOPE + A_V)
    wk = ukv[:, :, :A_NOPE].reshape(A_KV_RANK, 1024).astype(BF16)
    wv = ukv[:, :, A_NOPE:].reshape(A_KV_RANK, 1024).astype(BF16)
    row = lambda v: v.reshape(1, -1).astype(F32)
    tile = lambda v, n: jnp.tile(v.astype(F32), n).reshape(1, -1)
    bd_kr = (jnp.arange(128)[:, None] < 64).astype(BF16) * jnp.ones((1, 128), BF16)
    post_consts = [
        tile(a_qn_nope[l], 8), tile(a_qn_rope[l], 8), row(a_kv_norm[l]),
        jnp.concatenate([a_kn_rope[l], jnp.zeros((64,), F32)]).reshape(1, 128),
        tile(a_kn_nope[l], 8), tile(b_qn[l], 4), tile(b_kn[l], 4), tile(c_qn[l], 8), tile(c_kn[l], 8),
        jnp.concatenate([b_forget[l], jnp.zeros((128 - B_HEADS,), F32)]).reshape(1, 128),
        _block_diag(512, 64), _block_diag(512, 128), jnp.ones((512, 512), BF16), bd_kr, wk, wv]
    wo = w_out[l].astype(BF16)
    return dict(
        attn_norm=row(attn_norm[l]), w_in=w_in_p, post_consts=post_consts,
        w_ukv=a_w_ukv[l].astype(BF16), gkn=row(a_kn_nope[l]),
        diff_extras=[row(c_lq1[l]), row(c_lk1[l]), row(c_lq2[l]), row(c_lk2[l]), row(c_out_norm[l])],
        wo_a=wo[:1024], wo_b=wo[1024:1536], wo_c=wo[1536:],
        ffn_norm=row(ffn_norm[l]), w_up=w_up[l].astype(BF16), conv_w=conv_w[l].astype(F32),
        conv_b=row(conv_b[l]), w_down=w_down[l].astype(BF16))


def _run_layer(x, lw, tabs, past, *, n_streams, t, lam_init, post_tm, ffn_tm):
    m = x.shape[0]
    z = _norm_mm(x, lw["attn_norm"], lw["w_in"], 512, Z_TN)
    (aq, kf, va, ckv, kr, bq, bk32, bk16, bv32, bv16, lf, cq, ck32, ck16, cv32, cv16) = _post(
        z, tabs, lw["post_consts"], post_tm)

    if past is None:
        qa, ka = _fox_prep(lf.reshape(n_streams, t, B_HEADS), bq, bk16, t, 512, ((0, 512),))
        oa = _attn_prompt(aq, kf, va, [], batch=n_streams, heads=A_HEADS, t=t, dqk=256, n_maps=1,
                          mask="chunk")
        ob = _attn_prompt(qa, ka, bv16, [], batch=n_streams, heads=B_HEADS, t=t, dqk=256, n_maps=1,
                          mask="causal")
        oc = _attn_prompt(cq, ck16, cv16, lw["diff_extras"], batch=n_streams, heads=C_HEADS, t=t, dqk=128,
                          n_maps=2, mask="chunk", lam_init=lam_init)
        conv_state, conv_layer = jnp.zeros((1, n_streams, CONV_W - 1, 2 * D_FF), F32), 0
    else:
        src_layer, c_ckv, c_kr, c_bk, c_bv, c_lf, c_ck, c_cv, conv_state = past
        p_len = c_ckv.shape[2]
        c_ckv, c_kr, conv_state = c_ckv[src_layer][None], c_kr[src_layer][None], conv_state[src_layer][None]
        layer = conv_layer = 0
        rows = lambda c: c[src_layer].reshape(1, n_streams, p_len, c.shape[3] * c.shape[4])
        lf_all = jnp.concatenate([c_lf[src_layer], lf.reshape(n_streams, t, B_HEADS)], axis=1)
        blocks = tuple((r, 512) for r in range(0, p_len, 512)) + ((p_len, t),)
        qa, ka, kaug = _fox_prep(lf_all, bq, bk16, t, p_len + t, blocks)
        oa = _mla_sample(aq, kf, va, c_ckv, c_kr, lw["w_ukv"], lw["gkn"], layer=layer, t=t)
        ob = _bc_sample(qa, ka, bv16, rows(c_bk), rows(c_bv), kaug, [], layer=layer, heads=B_HEADS,
                        n_maps=1, mask="causal", t=t)
        oc = _bc_sample(cq, ck16, cv16, rows(c_ck), rows(c_cv), None, lw["diff_extras"], layer=layer,
                        heads=C_HEADS, n_maps=2, mask="chunk", t=t, lam_init=lam_init)

    x1 = _mm_res(x, [oa, ob, oc], [lw["wo_a"], lw["wo_b"], lw["wo_c"]], 512, 1024)
    a, conv_g, conv_v = _ffn_up(x1, lw["ffn_norm"], lw["w_up"], lw["conv_w"], lw["conv_b"], conv_state,
                                layer=conv_layer, stream_len=t, tm=ffn_tm, tn=512)
    x2 = _mm_res(x1, [a], [lw["w_down"]], 512, 512)
    states = (ckv.reshape(n_streams, t, A_KV_RANK), kr.reshape(n_streams, t, A_ROPE),
              bk32.reshape(n_streams, t, B_HEADS, B_DIM), bv32.reshape(n_streams, t, B_HEADS, B_DIM),
              lf.reshape(n_streams, t, B_HEADS),
              ck32.reshape(n_streams, t, C_HEADS, 2 * C_QK), cv32.reshape(n_streams, t, C_HEADS, C_V),
              jnp.concatenate([conv_g, conv_v], axis=-1))
    return x2, states


def kernel(x_prompt, x_sample, cache_a_ckv, cache_a_krope, cache_b_k, cache_b_v, cache_b_logf, cache_c_k, cache_c_v, state_ffn_conv, attn_norm, w_in, b_forget, a_kv_norm, a_w_ukv, a_qn_nope, a_qn_rope, a_kn_nope, a_kn_rope, b_qn, b_kn, c_qn, c_kn, c_lq1, c_lk1, c_lq2, c_lk2, c_out_norm, w_out, ffn_norm, w_up, conv_w, conv_b, w_down):
    bp, tp, d = x_prompt.shape
    bs, ts, _ = x_sample.shape
    depth = w_in.shape[0]
    past_len = cache_a_ckv.shape[2]
    post_tm = 256
    tabs_p = _rope_tables(jnp.arange(tp, dtype=jnp.int32), 1)
    tabs_s = _rope_tables(past_len + jnp.arange(ts, dtype=jnp.int32), post_tm // ts)

    yp = x_prompt.reshape(bp * tp, d)
    ys = x_sample.reshape(bs * ts, d)
    states_p, states_s = [], []
    for l in range(depth):
        lw = _layer_weights(l, attn_norm, w_in, b_forget, a_kv_norm, a_w_ukv, a_qn_nope, a_qn_rope,
                            a_kn_nope, a_kn_rope, b_qn, b_kn, c_qn, c_kn, c_lq1, c_lk1, c_lq2, c_lk2,
                            c_out_norm, w_out, ffn_norm, w_up, conv_w, conv_b, w_down)
        lam_init = 0.8 - 0.6 * math.exp(-0.3 * l)
        yp, st_p = _run_layer(yp, lw, tabs_p, None, n_streams=bp, t=tp, lam_init=lam_init,
                              post_tm=post_tm, ffn_tm=512)
        past_s = (l, cache_a_ckv, cache_a_krope, cache_b_k, cache_b_v, cache_b_logf,
                  cache_c_k, cache_c_v, state_ffn_conv)
        ys, st_s = _run_layer(ys, lw, tabs_s, past_s, n_streams=bs, t=ts, lam_init=lam_init,
                              post_tm=post_tm, ffn_tm=256)
        states_p.append(st_p)
        states_s.append(st_s)
    outs_p = [jnp.stack(s) for s in zip(*states_p)]
    outs_s = [jnp.stack(s) for s in zip(*states_s)]
    return (yp.reshape(bp, tp, d), ys.reshape(bs, ts, d), *outs_p, *outs_s)
```

```python
import functools
import math

import jax
import jax.numpy as jnp
from jax import lax
from jax.experimental import pallas as pl
from jax.experimental.pallas import tpu as pltpu

F32 = jnp.float32
BF16 = jnp.bfloat16

D_MODEL = 2048
CHUNK = 64
ROPE_THETA = 500000.0
EPS = 1e-6
NEG_INF = -1e30
LOG2E = 1.4426950408889634

A_HEADS = 8
A_NOPE = 128
A_ROPE = 64
A_V = 128
A_QK = A_NOPE + A_ROPE
A_KV_RANK = 512
B_HEADS = 4
B_DIM = 128
C_HEADS = 4
C_QK = 64
C_V = 128
C_ROT = 16
D_FF = 5632
CONV_W = 3

LANES = 128
HALO_ROW = 8
VMEM_LIMIT = 56 * 1024 * 1024

Z_QN, Z_QR, Z_CKV, Z_BQ, Z_BK, Z_BV, Z_CQ, Z_CK, Z_CV, Z_KR, Z_BF, Z_COLS = (
    0, 1024, 1536, 2048, 2560, 3072, 3584, 4096, 4608, 5120, 5248, 5376)
Z_TN = 896
FFN_CHUNK = 256
ATTN_TQ = 1024
ATTN_TK = 512
ATTN_SUB = 256


def _params(sem):
    return pltpu.CompilerParams(dimension_semantics=sem, vmem_limit_bytes=VMEM_LIMIT)


def _dot(a, b):
    return jnp.dot(a, b, preferred_element_type=F32)


def _dot_nt(a, b):
    return lax.dot_general(a, b, (((1,), (1,)), ((), ())), preferred_element_type=F32)


def _lane_iota(shape):
    return lax.broadcasted_iota(jnp.int32, shape, len(shape) - 1)


def _permute_w_in_kernel(w_ref, o_ref):
    def put(dst, src, n):
        o_ref[:, dst:dst + n] = w_ref[0, :, src:src + n].astype(BF16)

    def clear(dst, n):
        o_ref[:, dst:dst + n] = jnp.zeros((o_ref.shape[0], n), BF16)

    for h in range(A_HEADS):
        put(Z_QN + h * A_NOPE, h * A_QK, A_NOPE)
        put(Z_QR + h * A_ROPE, h * A_QK + A_NOPE, A_ROPE)
    ckv0 = A_HEADS * A_QK
    kr0 = ckv0 + A_KV_RANK
    bq0 = kr0 + A_ROPE
    bf0 = bq0 + 3 * B_HEADS * B_DIM
    cq0 = bf0 + B_HEADS
    put(Z_CKV, ckv0, A_KV_RANK)
    put(Z_BQ, bq0, 3 * B_HEADS * B_DIM)
    put(Z_CQ, cq0, 3 * C_HEADS * C_V)
    put(Z_KR, kr0, A_ROPE)
    clear(Z_KR + A_ROPE, LANES - A_ROPE)
    put(Z_BF, bf0, B_HEADS)
    clear(Z_BF + B_HEADS, Z_COLS - Z_BF - B_HEADS)


def _permute_w_in(w_in, layer, tr):
    _, d, cols = w_in.shape
    return pl.pallas_call(
        _permute_w_in_kernel,
        grid=(d // tr,),
        in_specs=[pl.BlockSpec((1, tr, cols), lambda i: (layer, i, 0))],
        out_specs=pl.BlockSpec((tr, Z_COLS), lambda i: (i, 0)),
        out_shape=jax.ShapeDtypeStruct((d, Z_COLS), BF16),
        compiler_params=_params(("arbitrary",)),
        name="permute_w_in",
    )(w_in)


def _norm_mm_kernel(x_ref, g_ref, w_ref, o_ref, h_scr):
    @pl.when(pl.program_id(1) == 0)
    def _():
        x = x_ref[...]
        ms = jnp.mean(x * x, axis=-1, keepdims=True)
        h_scr[...] = (x * lax.rsqrt(ms + EPS) * g_ref[...]).astype(BF16)

    o_ref[...] = _dot(h_scr[...], w_ref[...])


def _norm_mm(x, g, w, tm, tn):
    m, d = x.shape
    n = w.shape[1]
    return pl.pallas_call(
        _norm_mm_kernel,
        grid=(m // tm, n // tn),
        in_specs=[pl.BlockSpec((tm, d), lambda i, j: (i, 0)),
                  pl.BlockSpec((1, d), lambda i, j: (0, 0)),
                  pl.BlockSpec((d, tn), lambda i, j: (0, j))],
        out_specs=pl.BlockSpec((tm, tn), lambda i, j: (i, j)),
        out_shape=jax.ShapeDtypeStruct((m, n), F32),
        scratch_shapes=[pltpu.VMEM((tm, d), BF16)],
        compiler_params=_params(("arbitrary", "arbitrary")),
        name="norm_mm",
    )(x, g, w)


def _seg_sumsq(x, bd):
    sq = x * x
    hi = sq.astype(BF16)
    lo = (sq - hi.astype(F32)).astype(BF16)
    return _dot(hi, bd) + _dot(lo, bd)


def _seg_norm(x, bd, seg, g):
    return x * lax.rsqrt(_seg_sumsq(x, bd) * (1.0 / seg) + EPS) * g


def _rope(x, cos, sin, half):
    n = x.shape[-1]
    first = (_lane_iota(x.shape) % 64) < half
    partner = jnp.where(first, pltpu.roll(x, n - half, 1), pltpu.roll(x, half, 1))
    return x * cos + partner * sin


def _post_kernel(z_ref, cosa_ref, sina_ref, cosc_ref, sinc_ref,
                 gqn_ref, gqr_ref, gckv_ref, gkr_ref, gkn_ref, gbq_ref, gbk_ref, gcq_ref, gck_ref,
                 bfg_ref, bd64_ref, bd128_ref, ones_ref, bdkr_ref, wk_ref, wv_ref,
                 aq_ref, kf_ref, va_ref, ckv_ref, kr_ref, bq_ref, bk32_ref, bk16_ref,
                 bv32_ref, bv16_ref, lf_ref, cq_ref, ck32_ref, ck16_ref, cv32_ref, cv16_ref):
    bd64 = bd64_ref[...]
    bd128 = bd128_ref[...]
    tm = z_ref.shape[0]
    lane = _lane_iota((tm, LANES))
    low = lane < 64

    cosa4 = jnp.concatenate([cosa_ref[...]] * 4, axis=1)
    sina4 = jnp.concatenate([sina_ref[...]] * 4, axis=1)
    cosc4 = jnp.concatenate([cosc_ref[...]] * 4, axis=1)
    sinc4 = jnp.concatenate([sinc_ref[...]] * 4, axis=1)

    a_scale = A_QK ** -0.5 * LOG2E
    qr = _seg_norm(z_ref[:, Z_QR:Z_QR + 512], bd64, 64, gqr_ref[...])
    qr = _rope(qr, cosa4, sina4, A_ROPE // 2) * a_scale
    for half in range(2):
        c0 = Z_QN + half * 512
        qn = _seg_norm(z_ref[:, c0:c0 + 512], bd128, 128, gqn_ref[:, half * 512:(half + 1) * 512]) * a_scale
        for hh in range(4):
            h = half * 4 + hh
            aq_ref[:, h * 256:h * 256 + 128] = qn[:, hh * 128:(hh + 1) * 128].astype(BF16)
    for h in range(A_HEADS):
        pair = qr[:, (h // 2) * 128:(h // 2 + 1) * 128]
        keep = low if h % 2 == 0 else jnp.logical_not(low)
        aq_ref[:, h * 256 + 128:h * 256 + 256] = jnp.where(keep, pair, 0.0).astype(BF16)

    ckv = _seg_norm(z_ref[:, Z_CKV:Z_CKV + 512], ones_ref[...], 512, gckv_ref[...])
    ckv_ref[...] = ckv
    ckv16 = ckv.astype(BF16)
    va_ref[...] = _dot(ckv16, wv_ref[...]).astype(BF16)

    kr = _seg_norm(z_ref[:, Z_KR:Z_KR + 128], bdkr_ref[...], 64, gkr_ref[...])
    kr = _rope(kr, cosa_ref[...], sina_ref[...], A_ROPE // 2)
    kr_ref[...] = kr[:, :A_ROPE]
    kr2 = kr + pltpu.roll(kr, 64, 1)
    kr_even = jnp.where(low, kr2, 0.0).astype(BF16)
    kr_odd = jnp.where(low, 0.0, kr2).astype(BF16)
    for half in range(2):
        kk = _dot(ckv16, wk_ref[:, half * 512:(half + 1) * 512])
        kn = _seg_norm(kk, bd128, 128, gkn_ref[:, half * 512:(half + 1) * 512])
        for hh in range(4):
            h = half * 4 + hh
            kf_ref[:, h * 256:h * 256 + 128] = kn[:, hh * 128:(hh + 1) * 128].astype(BF16)
            kf_ref[:, h * 256 + 128:h * 256 + 256] = kr_even if h % 2 == 0 else kr_odd

    bq = _seg_norm(z_ref[:, Z_BQ:Z_BQ + 512], bd128, 128, gbq_ref[...]) * (B_DIM ** -0.5 * LOG2E)
    bq_ref[...] = bq.astype(BF16)
    bk = _seg_norm(z_ref[:, Z_BK:Z_BK + 512], bd128, 128, gbk_ref[...])
    bk32_ref[...] = bk
    bk16_ref[...] = bk.astype(BF16)
    bv = z_ref[:, Z_BV:Z_BV + 512]
    bv32_ref[...] = bv
    bv16_ref[...] = bv.astype(BF16)
    f = z_ref[:, Z_BF:Z_BF + 128][:, 0:B_HEADS] + bfg_ref[:, 0:B_HEADS]
    lf_ref[...] = jnp.minimum(f, 0.0) - jnp.log1p(jnp.exp(-jnp.abs(f)))

    cq = _seg_norm(z_ref[:, Z_CQ:Z_CQ + 512], bd64, 64, gcq_ref[...])
    cq_ref[...] = (_rope(cq, cosc4, sinc4, C_ROT // 2) * (C_QK ** -0.5 * LOG2E)).astype(BF16)
    ck = _seg_norm(z_ref[:, Z_CK:Z_CK + 512], bd64, 64, gck_ref[...])
    ck = _rope(ck, cosc4, sinc4, C_ROT // 2)
    ck32_ref[...] = ck
    ck16_ref[...] = ck.astype(BF16)
    cv = z_ref[:, Z_CV:Z_CV + 512]
    cv32_ref[...] = cv
    cv16_ref[...] = cv.astype(BF16)


def _post(z, tabs, consts, tm):
    m = z.shape[0]
    tab_rows = tabs[0].shape[0]
    tab_blocks = tab_rows // tm

    def row(w):
        return pl.BlockSpec((tm, w), lambda i: (i, 0))

    def full(a):
        return pl.BlockSpec(a.shape, lambda i: (0, 0))

    tab_spec = pl.BlockSpec((tm, LANES), lambda i: (i % tab_blocks, 0))
    widths = [(2048, BF16), (2048, BF16), (1024, BF16), (512, F32), (64, F32), (512, BF16), (512, F32),
              (512, BF16), (512, F32), (512, BF16), (B_HEADS, F32), (512, BF16), (512, F32), (512, BF16),
              (512, F32), (512, BF16)]
    return pl.pallas_call(
        _post_kernel,
        grid=(m // tm,),
        in_specs=[row(Z_COLS)] + [tab_spec] * 4 + [full(c) for c in consts],
        out_specs=[row(w) for w, _ in widths],
        out_shape=[jax.ShapeDtypeStruct((m, w), dt) for w, dt in widths],
        compiler_params=_params(("arbitrary",)),
        name="post",
    )(z, *tabs, *consts)


def _split3(c):
    hi = c.astype(BF16).astype(F32)
    r1 = c - hi
    mid = r1.astype(BF16).astype(F32)
    lo = r1 - mid
    return hi, mid, lo


def _cumsum_block(lf, carry):
    n = lf.shape[0]
    r = lax.broadcasted_iota(jnp.int32, (n, n), 0)
    c = lax.broadcasted_iota(jnp.int32, (n, n), 1)
    tri = jnp.where(c <= r, 1.0, 0.0).astype(BF16)
    hi, mid, lo = _split3(lf)
    cum = _dot(tri, hi.astype(BF16)) + _dot(tri, mid.astype(BF16)) + _dot(tri, lo.astype(BF16))
    return cum + carry


def _aug_q(c, lane):
    hi, mid, lo = _split3(c)
    return jnp.where(lane == 0, hi, jnp.where(lane == 1, mid, jnp.where(lane == 2, lo,
                     jnp.where(lane < 6, 1.0, 0.0)))).astype(BF16)


def _aug_k(c, lane):
    hi, mid, lo = _split3(c)
    return jnp.where(lane < 3, 1.0, jnp.where(lane == 3, -hi, jnp.where(lane == 4, -mid,
                     jnp.where(lane == 5, -lo, 0.0)))).astype(BF16)


def _fox_prep_kernel(lf_ref, bq_ref, bk_ref, *out_refs, past, blocks):
    carry_scr = out_refs[-1]
    if past:
        qa_ref, ka_ref, kpast_ref = out_refs[:-1]
    else:
        qa_ref, ka_ref = out_refs[:-1]

    @pl.when(pl.program_id(1) == 0)
    def _():
        carry_scr[...] = jnp.zeros(carry_scr.shape, F32)

    carry = carry_scr[...]
    for r0, n in blocks:
        cum = _cumsum_block(lf_ref[0, r0:r0 + n, :], carry)
        carry = cum[n - 1:n, :]
        lane = _lane_iota((n, LANES))
        for h in range(B_HEADS):
            c = cum[:, h:h + 1] * LOG2E
            if r0 < past:
                kpast_ref[0, r0:r0 + n, h * 128:(h + 1) * 128] = _aug_k(c, lane)
            else:
                t0 = r0 - past
                qa_ref[t0:t0 + n, h * 256:h * 256 + 128] = bq_ref[t0:t0 + n, h * 128:(h + 1) * 128]
                qa_ref[t0:t0 + n, h * 256 + 128:h * 256 + 256] = _aug_q(c, lane)
                ka_ref[t0:t0 + n, h * 256:h * 256 + 128] = bk_ref[t0:t0 + n, h * 128:(h + 1) * 128]
                ka_ref[t0:t0 + n, h * 256 + 128:h * 256 + 256] = _aug_k(c, lane)
    carry_scr[...] = carry


def _fox_prep(lf_all, bq, bk, t_new, rows, blocks):
    s, t_tot, _ = lf_all.shape
    past = t_tot - t_new
    nb = t_tot // rows
    new_rows = rows - past
    m = bq.shape[0]
    new_spec = lambda w: pl.BlockSpec((new_rows, w), lambda i, j: (i * nb + j, 0))
    out_specs = [new_spec(1024), new_spec(1024)]
    out_shape = [jax.ShapeDtypeStruct((m, 1024), BF16), jax.ShapeDtypeStruct((m, 1024), BF16)]
    if past:
        out_specs.append(pl.BlockSpec((1, past, 512), lambda i, j: (i, 0, 0)))
        out_shape.append(jax.ShapeDtypeStruct((s, past, 512), BF16))
    return pl.pallas_call(
        functools.partial(_fox_prep_kernel, past=past, blocks=blocks),
        grid=(s, nb),
        in_specs=[pl.BlockSpec((1, rows, B_HEADS), lambda i, j: (i, j, 0)), new_spec(512), new_spec(512)],
        out_specs=out_specs,
        out_shape=out_shape,
        scratch_shapes=[pltpu.VMEM((1, B_HEADS), F32)],
        compiler_params=_params(("arbitrary", "arbitrary")),
        name="fox_prep",
    )(lf_all, bq, bk)


def _lanes(x, n):
    return x if n == 1 else jnp.concatenate([x] * n, axis=1)


def _with_ones(v):
    return jnp.concatenate([v, jnp.ones_like(v)], axis=1)


def _softmax_start(rows):
    return jnp.full((rows, LANES), NEG_INF, F32), jnp.zeros((rows, 2 * LANES), F32)


def _softmax_step(state, s, v1):
    m_old, acc = state
    tk = s.shape[1]
    m_new = jnp.maximum(m_old, jnp.max(s, axis=-1, keepdims=True))
    alpha = jnp.exp2(m_old - m_new)
    m_full = _lanes(m_new, tk // LANES) if tk >= LANES else m_new[:, :tk]
    p = jnp.exp2(s - m_full)
    return m_new, _lanes(alpha, 2) * acc + _dot(p.astype(BF16), v1)


def _softmax_out(state):
    acc = state[1]
    return acc[:, :LANES] / acc[:, LANES:]


def _visible(kind, q0, k0, tq, tk):
    qp = q0 + lax.broadcasted_iota(jnp.int32, (tq, tk), 0)
    kp = k0 + lax.broadcasted_iota(jnp.int32, (tq, tk), 1)
    if kind == "chunk":
        return (kp // CHUNK) <= (qp // CHUNK)
    return kp <= qp


def _split_maps(q, n_maps):
    if n_maps == 1:
        return [q]
    low = _lane_iota(q.shape) < C_QK
    zero = jnp.zeros_like(q)
    return [jnp.where(low, q, zero), jnp.where(low, zero, q)]


def _diff_lambda(lq1_ref, lk1_ref, lq2_ref, lk2_ref, lam_init):
    s1 = jnp.sum(lq1_ref[...] * lk1_ref[...], axis=-1, keepdims=True)
    s2 = jnp.sum(lq2_ref[...] * lk2_ref[...], axis=-1, keepdims=True)
    return jnp.exp(s1) - jnp.exp(s2) + lam_init


def _attn_finish(states, extra_refs, lam_init):
    if len(states) == 1:
        return _softmax_out(states[0])
    lq1_ref, lk1_ref, lq2_ref, lk2_ref, gout_ref = extra_refs
    lam = _diff_lambda(lq1_ref, lk1_ref, lq2_ref, lk2_ref, lam_init)
    o = _softmax_out(states[0]) - lam * _softmax_out(states[1])
    ms = jnp.mean(o * o, axis=-1, keepdims=True)
    return o * lax.rsqrt(ms + EPS) * gout_ref[...] * (1.0 - lam_init)


def _attn_prompt_kernel(*refs, n_maps, mask, tq, tk, sub, lam_init):
    q_ref, k_ref, v_ref = refs[:3]
    extra_refs = refs[3:-3]
    o_ref, m_scr, acc_scr = refs[-3:]
    qi = pl.program_id(2)
    n_sub = tq // sub
    m0, acc0 = _softmax_start(sub)
    for i in range(n_maps):
        for r in range(n_sub):
            m_scr[i, r] = m0
            acc_scr[i, r] = acc0

    def block(k0, diag):
        k = k_ref[pl.ds(k0, tk), :]
        v1 = _with_ones(v_ref[pl.ds(k0, tk), :])
        todo = []
        for r in range(n_sub):
            vis = None
            if diag is not None:
                if diag * tk > r * sub + sub - 1:
                    continue
                if diag * tk + tk - 1 > r * sub:
                    vis = _visible(mask, r * sub, diag * tk, sub, tk)
            todo.append((r, vis))
        scores = {r: [_dot_nt(qm, k) for qm in _split_maps(q_ref[r * sub:(r + 1) * sub, :], n_maps)]
                  for r, _ in todo}
        for r, vis in todo:
            for i in range(n_maps):
                s = scores[r][i] if vis is None else jnp.where(vis, scores[r][i], NEG_INF)
                m_scr[i, r], acc_scr[i, r] = _softmax_step((m_scr[i, r], acc_scr[i, r]), s, v1)

    def full_block(kb, carry):
        block(pl.multiple_of(kb * tk, tk), None)
        return carry

    per_q = tq // tk
    lax.fori_loop(0, qi * per_q, full_block, 0)
    for d in range(per_q):
        block(pl.multiple_of((qi * per_q + d) * tk, tk), d)
    for r in range(n_sub):
        states = [(m_scr[i, r], acc_scr[i, r]) for i in range(n_maps)]
        o_ref[r * sub:(r + 1) * sub, :] = _attn_finish(states, extra_refs, lam_init).astype(o_ref.dtype)


def _attn_prompt(q, k, v, extras, *, batch, heads, t, dqk, n_maps, mask, lam_init=0.0):
    tq, tk, sub = ATTN_TQ, ATTN_TK, ATTN_SUB
    nq = t // tq
    dv = 128
    extra_specs = [pl.BlockSpec(e.shape, lambda b, h, i: (0, 0)) for e in extras]
    return pl.pallas_call(
        functools.partial(_attn_prompt_kernel, n_maps=n_maps, mask=mask, tq=tq, tk=tk, sub=sub,
                          lam_init=lam_init),
        grid=(batch, heads, nq),
        in_specs=[pl.BlockSpec((tq, dqk), lambda b, h, i: (b * nq + i, h)),
                  pl.BlockSpec((t, dqk), lambda b, h, i: (b, h)),
                  pl.BlockSpec((t, dv), lambda b, h, i: (b, h))] + extra_specs,
        out_specs=pl.BlockSpec((tq, dv), lambda b, h, i: (b * nq + i, h)),
        out_shape=jax.ShapeDtypeStruct((batch * t, heads * dv), BF16),
        scratch_shapes=[pltpu.VMEM((n_maps, tq // sub, sub, LANES), F32),
                        pltpu.VMEM((n_maps, tq // sub, sub, 2 * LANES), F32)],
        compiler_params=_params(("arbitrary", "arbitrary", "arbitrary")),
        name="attn_prompt_" + mask + str(n_maps),
    )(q, k, v, *extras)


def _mla_sample_kernel(q_ref, kf_ref, va_ref, ckv_ref, kr_ref, wukv_ref, gkn_ref, o_ref, *, past, t):
    ckv_p = ckv_ref[0, 0]
    kr_p = kr_ref[0, 0]
    vis = _visible("chunk", past, past, t, t)
    up = lambda h: _dot(ckv_p, wukv_ref[:, h * 256:(h + 1) * 256])
    kv_next = up(0)
    for h in range(A_HEADS):
        kv = kv_next
        if h + 1 < A_HEADS:
            kv_next = up(h + 1)
        kk = kv[:, :A_NOPE]
        ms = jnp.mean(kk * kk, axis=-1, keepdims=True)
        kn = (kk * lax.rsqrt(ms + EPS) * gkn_ref[...]).astype(BF16)
        vp = kv[:, A_NOPE:].astype(BF16)
        q = q_ref[:, h * 256:(h + 1) * 256]
        r0 = 128 + (h % 2) * A_ROPE
        s = _dot_nt(q[:, :128], kn) + _dot_nt(q[:, r0:r0 + A_ROPE], kr_p)
        state = _softmax_step(_softmax_start(t), s, _with_ones(vp))
        s = jnp.where(vis, _dot_nt(q, kf_ref[:, h * 256:(h + 1) * 256]), NEG_INF)
        state = _softmax_step(state, s, _with_ones(va_ref[:, h * 128:(h + 1) * 128]))
        o_ref[:, h * 128:(h + 1) * 128] = _softmax_out(state).astype(o_ref.dtype)


def _mla_sample(q, kf, va, cache_ckv, cache_kr, w_ukv, gkn, *, layer, t):
    _, s, past, rank = cache_ckv.shape
    return pl.pallas_call(
        functools.partial(_mla_sample_kernel, past=past, t=t),
        grid=(s,),
        in_specs=[pl.BlockSpec((t, 2048), lambda i: (i, 0)),
                  pl.BlockSpec((t, 2048), lambda i: (i, 0)),
                  pl.BlockSpec((t, 1024), lambda i: (i, 0)),
                  pl.BlockSpec((1, 1, past, rank), lambda i: (layer, i, 0, 0)),
                  pl.BlockSpec((1, 1, past, A_ROPE), lambda i: (layer, i, 0, 0)),
                  pl.BlockSpec(w_ukv.shape, lambda i: (0, 0)),
                  pl.BlockSpec(gkn.shape, lambda i: (0, 0))],
        out_specs=pl.BlockSpec((t, 1024), lambda i: (i, 0)),
        out_shape=jax.ShapeDtypeStruct((s * t, 1024), BF16),
        compiler_params=_params(("arbitrary",)),
        name="mla_sample",
    )(q, kf, va, cache_ckv, cache_kr, w_ukv, gkn)


def _bc_sample_kernel(*refs, n_maps, mask, heads, past, t, lam_init, with_aug):
    q_ref, kn_ref, vn_ref, kp_ref, vp_ref = refs[:5]
    rest = refs[5:-1]
    o_ref = refs[-1]
    if with_aug:
        kaug_ref, extra_refs = rest[0], rest[1:]
    else:
        extra_refs = rest
    dq = q_ref.shape[1] // heads
    vis = _visible(mask, past, past, t, t)
    for h in range(heads):
        q = q_ref[:, h * dq:(h + 1) * dq]
        kp = kp_ref[0, 0, :, h * 128:(h + 1) * 128]
        vp = _with_ones(vp_ref[0, 0, :, h * 128:(h + 1) * 128])
        kn = kn_ref[:, h * dq:(h + 1) * dq]
        vn = _with_ones(vn_ref[:, h * 128:(h + 1) * 128])
        states = []
        if with_aug:
            s = _dot_nt(q[:, :128], kp) + _dot_nt(q[:, 128:], kaug_ref[0, :, h * 128:(h + 1) * 128])
            state = _softmax_step(_softmax_start(t), s, vp)
            states.append(_softmax_step(state, jnp.where(vis, _dot_nt(q, kn), NEG_INF), vn))
        else:
            for qm in _split_maps(q, n_maps):
                state = _softmax_step(_softmax_start(t), _dot_nt(qm, kp), vp)
                states.append(_softmax_step(state, jnp.where(vis, _dot_nt(qm, kn), NEG_INF), vn))
        o_ref[:, h * 128:(h + 1) * 128] = _attn_finish(states, extra_refs, lam_init).astype(o_ref.dtype)


def _bc_sample(q, kn, vn, cache_k, cache_v, kaug, extras, *, layer, heads, n_maps, mask, t, lam_init=0.0):
    _, s, past, _ = cache_k.shape
    dq = q.shape[1]
    cache_spec = pl.BlockSpec((1, 1, past, heads * 128), lambda i: (layer, i, 0, 0))
    in_specs = [pl.BlockSpec((t, dq), lambda i: (i, 0)),
                pl.BlockSpec((t, dq), lambda i: (i, 0)),
                pl.BlockSpec((t, heads * 128), lambda i: (i, 0)),
                cache_spec, cache_spec]
    args = [q, kn, vn, cache_k, cache_v]
    if kaug is not None:
        in_specs.append(pl.BlockSpec((1, past, heads * 128), lambda i: (i, 0, 0)))
        args.append(kaug)
    in_specs += [pl.BlockSpec(e.shape, lambda i: (0, 0)) for e in extras]
    return pl.pallas_call(
        functools.partial(_bc_sample_kernel, n_maps=n_maps, mask=mask, heads=heads, past=past, t=t,
                          lam_init=lam_init, with_aug=kaug is not None),
        grid=(s,),
        in_specs=in_specs,
        out_specs=pl.BlockSpec((t, heads * 128), lambda i: (i, 0)),
        out_shape=jax.ShapeDtypeStruct((s * t, heads * 128), BF16),
        compiler_params=_params(("arbitrary",)),
        name="sample_" + mask + str(n_maps),
    )(*args, *extras)


def _mm_res_kernel(*refs, n_in):
    res_ref = refs[2 * n_in]
    o_ref = refs[2 * n_in + 1]
    acc = res_ref[...]
    for i in range(n_in):
        acc = acc + _dot(refs[i][...], refs[n_in + i][...])
    o_ref[...] = acc


def _mm_res(res, a_list, w_list, tm, tn):
    m, n = res.shape
    n_in = len(a_list)
    in_specs = ([pl.BlockSpec((tm, a.shape[1]), lambda j, i: (i, 0)) for a in a_list]
                + [pl.BlockSpec((w.shape[0], tn), lambda j, i: (0, j)) for w in w_list]
                + [pl.BlockSpec((tm, tn), lambda j, i: (i, j))])
    return pl.pallas_call(
        functools.partial(_mm_res_kernel, n_in=n_in),
        grid=(n // tn, m // tm),
        in_specs=in_specs,
        out_specs=pl.BlockSpec((tm, tn), lambda j, i: (i, j)),
        out_shape=jax.ShapeDtypeStruct((m, n), F32),
        compiler_params=_params(("arbitrary", "arbitrary")),
        name="mm_res%d" % n_in,
    )(*a_list, *w_list, res)


def _ffn_up_kernel(x_ref, g_ref, wg_ref, wv_ref, cwg_ref, cwv_ref, cbg_ref, cbv_ref, sg_ref, sv_ref,
                   a_ref, ng_ref, nv_ref, h_scr, bufg, bufv, carg, carv, *, rows, tiles_per_stream):
    i = pl.program_id(0)
    j = pl.program_id(1)
    tm = x_ref.shape[0]
    n_sub = tm // rows

    @pl.when(j == 0)
    def _():
        x = x_ref[...]
        ms = jnp.mean(x * x, axis=-1, keepdims=True)
        h_scr[...] = (x * lax.rsqrt(ms + EPS) * g_ref[...]).astype(BF16)

    h = h_scr[...]
    tn = a_ref.shape[1]
    chunks = [(c, c + FFN_CHUNK) for c in range(0, tn, FFN_CHUNK)]
    dots = [(_dot(h, wg_ref[:, c0:c1]), _dot(h, wv_ref[:, c0:c1])) for c0, c1 in chunks]
    for (c0, c1), us in zip(chunks, dots):
        sides = ((us[0], bufg, carg, sg_ref, cwg_ref, cbg_ref, ng_ref),
                 (us[1], bufv, carv, sv_ref, cwv_ref, cbv_ref, nv_ref))
        ys = [[], []]
        for idx, (u, buf, car, s_ref, cw_ref, cb_ref, n_ref) in enumerate(sides):
            for s in range(n_sub):
                r0 = s * rows
                buf[HALO_ROW:HALO_ROW + rows, c0:c1] = u[r0:r0 + rows]
                if tiles_per_stream == 1:
                    halo = s_ref[0, s, :, c0:c1]
                else:
                    halo = jnp.where((i % tiles_per_stream) == 0, s_ref[0, 0, :, c0:c1], car[j, :, c0:c1])
                buf[HALO_ROW - 2:HALO_ROW, c0:c1] = halo
                y = (cb_ref[:, c0:c1] + buf[HALO_ROW - 2:HALO_ROW - 2 + rows, c0:c1] * cw_ref[0:1, c0:c1]
                     + buf[HALO_ROW - 1:HALO_ROW - 1 + rows, c0:c1] * cw_ref[1:2, c0:c1]
                     + u[r0:r0 + rows] * cw_ref[2:3, c0:c1])
                ys[idx].append(y)
                n_ref[s, :, c0:c1] = u[r0 + rows - 2:r0 + rows]
            if tiles_per_stream > 1:
                car[j, :, c0:c1] = u[tm - 2:tm]
        for s in range(n_sub):
            gate = ys[0][s]
            a_ref[s * rows:(s + 1) * rows, c0:c1] = (gate * (1.0 / (1.0 + jnp.exp(-gate))) * ys[1][s]).astype(BF16)


def _ffn_up(x, g, w_up, conv_w, conv_b, state, *, layer, stream_len, tm, tn):
    m, d = x.shape
    nj = D_FF // tn
    if stream_len >= tm:
        rows, tiles_per_stream, n_sub = tm, stream_len // tm, 1
        stream_of = lambda i: i // tiles_per_stream
    else:
        rows, tiles_per_stream, n_sub = stream_len, 1, tm // stream_len
        stream_of = lambda i: i
    gcol = lambda i, j: (0, j)
    vcol = lambda i, j: (0, nj + j)
    outs = pl.pallas_call(
        functools.partial(_ffn_up_kernel, rows=rows, tiles_per_stream=tiles_per_stream),
        grid=(m // tm, nj),
        in_specs=[pl.BlockSpec((tm, d), lambda i, j: (i, 0)),
                  pl.BlockSpec((1, d), lambda i, j: (0, 0)),
                  pl.BlockSpec((d, tn), gcol), pl.BlockSpec((d, tn), vcol),
                  pl.BlockSpec((CONV_W, tn), gcol), pl.BlockSpec((CONV_W, tn), vcol),
                  pl.BlockSpec((1, tn), gcol), pl.BlockSpec((1, tn), vcol),
                  pl.BlockSpec((1, n_sub, 2, tn), lambda i, j: (layer, stream_of(i), 0, j)),
                  pl.BlockSpec((1, n_sub, 2, tn), lambda i, j: (layer, stream_of(i), 0, nj + j))],
        out_specs=[pl.BlockSpec((tm, tn), lambda i, j: (i, j)),
                   pl.BlockSpec((n_sub, 2, tn), lambda i, j: (i, 0, j)),
                   pl.BlockSpec((n_sub, 2, tn), lambda i, j: (i, 0, j))],
        out_shape=[jax.ShapeDtypeStruct((m, D_FF), BF16),
                   jax.ShapeDtypeStruct((m // rows, 2, D_FF), F32),
                   jax.ShapeDtypeStruct((m // rows, 2, D_FF), F32)],
        scratch_shapes=[pltpu.VMEM((tm, d), BF16),
                        pltpu.VMEM((HALO_ROW + rows, tn), F32), pltpu.VMEM((HALO_ROW + rows, tn), F32),
                        pltpu.VMEM((nj, 2, tn), F32), pltpu.VMEM((nj, 2, tn), F32)],
        compiler_params=_params(("arbitrary", "arbitrary")),
        name="ffn_up",
    )(x, g, w_up, w_up, conv_w, conv_w, conv_b, conv_b, state, state)
    a, tail_g, tail_v = outs
    last = slice(tiles_per_stream - 1, None, tiles_per_stream)
    return a, tail_g[last], tail_v[last]


def _rope_tables(pos, reps):
    posf = pos.astype(F32)[:, None]

    def tab(n_rot):
        half = n_rot // 2
        inv = jnp.power(jnp.float32(ROPE_THETA), -jnp.arange(half, dtype=F32) * (2.0 / n_rot))
        ang = posf * inv[None, :]
        rest = 64 - n_rot
        cos = jnp.concatenate([jnp.cos(ang), jnp.cos(ang), jnp.ones((pos.shape[0], rest), F32)], axis=1)
        sin = jnp.concatenate([-jnp.sin(ang), jnp.sin(ang), jnp.zeros((pos.shape[0], rest), F32)], axis=1)
        return jnp.tile(cos, (reps, 2)), jnp.tile(sin, (reps, 2))

    cosa, sina = tab(A_ROPE)
    cosc, sinc = tab(C_ROT)
    return cosa, sina, cosc, sinc


def _block_diag(width, seg):
    r = jnp.arange(width)[:, None] // seg
    c = jnp.arange(width)[None, :] // seg
    return (r == c).astype(BF16)


def _layer_weights(l, attn_norm, w_in, b_forget, a_kv_norm, a_w_ukv, a_qn_nope, a_qn_rope, a_kn_nope,
                   a_kn_rope, b_qn, b_kn, c_qn, c_kn, c_lq1, c_lk1, c_lq2, c_lk2, c_out_norm, w_out,
                   ffn_norm, w_up, conv_w, conv_b, w_down):
    w_in_p = _permute_w_in(w_in, l, 256)
    ukv = a_w_ukv[l].reshape(A_KV_RANK, A_HEADS, A_NOPE + A_V)
    wk = ukv[:, :, :A_NOPE].reshape(A_KV_RANK, 1024).astype(BF16)
    wv = ukv[:, :, A_NOPE:].reshape(A_KV_RANK, 1024).astype(BF16)
    row = lambda v: v.reshape(1, -1).astype(F32)
    tile = lambda v, n: jnp.tile(v.astype(F32), n).reshape(1, -1)
    bd_kr = (jnp.arange(128)[:, None] < 64).astype(BF16) * jnp.ones((1, 128), BF16)
    post_consts = [
        tile(a_qn_nope[l], 8), tile(a_qn_rope[l], 8), row(a_kv_norm[l]),
        jnp.concatenate([a_kn_rope[l], jnp.zeros((64,), F32)]).reshape(1, 128),
        tile(a_kn_nope[l], 8), tile(b_qn[l], 4), tile(b_kn[l], 4), tile(c_qn[l], 8), tile(c_kn[l], 8),
        jnp.concatenate([b_forget[l], jnp.zeros((128 - B_HEADS,), F32)]).reshape(1, 128),
        _block_diag(512, 64), _block_diag(512, 128), jnp.ones((512, 512), BF16), bd_kr, wk, wv]
    wo = w_out[l].astype(BF16)
    return dict(
        attn_norm=row(attn_norm[l]), w_in=w_in_p, post_consts=post_consts,
        w_ukv=a_w_ukv[l].astype(BF16), gkn=row(a_kn_nope[l]),
        diff_extras=[row(c_lq1[l]), row(c_lk1[l]), row(c_lq2[l]), row(c_lk2[l]), row(c_out_norm[l])],
        wo_a=wo[:1024], wo_b=wo[1024:1536], wo_c=wo[1536:],
        ffn_norm=row(ffn_norm[l]), w_up=w_up[l].astype(BF16), conv_w=conv_w[l].astype(F32),
        conv_b=row(conv_b[l]), w_down=w_down[l].astype(BF16))


def _run_layer(x, lw, tabs, past, *, n_streams, t, lam_init, post_tm, ffn_tm):
    m = x.shape[0]
    z = _norm_mm(x, lw["attn_norm"], lw["w_in"], 512, Z_TN)
    (aq, kf, va, ckv, kr, bq, bk32, bk16, bv32, bv16, lf, cq, ck32, ck16, cv32, cv16) = _post(
        z, tabs, lw["post_consts"], post_tm)

    if past is None:
        qa, ka = _fox_prep(lf.reshape(n_streams, t, B_HEADS), bq, bk16, t, 512, ((0, 512),))
        oa = _attn_prompt(aq, kf, va, [], batch=n_streams, heads=A_HEADS, t=t, dqk=256, n_maps=1,
                          mask="chunk")
        ob = _attn_prompt(qa, ka, bv16, [], batch=n_streams, heads=B_HEADS, t=t, dqk=256, n_maps=1,
                          mask="causal")
        oc = _attn_prompt(cq, ck16, cv16, lw["diff_extras"], batch=n_streams, heads=C_HEADS, t=t, dqk=128,
                          n_maps=2, mask="chunk", lam_init=lam_init)
        conv_state, conv_layer = jnp.zeros((1, n_streams, CONV_W - 1, 2 * D_FF), F32), 0
    else:
        layer, c_ckv, c_kr, c_bk, c_bv, c_lf, c_ck, c_cv, conv_state = past
        p_len = c_ckv.shape[2]
        conv_state, conv_layer = conv_state[layer][None], 0
        lf_all = jnp.concatenate([c_lf[layer], lf.reshape(n_streams, t, B_HEADS)], axis=1)
        blocks = tuple((r, 512) for r in range(0, p_len, 512)) + ((p_len, t),)
        qa, ka, kaug = _fox_prep(lf_all, bq, bk16, t, p_len + t, blocks)
        oa = _mla_sample(aq, kf, va, c_ckv, c_kr, lw["w_ukv"], lw["gkn"], layer=layer, t=t)
        ob = _bc_sample(qa, ka, bv16, c_bk, c_bv, kaug, [], layer=layer, heads=B_HEADS,
                        n_maps=1, mask="causal", t=t)
        oc = _bc_sample(cq, ck16, cv16, c_ck, c_cv, None, lw["diff_extras"], layer=layer,
                        heads=C_HEADS, n_maps=2, mask="chunk", t=t, lam_init=lam_init)

    x1 = _mm_res(x, [oa, ob, oc], [lw["wo_a"], lw["wo_b"], lw["wo_c"]], 512, 1024)
    a, conv_g, conv_v = _ffn_up(x1, lw["ffn_norm"], lw["w_up"], lw["conv_w"], lw["conv_b"], conv_state,
                                layer=conv_layer, stream_len=t, tm=ffn_tm, tn=512)
    x2 = _mm_res(x1, [a], [lw["w_down"]], 512, 512)
    states = (ckv.reshape(n_streams, t, A_KV_RANK), kr.reshape(n_streams, t, A_ROPE),
              bk32.reshape(n_streams, t, B_HEADS, B_DIM), bv32.reshape(n_streams, t, B_HEADS, B_DIM),
              lf.reshape(n_streams, t, B_HEADS),
              ck32.reshape(n_streams, t, C_HEADS, 2 * C_QK), cv32.reshape(n_streams, t, C_HEADS, C_V),
              jnp.concatenate([conv_g, conv_v], axis=-1))
    return x2, states


def kernel(x_prompt, x_sample, cache_a_ckv, cache_a_krope, cache_b_k, cache_b_v, cache_b_logf, cache_c_k, cache_c_v, state_ffn_conv, attn_norm, w_in, b_forget, a_kv_norm, a_w_ukv, a_qn_nope, a_qn_rope, a_kn_nope, a_kn_rope, b_qn, b_kn, c_qn, c_kn, c_lq1, c_lk1, c_lq2, c_lk2, c_out_norm, w_out, ffn_norm, w_up, conv_w, conv_b, w_down):
    bp, tp, d = x_prompt.shape
    bs, ts, _ = x_sample.shape
    depth = w_in.shape[0]
    past_len = cache_a_ckv.shape[2]
    post_tm = 256
    tabs_p = _rope_tables(jnp.arange(tp, dtype=jnp.int32), 1)
    tabs_s = _rope_tables(past_len + jnp.arange(ts, dtype=jnp.int32), post_tm // ts)

    merge_heads = lambda c: c.reshape(c.shape[:3] + (c.shape[3] * c.shape[4],)).astype(BF16)
    caches = (cache_a_ckv.astype(BF16), cache_a_krope.astype(BF16), merge_heads(cache_b_k),
              merge_heads(cache_b_v), cache_b_logf, merge_heads(cache_c_k), merge_heads(cache_c_v),
              state_ffn_conv)
    yp = x_prompt.reshape(bp * tp, d)
    ys = x_sample.reshape(bs * ts, d)
    states_p, states_s = [], []
    for l in range(depth):
        lw = _layer_weights(l, attn_norm, w_in, b_forget, a_kv_norm, a_w_ukv, a_qn_nope, a_qn_rope,
                            a_kn_nope, a_kn_rope, b_qn, b_kn, c_qn, c_kn, c_lq1, c_lk1, c_lq2, c_lk2,
                            c_out_norm, w_out, ffn_norm, w_up, conv_w, conv_b, w_down)
        lam_init = 0.8 - 0.6 * math.exp(-0.3 * l)
        yp, st_p = _run_layer(yp, lw, tabs_p, None, n_streams=bp, t=tp, lam_init=lam_init,
                              post_tm=post_tm, ffn_tm=512)
        ys, st_s = _run_layer(ys, lw, tabs_s, (l,) + caches, n_streams=bs, t=ts, lam_init=lam_init,
                              post_tm=post_tm, ffn_tm=256)
        states_p.append(st_p)
        states_s.append(st_s)
    outs_p = [jnp.stack(s) for s in zip(*states_p)]
    outs_s = [jnp.stack(s) for s in zip(*states_s)]
    return (yp.reshape(bp, tp, d), ys.reshape(bs, ts, d), *outs_p, *outs_s)
```

```python
import functools
import math

import jax
import jax.numpy as jnp
from jax import lax
from jax.experimental import pallas as pl
from jax.experimental.pallas import tpu as pltpu

F32 = jnp.float32
BF16 = jnp.bfloat16

D_MODEL = 2048
CHUNK = 64
ROPE_THETA = 500000.0
EPS = 1e-6
NEG_INF = -1e30
LOG2E = 1.4426950408889634

A_HEADS = 8
A_NOPE = 128
A_ROPE = 64
A_V = 128
A_QK = A_NOPE + A_ROPE
A_KV_RANK = 512
B_HEADS = 4
B_DIM = 128
C_HEADS = 4
C_QK = 64
C_V = 128
C_ROT = 16
D_FF = 5632
CONV_W = 3

LANES = 128
HALO_ROW = 8
VMEM_LIMIT = 56 * 1024 * 1024

Z_QN, Z_QR, Z_CKV, Z_BQ, Z_BK, Z_BV, Z_CQ, Z_CK, Z_CV, Z_KR, Z_BF, Z_COLS = (
    0, 1024, 1536, 2048, 2560, 3072, 3584, 4096, 4608, 5120, 5248, 5376)
Z_TN = 896
FFN_CHUNK = 256
FFN_ROWS = 32
ATTN_TQ = 1024
ATTN_TK = 512
ATTN_SUB = 256


def _params(sem):
    return pltpu.CompilerParams(dimension_semantics=sem, vmem_limit_bytes=VMEM_LIMIT)


def _dot(a, b):
    return jnp.dot(a, b, preferred_element_type=F32)


def _dot_nt(a, b):
    return lax.dot_general(a, b, (((1,), (1,)), ((), ())), preferred_element_type=F32)


def _lane_iota(shape):
    return lax.broadcasted_iota(jnp.int32, shape, len(shape) - 1)


def _permute_w_in_kernel(w_ref, o_ref):
    def put(dst, src, n):
        o_ref[:, dst:dst + n] = w_ref[0, :, src:src + n].astype(BF16)

    def clear(dst, n):
        o_ref[:, dst:dst + n] = jnp.zeros((o_ref.shape[0], n), BF16)

    for h in range(A_HEADS):
        put(Z_QN + h * A_NOPE, h * A_QK, A_NOPE)
        put(Z_QR + h * A_ROPE, h * A_QK + A_NOPE, A_ROPE)
    ckv0 = A_HEADS * A_QK
    kr0 = ckv0 + A_KV_RANK
    bq0 = kr0 + A_ROPE
    bf0 = bq0 + 3 * B_HEADS * B_DIM
    cq0 = bf0 + B_HEADS
    put(Z_CKV, ckv0, A_KV_RANK)
    put(Z_BQ, bq0, 3 * B_HEADS * B_DIM)
    put(Z_CQ, cq0, 3 * C_HEADS * C_V)
    put(Z_KR, kr0, A_ROPE)
    clear(Z_KR + A_ROPE, LANES - A_ROPE)
    put(Z_BF, bf0, B_HEADS)
    clear(Z_BF + B_HEADS, Z_COLS - Z_BF - B_HEADS)


def _permute_w_in(w_in, layer, tr):
    _, d, cols = w_in.shape
    return pl.pallas_call(
        _permute_w_in_kernel,
        grid=(d // tr,),
        in_specs=[pl.BlockSpec((1, tr, cols), lambda i: (layer, i, 0))],
        out_specs=pl.BlockSpec((tr, Z_COLS), lambda i: (i, 0)),
        out_shape=jax.ShapeDtypeStruct((d, Z_COLS), BF16),
        compiler_params=_params(("arbitrary",)),
        name="permute_w_in",
    )(w_in)


def _norm_mm_kernel(x_ref, g_ref, w_ref, o_ref, h_scr):
    @pl.when(pl.program_id(1) == 0)
    def _():
        x = x_ref[...]
        ms = jnp.mean(x * x, axis=-1, keepdims=True)
        h_scr[...] = (x * lax.rsqrt(ms + EPS) * g_ref[...]).astype(BF16)

    o_ref[...] = _dot(h_scr[...], w_ref[...])


def _norm_mm(x, g, w, tm, tn):
    m, d = x.shape
    n = w.shape[1]
    return pl.pallas_call(
        _norm_mm_kernel,
        grid=(m // tm, n // tn),
        in_specs=[pl.BlockSpec((tm, d), lambda i, j: (i, 0)),
                  pl.BlockSpec((1, d), lambda i, j: (0, 0)),
                  pl.BlockSpec((d, tn), lambda i, j: (0, j))],
        out_specs=pl.BlockSpec((tm, tn), lambda i, j: (i, j)),
        out_shape=jax.ShapeDtypeStruct((m, n), F32),
        scratch_shapes=[pltpu.VMEM((tm, d), BF16)],
        compiler_params=_params(("arbitrary", "arbitrary")),
        name="norm_mm",
    )(x, g, w)


def _seg_sumsq(x, bd):
    sq = x * x
    hi = sq.astype(BF16)
    lo = (sq - hi.astype(F32)).astype(BF16)
    return _dot(hi, bd) + _dot(lo, bd)


def _seg_norm(x, bd, seg, g):
    return x * lax.rsqrt(_seg_sumsq(x, bd) * (1.0 / seg) + EPS) * g


def _rope(x, cos, sin, half):
    n = x.shape[-1]
    first = (_lane_iota(x.shape) % 64) < half
    partner = jnp.where(first, pltpu.roll(x, n - half, 1), pltpu.roll(x, half, 1))
    return x * cos + partner * sin


def _post_kernel(z_ref, cosa_ref, sina_ref, cosc_ref, sinc_ref,
                 gqn_ref, gqr_ref, gckv_ref, gkr_ref, gkn_ref, gbq_ref, gbk_ref, gcq_ref, gck_ref,
                 bfg_ref, bd64_ref, bd128_ref, ones_ref, bdkr_ref, wk_ref, wv_ref,
                 aq_ref, kf_ref, va_ref, ckv_ref, kr_ref, bq_ref, bk32_ref, bk16_ref,
                 bv32_ref, bv16_ref, lf_ref, cq_ref, ck32_ref, ck16_ref, cv32_ref, cv16_ref):
    bd64 = bd64_ref[...]
    bd128 = bd128_ref[...]
    tm = z_ref.shape[0]
    lane = _lane_iota((tm, LANES))
    low = lane < 64

    cosa4 = jnp.concatenate([cosa_ref[...]] * 4, axis=1)
    sina4 = jnp.concatenate([sina_ref[...]] * 4, axis=1)
    cosc4 = jnp.concatenate([cosc_ref[...]] * 4, axis=1)
    sinc4 = jnp.concatenate([sinc_ref[...]] * 4, axis=1)

    a_scale = A_QK ** -0.5 * LOG2E
    qr = _seg_norm(z_ref[:, Z_QR:Z_QR + 512], bd64, 64, gqr_ref[...])
    qr = _rope(qr, cosa4, sina4, A_ROPE // 2) * a_scale
    for half in range(2):
        c0 = Z_QN + half * 512
        qn = _seg_norm(z_ref[:, c0:c0 + 512], bd128, 128, gqn_ref[:, half * 512:(half + 1) * 512]) * a_scale
        for hh in range(4):
            h = half * 4 + hh
            aq_ref[:, h * 256:h * 256 + 128] = qn[:, hh * 128:(hh + 1) * 128].astype(BF16)
    for h in range(A_HEADS):
        pair = qr[:, (h // 2) * 128:(h // 2 + 1) * 128]
        keep = low if h % 2 == 0 else jnp.logical_not(low)
        aq_ref[:, h * 256 + 128:h * 256 + 256] = jnp.where(keep, pair, 0.0).astype(BF16)

    ckv = _seg_norm(z_ref[:, Z_CKV:Z_CKV + 512], ones_ref[...], 512, gckv_ref[...])
    ckv_ref[...] = ckv
    ckv16 = ckv.astype(BF16)
    va_ref[...] = _dot(ckv16, wv_ref[...]).astype(BF16)

    kr = _seg_norm(z_ref[:, Z_KR:Z_KR + 128], bdkr_ref[...], 64, gkr_ref[...])
    kr = _rope(kr, cosa_ref[...], sina_ref[...], A_ROPE // 2)
    kr_ref[...] = kr[:, :A_ROPE]
    kr2 = kr + pltpu.roll(kr, 64, 1)
    kr_even = jnp.where(low, kr2, 0.0).astype(BF16)
    kr_odd = jnp.where(low, 0.0, kr2).astype(BF16)
    for half in range(2):
        kk = _dot(ckv16, wk_ref[:, half * 512:(half + 1) * 512])
        kn = _seg_norm(kk, bd128, 128, gkn_ref[:, half * 512:(half + 1) * 512])
        for hh in range(4):
            h = half * 4 + hh
            kf_ref[:, h * 256:h * 256 + 128] = kn[:, hh * 128:(hh + 1) * 128].astype(BF16)
            kf_ref[:, h * 256 + 128:h * 256 + 256] = kr_even if h % 2 == 0 else kr_odd

    bq = _seg_norm(z_ref[:, Z_BQ:Z_BQ + 512], bd128, 128, gbq_ref[...]) * (B_DIM ** -0.5 * LOG2E)
    bq_ref[...] = bq.astype(BF16)
    bk = _seg_norm(z_ref[:, Z_BK:Z_BK + 512], bd128, 128, gbk_ref[...])
    bk32_ref[...] = bk
    bk16_ref[...] = bk.astype(BF16)
    bv = z_ref[:, Z_BV:Z_BV + 512]
    bv32_ref[...] = bv
    bv16_ref[...] = bv.astype(BF16)
    f = z_ref[:, Z_BF:Z_BF + 128][:, 0:B_HEADS] + bfg_ref[:, 0:B_HEADS]
    lf_ref[...] = jnp.minimum(f, 0.0) - jnp.log1p(jnp.exp(-jnp.abs(f)))

    cq = _seg_norm(z_ref[:, Z_CQ:Z_CQ + 512], bd64, 64, gcq_ref[...])
    cq_ref[...] = (_rope(cq, cosc4, sinc4, C_ROT // 2) * (C_QK ** -0.5 * LOG2E)).astype(BF16)
    ck = _seg_norm(z_ref[:, Z_CK:Z_CK + 512], bd64, 64, gck_ref[...])
    ck = _rope(ck, cosc4, sinc4, C_ROT // 2)
    ck32_ref[...] = ck
    ck16_ref[...] = ck.astype(BF16)
    cv = z_ref[:, Z_CV:Z_CV + 512]
    cv32_ref[...] = cv
    cv16_ref[...] = cv.astype(BF16)


def _post(z, tabs, consts, tm):
    m = z.shape[0]
    tab_rows = tabs[0].shape[0]
    tab_blocks = tab_rows // tm

    def row(w):
        return pl.BlockSpec((tm, w), lambda i: (i, 0))

    def full(a):
        return pl.BlockSpec(a.shape, lambda i: (0, 0))

    tab_spec = pl.BlockSpec((tm, LANES), lambda i: (i % tab_blocks, 0))
    widths = [(2048, BF16), (2048, BF16), (1024, BF16), (512, F32), (64, F32), (512, BF16), (512, F32),
              (512, BF16), (512, F32), (512, BF16), (B_HEADS, F32), (512, BF16), (512, F32), (512, BF16),
              (512, F32), (512, BF16)]
    return pl.pallas_call(
        _post_kernel,
        grid=(m // tm,),
        in_specs=[row(Z_COLS)] + [tab_spec] * 4 + [full(c) for c in consts],
        out_specs=[row(w) for w, _ in widths],
        out_shape=[jax.ShapeDtypeStruct((m, w), dt) for w, dt in widths],
        compiler_params=_params(("arbitrary",)),
        name="post",
    )(z, *tabs, *consts)


def _split3(c):
    hi = c.astype(BF16).astype(F32)
    r1 = c - hi
    mid = r1.astype(BF16).astype(F32)
    lo = r1 - mid
    return hi, mid, lo


def _cumsum_block(lf, carry):
    n = lf.shape[0]
    r = lax.broadcasted_iota(jnp.int32, (n, n), 0)
    c = lax.broadcasted_iota(jnp.int32, (n, n), 1)
    tri = jnp.where(c <= r, 1.0, 0.0).astype(BF16)
    hi, mid, lo = _split3(lf)
    cum = _dot(tri, hi.astype(BF16)) + _dot(tri, mid.astype(BF16)) + _dot(tri, lo.astype(BF16))
    return cum + carry


def _aug_q(c, lane):
    hi, mid, lo = _split3(c)
    return jnp.where(lane == 0, hi, jnp.where(lane == 1, mid, jnp.where(lane == 2, lo,
                     jnp.where(lane < 6, 1.0, 0.0)))).astype(BF16)


def _aug_k(c, lane):
    hi, mid, lo = _split3(c)
    return jnp.where(lane < 3, 1.0, jnp.where(lane == 3, -hi, jnp.where(lane == 4, -mid,
                     jnp.where(lane == 5, -lo, 0.0)))).astype(BF16)


def _fox_prep_kernel(lf_ref, bq_ref, bk_ref, *out_refs, past, blocks):
    carry_scr = out_refs[-1]
    if past:
        qa_ref, ka_ref, kpast_ref = out_refs[:-1]
    else:
        qa_ref, ka_ref = out_refs[:-1]

    @pl.when(pl.program_id(1) == 0)
    def _():
        carry_scr[...] = jnp.zeros(carry_scr.shape, F32)

    carry = carry_scr[...]
    for r0, n in blocks:
        cum = _cumsum_block(lf_ref[0, r0:r0 + n, :], carry)
        carry = cum[n - 1:n, :]
        lane = _lane_iota((n, LANES))
        for h in range(B_HEADS):
            c = cum[:, h:h + 1] * LOG2E
            if r0 < past:
                kpast_ref[0, r0:r0 + n, h * 128:(h + 1) * 128] = _aug_k(c, lane)
            else:
                t0 = r0 - past
                qa_ref[t0:t0 + n, h * 256:h * 256 + 128] = bq_ref[t0:t0 + n, h * 128:(h + 1) * 128]
                qa_ref[t0:t0 + n, h * 256 + 128:h * 256 + 256] = _aug_q(c, lane)
                ka_ref[t0:t0 + n, h * 256:h * 256 + 128] = bk_ref[t0:t0 + n, h * 128:(h + 1) * 128]
                ka_ref[t0:t0 + n, h * 256 + 128:h * 256 + 256] = _aug_k(c, lane)
    carry_scr[...] = carry


def _fox_prep(lf_all, bq, bk, t_new, rows, blocks):
    s, t_tot, _ = lf_all.shape
    past = t_tot - t_new
    nb = t_tot // rows
    new_rows = rows - past
    m = bq.shape[0]
    new_spec = lambda w: pl.BlockSpec((new_rows, w), lambda i, j: (i * nb + j, 0))
    out_specs = [new_spec(1024), new_spec(1024)]
    out_shape = [jax.ShapeDtypeStruct((m, 1024), BF16), jax.ShapeDtypeStruct((m, 1024), BF16)]
    if past:
        out_specs.append(pl.BlockSpec((1, past, 512), lambda i, j: (i, 0, 0)))
        out_shape.append(jax.ShapeDtypeStruct((s, past, 512), BF16))
    return pl.pallas_call(
        functools.partial(_fox_prep_kernel, past=past, blocks=blocks),
        grid=(s, nb),
        in_specs=[pl.BlockSpec((1, rows, B_HEADS), lambda i, j: (i, j, 0)), new_spec(512), new_spec(512)],
        out_specs=out_specs,
        out_shape=out_shape,
        scratch_shapes=[pltpu.VMEM((1, B_HEADS), F32)],
        compiler_params=_params(("arbitrary", "arbitrary")),
        name="fox_prep",
    )(lf_all, bq, bk)


def _lanes(x, n):
    return x if n == 1 else jnp.concatenate([x] * n, axis=1)


def _with_ones(v):
    return jnp.concatenate([v, jnp.ones_like(v)], axis=1)


def _softmax_start(rows):
    return jnp.full((rows, LANES), NEG_INF, F32), jnp.zeros((rows, 2 * LANES), F32)


def _softmax_step(state, s, v1):
    m_old, acc = state
    tk = s.shape[1]
    m_new = jnp.maximum(m_old, jnp.max(s, axis=-1, keepdims=True))
    alpha = jnp.exp2(m_old - m_new)
    m_full = _lanes(m_new, tk // LANES) if tk >= LANES else m_new[:, :tk]
    p = jnp.exp2(s - m_full)
    return m_new, _lanes(alpha, 2) * acc + _dot(p.astype(BF16), v1)


def _softmax_out(state):
    acc = state[1]
    return acc[:, :LANES] / acc[:, LANES:]


def _visible(kind, q0, k0, tq, tk):
    qp = q0 + lax.broadcasted_iota(jnp.int32, (tq, tk), 0)
    kp = k0 + lax.broadcasted_iota(jnp.int32, (tq, tk), 1)
    if kind == "chunk":
        return (kp // CHUNK) <= (qp // CHUNK)
    return kp <= qp


def _split_maps(q, n_maps):
    if n_maps == 1:
        return [q]
    low = _lane_iota(q.shape) < C_QK
    zero = jnp.zeros_like(q)
    return [jnp.where(low, q, zero), jnp.where(low, zero, q)]


def _diff_lambda(lq1_ref, lk1_ref, lq2_ref, lk2_ref, lam_init):
    s1 = jnp.sum(lq1_ref[...] * lk1_ref[...], axis=-1, keepdims=True)
    s2 = jnp.sum(lq2_ref[...] * lk2_ref[...], axis=-1, keepdims=True)
    return jnp.exp(s1) - jnp.exp(s2) + lam_init


def _attn_finish(states, extra_refs, lam_init):
    if len(states) == 1:
        return _softmax_out(states[0])
    lq1_ref, lk1_ref, lq2_ref, lk2_ref, gout_ref = extra_refs
    lam = _diff_lambda(lq1_ref, lk1_ref, lq2_ref, lk2_ref, lam_init)
    o = _softmax_out(states[0]) - lam * _softmax_out(states[1])
    ms = jnp.mean(o * o, axis=-1, keepdims=True)
    return o * lax.rsqrt(ms + EPS) * gout_ref[...] * (1.0 - lam_init)


def _attn_prompt_kernel(*refs, n_maps, mask, tq, tk, sub, lam_init):
    q_ref, k_ref, v_ref = refs[:3]
    extra_refs = refs[3:-3]
    o_ref, m_scr, acc_scr = refs[-3:]
    qi = pl.program_id(2)
    n_sub = tq // sub
    m0, acc0 = _softmax_start(sub)
    for i in range(n_maps):
        for r in range(n_sub):
            m_scr[i, r] = m0
            acc_scr[i, r] = acc0

    def block(k0, diag):
        k = k_ref[pl.ds(k0, tk), :]
        v1 = _with_ones(v_ref[pl.ds(k0, tk), :])
        todo = []
        for r in range(n_sub):
            vis = None
            if diag is not None:
                if diag * tk > r * sub + sub - 1:
                    continue
                if diag * tk + tk - 1 > r * sub:
                    vis = _visible(mask, r * sub, diag * tk, sub, tk)
            todo.append((r, vis))
        scores = {r: [_dot_nt(qm, k) for qm in _split_maps(q_ref[r * sub:(r + 1) * sub, :], n_maps)]
                  for r, _ in todo}
        for r, vis in todo:
            for i in range(n_maps):
                s = scores[r][i] if vis is None else jnp.where(vis, scores[r][i], NEG_INF)
                m_scr[i, r], acc_scr[i, r] = _softmax_step((m_scr[i, r], acc_scr[i, r]), s, v1)

    def full_block(kb, carry):
        block(pl.multiple_of(kb * tk, tk), None)
        return carry

    per_q = tq // tk
    lax.fori_loop(0, qi * per_q, full_block, 0)
    for d in range(per_q):
        block(pl.multiple_of((qi * per_q + d) * tk, tk), d)
    for r in range(n_sub):
        states = [(m_scr[i, r], acc_scr[i, r]) for i in range(n_maps)]
        o_ref[r * sub:(r + 1) * sub, :] = _attn_finish(states, extra_refs, lam_init).astype(o_ref.dtype)


def _attn_prompt(q, k, v, extras, *, batch, heads, t, dqk, n_maps, mask, lam_init=0.0):
    tq, tk, sub = ATTN_TQ, ATTN_TK, ATTN_SUB
    nq = t // tq
    dv = 128
    extra_specs = [pl.BlockSpec(e.shape, lambda b, h, i: (0, 0)) for e in extras]
    return pl.pallas_call(
        functools.partial(_attn_prompt_kernel, n_maps=n_maps, mask=mask, tq=tq, tk=tk, sub=sub,
                          lam_init=lam_init),
        grid=(batch, heads, nq),
        in_specs=[pl.BlockSpec((tq, dqk), lambda b, h, i: (b * nq + i, h)),
                  pl.BlockSpec((t, dqk), lambda b, h, i: (b, h)),
                  pl.BlockSpec((t, dv), lambda b, h, i: (b, h))] + extra_specs,
        out_specs=pl.BlockSpec((tq, dv), lambda b, h, i: (b * nq + i, h)),
        out_shape=jax.ShapeDtypeStruct((batch * t, heads * dv), BF16),
        scratch_shapes=[pltpu.VMEM((n_maps, tq // sub, sub, LANES), F32),
                        pltpu.VMEM((n_maps, tq // sub, sub, 2 * LANES), F32)],
        compiler_params=_params(("arbitrary", "arbitrary", "arbitrary")),
        name="attn_prompt_" + mask + str(n_maps),
    )(q, k, v, *extras)


def _mla_sample_kernel(q_ref, kf_ref, va_ref, ckv_ref, kr_ref, wukv_ref, gkn_ref, o_ref, *, past, t):
    ckv_p = ckv_ref[0, 0]
    kr_p = kr_ref[0, 0]
    vis = _visible("chunk", past, past, t, t)
    up = lambda h: _dot(ckv_p, wukv_ref[:, h * 256:(h + 1) * 256])
    kv_next = up(0)
    for h in range(A_HEADS):
        kv = kv_next
        if h + 1 < A_HEADS:
            kv_next = up(h + 1)
        kk = kv[:, :A_NOPE]
        ms = jnp.mean(kk * kk, axis=-1, keepdims=True)
        kn = (kk * lax.rsqrt(ms + EPS) * gkn_ref[...]).astype(BF16)
        vp = kv[:, A_NOPE:].astype(BF16)
        q = q_ref[:, h * 256:(h + 1) * 256]
        r0 = 128 + (h % 2) * A_ROPE
        s = _dot_nt(q[:, :128], kn) + _dot_nt(q[:, r0:r0 + A_ROPE], kr_p)
        state = _softmax_step(_softmax_start(t), s, _with_ones(vp))
        s = jnp.where(vis, _dot_nt(q, kf_ref[:, h * 256:(h + 1) * 256]), NEG_INF)
        state = _softmax_step(state, s, _with_ones(va_ref[:, h * 128:(h + 1) * 128]))
        o_ref[:, h * 128:(h + 1) * 128] = _softmax_out(state).astype(o_ref.dtype)


def _mla_sample(q, kf, va, cache_ckv, cache_kr, w_ukv, gkn, *, layer, t):
    _, s, past, rank = cache_ckv.shape
    return pl.pallas_call(
        functools.partial(_mla_sample_kernel, past=past, t=t),
        grid=(s,),
        in_specs=[pl.BlockSpec((t, 2048), lambda i: (i, 0)),
                  pl.BlockSpec((t, 2048), lambda i: (i, 0)),
                  pl.BlockSpec((t, 1024), lambda i: (i, 0)),
                  pl.BlockSpec((1, 1, past, rank), lambda i: (layer, i, 0, 0)),
                  pl.BlockSpec((1, 1, past, A_ROPE), lambda i: (layer, i, 0, 0)),
                  pl.BlockSpec(w_ukv.shape, lambda i: (0, 0)),
                  pl.BlockSpec(gkn.shape, lambda i: (0, 0))],
        out_specs=pl.BlockSpec((t, 1024), lambda i: (i, 0)),
        out_shape=jax.ShapeDtypeStruct((s * t, 1024), BF16),
        compiler_params=_params(("arbitrary",)),
        name="mla_sample",
    )(q, kf, va, cache_ckv, cache_kr, w_ukv, gkn)


def _bc_sample_kernel(*refs, n_maps, mask, heads, past, t, lam_init, with_aug):
    q_ref, kn_ref, vn_ref, kp_ref, vp_ref = refs[:5]
    rest = refs[5:-1]
    o_ref = refs[-1]
    if with_aug:
        kaug_ref, extra_refs = rest[0], rest[1:]
    else:
        extra_refs = rest
    dq = q_ref.shape[1] // heads
    vis = _visible(mask, past, past, t, t)
    for h in range(heads):
        q = q_ref[:, h * dq:(h + 1) * dq]
        kp = kp_ref[0, 0, :, h * 128:(h + 1) * 128]
        vp = _with_ones(vp_ref[0, 0, :, h * 128:(h + 1) * 128])
        kn = kn_ref[:, h * dq:(h + 1) * dq]
        vn = _with_ones(vn_ref[:, h * 128:(h + 1) * 128])
        states = []
        if with_aug:
            s = _dot_nt(q[:, :128], kp) + _dot_nt(q[:, 128:], kaug_ref[0, :, h * 128:(h + 1) * 128])
            state = _softmax_step(_softmax_start(t), s, vp)
            states.append(_softmax_step(state, jnp.where(vis, _dot_nt(q, kn), NEG_INF), vn))
        else:
            for qm in _split_maps(q, n_maps):
                state = _softmax_step(_softmax_start(t), _dot_nt(qm, kp), vp)
                states.append(_softmax_step(state, jnp.where(vis, _dot_nt(qm, kn), NEG_INF), vn))
        o_ref[:, h * 128:(h + 1) * 128] = _attn_finish(states, extra_refs, lam_init).astype(o_ref.dtype)


def _bc_sample(q, kn, vn, cache_k, cache_v, kaug, extras, *, layer, heads, n_maps, mask, t, lam_init=0.0):
    _, s, past, _ = cache_k.shape
    dq = q.shape[1]
    cache_spec = pl.BlockSpec((1, 1, past, heads * 128), lambda i: (layer, i, 0, 0))
    in_specs = [pl.BlockSpec((t, dq), lambda i: (i, 0)),
                pl.BlockSpec((t, dq), lambda i: (i, 0)),
                pl.BlockSpec((t, heads * 128), lambda i: (i, 0)),
                cache_spec, cache_spec]
    args = [q, kn, vn, cache_k, cache_v]
    if kaug is not None:
        in_specs.append(pl.BlockSpec((1, past, heads * 128), lambda i: (i, 0, 0)))
        args.append(kaug)
    in_specs += [pl.BlockSpec(e.shape, lambda i: (0, 0)) for e in extras]
    return pl.pallas_call(
        functools.partial(_bc_sample_kernel, n_maps=n_maps, mask=mask, heads=heads, past=past, t=t,
                          lam_init=lam_init, with_aug=kaug is not None),
        grid=(s,),
        in_specs=in_specs,
        out_specs=pl.BlockSpec((t, heads * 128), lambda i: (i, 0)),
        out_shape=jax.ShapeDtypeStruct((s * t, heads * 128), BF16),
        compiler_params=_params(("arbitrary",)),
        name="sample_" + mask + str(n_maps),
    )(*args, *extras)


def _mm_res_kernel(*refs, n_in):
    res_ref = refs[2 * n_in]
    o_ref = refs[2 * n_in + 1]
    acc = res_ref[...]
    for i in range(n_in):
        acc = acc + _dot(refs[i][...], refs[n_in + i][...])
    o_ref[...] = acc


def _mm_res(res, a_list, w_list, tm, tn):
    m, n = res.shape
    n_in = len(a_list)
    in_specs = ([pl.BlockSpec((tm, a.shape[1]), lambda j, i: (i, 0)) for a in a_list]
                + [pl.BlockSpec((w.shape[0], tn), lambda j, i: (0, j)) for w in w_list]
                + [pl.BlockSpec((tm, tn), lambda j, i: (i, j))])
    return pl.pallas_call(
        functools.partial(_mm_res_kernel, n_in=n_in),
        grid=(n // tn, m // tm),
        in_specs=in_specs,
        out_specs=pl.BlockSpec((tm, tn), lambda j, i: (i, j)),
        out_shape=jax.ShapeDtypeStruct((m, n), F32),
        compiler_params=_params(("arbitrary", "arbitrary")),
        name="mm_res%d" % n_in,
    )(*a_list, *w_list, res)


def _ffn_up_kernel(x_ref, g_ref, wg_ref, wv_ref, cwg_ref, cwv_ref, cbg_ref, cbv_ref, sg_ref, sv_ref,
                   a_ref, ng_ref, nv_ref, h_scr, rawg0, rawv0, rawg1, rawv1, carg, carv,
                   *, rows, tiles_per_stream, nj):
    i = pl.program_id(0)
    j = pl.program_id(1)
    tm = x_ref.shape[0]
    n_sub = tm // rows
    tn = a_ref.shape[1]
    chunks = [(c, c + FFN_CHUNK) for c in range(0, tn, FFN_CHUNK)]
    raws = ((rawg0, rawv0), (rawg1, rawv1))

    def matmuls():
        h = h_scr[...]
        return [(_dot(h, wg_ref[:, c0:c1]), _dot(h, wv_ref[:, c0:c1])) for c0, c1 in chunks]

    def keep(dots, slot):
        for (c0, c1), us in zip(chunks, dots):
            for u, raw in zip(us, raws[slot]):
                for s in range(n_sub):
                    raw[s, HALO_ROW:HALO_ROW + rows, c0:c1] = u[s * rows:(s + 1) * rows]

    def epilogue(slot):
        jm = j - 1
        sides = ((raws[slot][0], carg, sg_ref, cwg_ref, cbg_ref, ng_ref),
                 (raws[slot][1], carv, sv_ref, cwv_ref, cbv_ref, nv_ref))
        for raw, car, s_ref, _, _, n_ref in sides:
            for s in range(n_sub):
                if tiles_per_stream == 1:
                    halo = s_ref[0, s]
                else:
                    halo = jnp.where((i % tiles_per_stream) == 0, s_ref[0, 0], car[jm])
                raw[s, HALO_ROW - 2:HALO_ROW, :] = halo
                n_ref[s] = raw[s, HALO_ROW + rows - 2:HALO_ROW + rows, :]
            if tiles_per_stream > 1:
                car[jm] = raw[n_sub - 1, HALO_ROW + rows - 2:HALO_ROW + rows, :]

        def conv(side, s, r0, c0, c1):
            raw, _, _, cw_ref, cb_ref, _ = side
            base = HALO_ROW + r0
            return (cb_ref[:, c0:c1]
                    + raw[s, base - 2:base - 2 + FFN_ROWS, c0:c1] * cw_ref[0:1, c0:c1]
                    + raw[s, base - 1:base - 1 + FFN_ROWS, c0:c1] * cw_ref[1:2, c0:c1]
                    + raw[s, base:base + FFN_ROWS, c0:c1] * cw_ref[2:3, c0:c1])

        for c0, c1 in chunks:
            for s in range(n_sub):
                for r0 in range(0, rows, FFN_ROWS):
                    gate = conv(sides[0], s, r0, c0, c1)
                    val = conv(sides[1], s, r0, c0, c1)
                    a_ref[s * rows + r0:s * rows + r0 + FFN_ROWS, c0:c1] = (
                        gate * (1.0 / (1.0 + jnp.exp(-gate))) * val).astype(BF16)

    @pl.when(j == 0)
    def _():
        x = x_ref[...]
        ms = jnp.mean(x * x, axis=-1, keepdims=True)
        h_scr[...] = (x * lax.rsqrt(ms + EPS) * g_ref[...]).astype(BF16)
        keep(matmuls(), 0)

    for slot in range(2):
        @pl.when((j > 0) & (j < nj) & (j % 2 == slot))
        def _():
            epilogue(1 - slot)
            keep(matmuls(), slot)

    @pl.when(j == nj)
    def _():
        epilogue((nj - 1) % 2)


def _ffn_up(x, g, w_up, conv_w, conv_b, state, *, layer, stream_len, tm, tn):
    m, d = x.shape
    nj = D_FF // tn
    if stream_len >= tm:
        rows, tiles_per_stream, n_sub = tm, stream_len // tm, 1
        stream_of = lambda i: i // tiles_per_stream
    else:
        rows, tiles_per_stream, n_sub = stream_len, 1, tm // stream_len
        stream_of = lambda i: i
    mm = lambda j: jnp.minimum(j, nj - 1)
    ep = lambda j: jnp.maximum(j - 1, 0)
    raw = pltpu.VMEM((n_sub, HALO_ROW + rows, tn), F32)
    outs = pl.pallas_call(
        functools.partial(_ffn_up_kernel, rows=rows, tiles_per_stream=tiles_per_stream, nj=nj),
        grid=(m // tm, nj + 1),
        in_specs=[pl.BlockSpec((tm, d), lambda i, j: (i, 0)),
                  pl.BlockSpec((1, d), lambda i, j: (0, 0)),
                  pl.BlockSpec((d, tn), lambda i, j: (0, mm(j))),
                  pl.BlockSpec((d, tn), lambda i, j: (0, nj + mm(j))),
                  pl.BlockSpec((CONV_W, tn), lambda i, j: (0, ep(j))),
                  pl.BlockSpec((CONV_W, tn), lambda i, j: (0, nj + ep(j))),
                  pl.BlockSpec((1, tn), lambda i, j: (0, ep(j))),
                  pl.BlockSpec((1, tn), lambda i, j: (0, nj + ep(j))),
                  pl.BlockSpec((1, n_sub, 2, tn), lambda i, j: (layer, stream_of(i), 0, ep(j))),
                  pl.BlockSpec((1, n_sub, 2, tn), lambda i, j: (layer, stream_of(i), 0, nj + ep(j)))],
        out_specs=[pl.BlockSpec((tm, tn), lambda i, j: (i, ep(j))),
                   pl.BlockSpec((n_sub, 2, tn), lambda i, j: (i, 0, ep(j))),
                   pl.BlockSpec((n_sub, 2, tn), lambda i, j: (i, 0, ep(j)))],
        out_shape=[jax.ShapeDtypeStruct((m, D_FF), BF16),
                   jax.ShapeDtypeStruct((m // rows, 2, D_FF), F32),
                   jax.ShapeDtypeStruct((m // rows, 2, D_FF), F32)],
        scratch_shapes=[pltpu.VMEM((tm, d), BF16), raw, raw, raw, raw,
                        pltpu.VMEM((nj, 2, tn), F32), pltpu.VMEM((nj, 2, tn), F32)],
        compiler_params=_params(("arbitrary", "arbitrary")),
        name="ffn_up",
    )(x, g, w_up, w_up, conv_w, conv_w, conv_b, conv_b, state, state)
    a, tail_g, tail_v = outs
    last = slice(tiles_per_stream - 1, None, tiles_per_stream)
    return a, tail_g[last], tail_v[last]


def _rope_tables(pos, reps):
    posf = pos.astype(F32)[:, None]

    def tab(n_rot):
        half = n_rot // 2
        inv = jnp.power(jnp.float32(ROPE_THETA), -jnp.arange(half, dtype=F32) * (2.0 / n_rot))
        ang = posf * inv[None, :]
        rest = 64 - n_rot
        cos = jnp.concatenate([jnp.cos(ang), jnp.cos(ang), jnp.ones((pos.shape[0], rest), F32)], axis=1)
        sin = jnp.concatenate([-jnp.sin(ang), jnp.sin(ang), jnp.zeros((pos.shape[0], rest), F32)], axis=1)
        return jnp.tile(cos, (reps, 2)), jnp.tile(sin, (reps, 2))

    cosa, sina = tab(A_ROPE)
    cosc, sinc = tab(C_ROT)
    return cosa, sina, cosc, sinc


def _block_diag(width, seg):
    r = jnp.arange(width)[:, None] // seg
    c = jnp.arange(width)[None, :] // seg
    return (r == c).astype(BF16)


def _layer_weights(l, attn_norm, w_in, b_forget, a_kv_norm, a_w_ukv, a_qn_nope, a_qn_rope, a_kn_nope,
                   a_kn_rope, b_qn, b_kn, c_qn, c_kn, c_lq1, c_lk1, c_lq2, c_lk2, c_out_norm, w_out,
                   ffn_norm, w_up, conv_w, conv_b, w_down):
    w_in_p = _permute_w_in(w_in, l, 256)
    ukv = a_w_ukv[l].reshape(A_KV_RANK, A_HEADS, A_NOPE + A_V)
    wk = ukv[:, :, :A_NOPE].reshape(A_KV_RANK, 1024).astype(BF16)
    wv = ukv[:, :, A_NOPE:].reshape(A_KV_RANK, 1024).astype(BF16)
    row = lambda v: v.reshape(1, -1).astype(F32)
    tile = lambda v, n: jnp.tile(v.astype(F32), n).reshape(1, -1)
    bd_kr = (jnp.arange(128)[:, None] < 64).astype(BF16) * jnp.ones((1, 128), BF16)
    post_consts = [
        tile(a_qn_nope[l], 8), tile(a_qn_rope[l], 8), row(a_kv_norm[l]),
        jnp.concatenate([a_kn_rope[l], jnp.zeros((64,), F32)]).reshape(1, 128),
        tile(a_kn_nope[l], 8), tile(b_qn[l], 4), tile(b_kn[l], 4), tile(c_qn[l], 8), tile(c_kn[l], 8),
        jnp.concatenate([b_forget[l], jnp.zeros((128 - B_HEADS,), F32)]).reshape(1, 128),
        _block_diag(512, 64), _block_diag(512, 128), jnp.ones((512, 512), BF16), bd_kr, wk, wv]
    wo = w_out[l].astype(BF16)
    return dict(
        attn_norm=row(attn_norm[l]), w_in=w_in_p, post_consts=post_consts,
        w_ukv=a_w_ukv[l].astype(BF16), gkn=row(a_kn_nope[l]),
        diff_extras=[row(c_lq1[l]), row(c_lk1[l]), row(c_lq2[l]), row(c_lk2[l]), row(c_out_norm[l])],
        wo_a=wo[:1024], wo_b=wo[1024:1536], wo_c=wo[1536:],
        ffn_norm=row(ffn_norm[l]), w_up=w_up[l].astype(BF16), conv_w=conv_w[l].astype(F32),
        conv_b=row(conv_b[l]), w_down=w_down[l].astype(BF16))


def _run_layer(x, lw, tabs, past, *, n_streams, t, lam_init, post_tm, ffn_tm):
    m = x.shape[0]
    z = _norm_mm(x, lw["attn_norm"], lw["w_in"], 512, Z_TN)
    (aq, kf, va, ckv, kr, bq, bk32, bk16, bv32, bv16, lf, cq, ck32, ck16, cv32, cv16) = _post(
        z, tabs, lw["post_consts"], post_tm)

    if past is None:
        qa, ka = _fox_prep(lf.reshape(n_streams, t, B_HEADS), bq, bk16, t, 512, ((0, 512),))
        oa = _attn_prompt(aq, kf, va, [], batch=n_streams, heads=A_HEADS, t=t, dqk=256, n_maps=1,
                          mask="chunk")
        ob = _attn_prompt(qa, ka, bv16, [], batch=n_streams, heads=B_HEADS, t=t, dqk=256, n_maps=1,
                          mask="causal")
        oc = _attn_prompt(cq, ck16, cv16, lw["diff_extras"], batch=n_streams, heads=C_HEADS, t=t, dqk=128,
                          n_maps=2, mask="chunk", lam_init=lam_init)
        conv_state, conv_layer = jnp.zeros((1, n_streams, CONV_W - 1, 2 * D_FF), F32), 0
    else:
        layer, c_ckv, c_kr, c_bk, c_bv, c_lf, c_ck, c_cv, conv_state = past
        p_len = c_ckv.shape[2]
        conv_state, conv_layer = conv_state[layer][None], 0
        lf_all = jnp.concatenate([c_lf[layer], lf.reshape(n_streams, t, B_HEADS)], axis=1)
        blocks = tuple((r, 512) for r in range(0, p_len, 512)) + ((p_len, t),)
        qa, ka, kaug = _fox_prep(lf_all, bq, bk16, t, p_len + t, blocks)
        oa = _mla_sample(aq, kf, va, c_ckv, c_kr, lw["w_ukv"], lw["gkn"], layer=layer, t=t)
        ob = _bc_sample(qa, ka, bv16, c_bk, c_bv, kaug, [], layer=layer, heads=B_HEADS,
                        n_maps=1, mask="causal", t=t)
        oc = _bc_sample(cq, ck16, cv16, c_ck, c_cv, None, lw["diff_extras"], layer=layer,
                        heads=C_HEADS, n_maps=2, mask="chunk", t=t, lam_init=lam_init)

    x1 = _mm_res(x, [oa, ob, oc], [lw["wo_a"], lw["wo_b"], lw["wo_c"]], 512, 1024)
    a, conv_g, conv_v = _ffn_up(x1, lw["ffn_norm"], lw["w_up"], lw["conv_w"], lw["conv_b"], conv_state,
                                layer=conv_layer, stream_len=t, tm=ffn_tm, tn=512)
    x2 = _mm_res(x1, [a], [lw["w_down"]], 512, 1024)
    states = (ckv.reshape(n_streams, t, A_KV_RANK), kr.reshape(n_streams, t, A_ROPE),
              bk32.reshape(n_streams, t, B_HEADS, B_DIM), bv32.reshape(n_streams, t, B_HEADS, B_DIM),
              lf.reshape(n_streams, t, B_HEADS),
              ck32.reshape(n_streams, t, C_HEADS, 2 * C_QK), cv32.reshape(n_streams, t, C_HEADS, C_V),
              jnp.concatenate([conv_g, conv_v], axis=-1))
    return x2, states


def kernel(x_prompt, x_sample, cache_a_ckv, cache_a_krope, cache_b_k, cache_b_v, cache_b_logf, cache_c_k, cache_c_v, state_ffn_conv, attn_norm, w_in, b_forget, a_kv_norm, a_w_ukv, a_qn_nope, a_qn_rope, a_kn_nope, a_kn_rope, b_qn, b_kn, c_qn, c_kn, c_lq1, c_lk1, c_lq2, c_lk2, c_out_norm, w_out, ffn_norm, w_up, conv_w, conv_b, w_down):
    bp, tp, d = x_prompt.shape
    bs, ts, _ = x_sample.shape
    depth = w_in.shape[0]
    past_len = cache_a_ckv.shape[2]
    post_tm = 256
    tabs_p = _rope_tables(jnp.arange(tp, dtype=jnp.int32), 1)
    tabs_s = _rope_tables(past_len + jnp.arange(ts, dtype=jnp.int32), post_tm // ts)

    merge_heads = lambda c: c.reshape(c.shape[:3] + (c.shape[3] * c.shape[4],)).astype(BF16)
    caches = (cache_a_ckv.astype(BF16), cache_a_krope.astype(BF16), merge_heads(cache_b_k),
              merge_heads(cache_b_v), cache_b_logf, merge_heads(cache_c_k), merge_heads(cache_c_v),
              state_ffn_conv)
    yp = x_prompt.reshape(bp * tp, d)
    ys = x_sample.reshape(bs * ts, d)
    states_p, states_s = [], []
    for l in range(depth):
        lw = _layer_weights(l, attn_norm, w_in, b_forget, a_kv_norm, a_w_ukv, a_qn_nope, a_qn_rope,
                            a_kn_nope, a_kn_rope, b_qn, b_kn, c_qn, c_kn, c_lq1, c_lk1, c_lq2, c_lk2,
                            c_out_norm, w_out, ffn_norm, w_up, conv_w, conv_b, w_down)
        lam_init = 0.8 - 0.6 * math.exp(-0.3 * l)
        yp, st_p = _run_layer(yp, lw, tabs_p, None, n_streams=bp, t=tp, lam_init=lam_init,
                              post_tm=post_tm, ffn_tm=512)
        ys, st_s = _run_layer(ys, lw, tabs_s, (l,) + caches, n_streams=bs, t=ts, lam_init=lam_init,
                              post_tm=post_tm, ffn_tm=256)
        states_p.append(st_p)
        states_s.append(st_s)
    outs_p = [jnp.stack(s) for s in zip(*states_p)]
    outs_s = [jnp.stack(s) for s in zip(*states_s)]
    return (yp.reshape(bp, tp, d), ys.reshape(bs, ts, d), *outs_p, *outs_s)
```

```python
import functools
import math

import jax
import jax.numpy as jnp
from jax import lax
from jax.experimental import pallas as pl
from jax.experimental.pallas import tpu as pltpu

F32 = jnp.float32
BF16 = jnp.bfloat16

D_MODEL = 2048
CHUNK = 64
ROPE_THETA = 500000.0
EPS = 1e-6
NEG_INF = -1e30
LOG2E = 1.4426950408889634

A_HEADS = 8
A_NOPE = 128
A_ROPE = 64
A_V = 128
A_QK = A_NOPE + A_ROPE
A_KV_RANK = 512
B_HEADS = 4
B_DIM = 128
C_HEADS = 4
C_QK = 64
C_V = 128
C_ROT = 16
D_FF = 5632
CONV_W = 3

LANES = 128
HALO_ROW = 8
VMEM_LIMIT = 56 * 1024 * 1024

Z_QN, Z_QR, Z_CKV, Z_BQ, Z_BK, Z_BV, Z_CQ, Z_CK, Z_CV, Z_KR, Z_BF, Z_COLS = (
    0, 1024, 1536, 2048, 2560, 3072, 3584, 4096, 4608, 5120, 5248, 5376)
Z_TN = 896
DENSE_TM = 1024
FFN_CHUNK = 256
FFN_ROWS = 32
ATTN_TQ = 1024
ATTN_TK = 512
ATTN_SUB = 256


def _params(sem):
    return pltpu.CompilerParams(dimension_semantics=sem, vmem_limit_bytes=VMEM_LIMIT)


def _dot(a, b):
    return jnp.dot(a, b, preferred_element_type=F32)


def _dot_nt(a, b):
    return lax.dot_general(a, b, (((1,), (1,)), ((), ())), preferred_element_type=F32)


def _lane_iota(shape):
    return lax.broadcasted_iota(jnp.int32, shape, len(shape) - 1)


def _permute_w_in_kernel(w_ref, o_ref):
    def put(dst, src, n):
        o_ref[:, dst:dst + n] = w_ref[0, :, src:src + n].astype(BF16)

    def clear(dst, n):
        o_ref[:, dst:dst + n] = jnp.zeros((o_ref.shape[0], n), BF16)

    for h in range(A_HEADS):
        put(Z_QN + h * A_NOPE, h * A_QK, A_NOPE)
        put(Z_QR + h * A_ROPE, h * A_QK + A_NOPE, A_ROPE)
    ckv0 = A_HEADS * A_QK
    kr0 = ckv0 + A_KV_RANK
    bq0 = kr0 + A_ROPE
    bf0 = bq0 + 3 * B_HEADS * B_DIM
    cq0 = bf0 + B_HEADS
    put(Z_CKV, ckv0, A_KV_RANK)
    put(Z_BQ, bq0, 3 * B_HEADS * B_DIM)
    put(Z_CQ, cq0, 3 * C_HEADS * C_V)
    put(Z_KR, kr0, A_ROPE)
    clear(Z_KR + A_ROPE, LANES - A_ROPE)
    put(Z_BF, bf0, B_HEADS)
    clear(Z_BF + B_HEADS, Z_COLS - Z_BF - B_HEADS)


def _permute_w_in(w_in, layer, tr):
    _, d, cols = w_in.shape
    return pl.pallas_call(
        _permute_w_in_kernel,
        grid=(d // tr,),
        in_specs=[pl.BlockSpec((1, tr, cols), lambda i: (layer, i, 0))],
        out_specs=pl.BlockSpec((tr, Z_COLS), lambda i: (i, 0)),
        out_shape=jax.ShapeDtypeStruct((d, Z_COLS), BF16),
        compiler_params=_params(("arbitrary",)),
        name="permute_w_in",
    )(w_in)


def _norm_mm_kernel(x_ref, g_ref, w_ref, o_ref, h_scr):
    @pl.when(pl.program_id(1) == 0)
    def _():
        x = x_ref[...]
        ms = jnp.mean(x * x, axis=-1, keepdims=True)
        h_scr[...] = (x * lax.rsqrt(ms + EPS) * g_ref[...]).astype(BF16)

    o_ref[...] = _dot(h_scr[...], w_ref[...])


def _norm_mm(x, g, w, tm, tn):
    m, d = x.shape
    n = w.shape[1]
    return pl.pallas_call(
        _norm_mm_kernel,
        grid=(m // tm, n // tn),
        in_specs=[pl.BlockSpec((tm, d), lambda i, j: (i, 0)),
                  pl.BlockSpec((1, d), lambda i, j: (0, 0)),
                  pl.BlockSpec((d, tn), lambda i, j: (0, j))],
        out_specs=pl.BlockSpec((tm, tn), lambda i, j: (i, j)),
        out_shape=jax.ShapeDtypeStruct((m, n), F32),
        scratch_shapes=[pltpu.VMEM((tm, d), BF16)],
        compiler_params=_params(("arbitrary", "arbitrary")),
        name="norm_mm",
    )(x, g, w)


def _seg_sumsq(x, bd):
    sq = x * x
    hi = sq.astype(BF16)
    lo = (sq - hi.astype(F32)).astype(BF16)
    return _dot(hi, bd) + _dot(lo, bd)


def _seg_norm(x, bd, seg, g):
    return x * lax.rsqrt(_seg_sumsq(x, bd) * (1.0 / seg) + EPS) * g


def _rope(x, cos, sin, half):
    n = x.shape[-1]
    first = (_lane_iota(x.shape) % 64) < half
    partner = jnp.where(first, pltpu.roll(x, n - half, 1), pltpu.roll(x, half, 1))
    return x * cos + partner * sin


def _post_kernel(z_ref, cosa_ref, sina_ref, cosc_ref, sinc_ref,
                 gqn_ref, gqr_ref, gckv_ref, gkr_ref, gkn_ref, gbq_ref, gbk_ref, gcq_ref, gck_ref,
                 bfg_ref, bd64_ref, bd128_ref, ones_ref, bdkr_ref, wk_ref, wv_ref,
                 aq_ref, kf_ref, va_ref, ckv_ref, kr_ref, bq_ref, bk32_ref, bk16_ref,
                 bv32_ref, bv16_ref, lf_ref, cq_ref, ck32_ref, ck16_ref, cv32_ref, cv16_ref):
    bd64 = bd64_ref[...]
    bd128 = bd128_ref[...]
    tm = z_ref.shape[0]
    lane = _lane_iota((tm, LANES))
    low = lane < 64

    cosa4 = jnp.concatenate([cosa_ref[...]] * 4, axis=1)
    sina4 = jnp.concatenate([sina_ref[...]] * 4, axis=1)
    cosc4 = jnp.concatenate([cosc_ref[...]] * 4, axis=1)
    sinc4 = jnp.concatenate([sinc_ref[...]] * 4, axis=1)

    a_scale = A_QK ** -0.5 * LOG2E
    qr = _seg_norm(z_ref[:, Z_QR:Z_QR + 512], bd64, 64, gqr_ref[...])
    qr = _rope(qr, cosa4, sina4, A_ROPE // 2) * a_scale
    for half in range(2):
        c0 = Z_QN + half * 512
        qn = _seg_norm(z_ref[:, c0:c0 + 512], bd128, 128, gqn_ref[:, half * 512:(half + 1) * 512]) * a_scale
        for hh in range(4):
            h = half * 4 + hh
            aq_ref[:, h * 256:h * 256 + 128] = qn[:, hh * 128:(hh + 1) * 128].astype(BF16)
    for h in range(A_HEADS):
        pair = qr[:, (h // 2) * 128:(h // 2 + 1) * 128]
        keep = low if h % 2 == 0 else jnp.logical_not(low)
        aq_ref[:, h * 256 + 128:h * 256 + 256] = jnp.where(keep, pair, 0.0).astype(BF16)

    ckv = _seg_norm(z_ref[:, Z_CKV:Z_CKV + 512], ones_ref[...], 512, gckv_ref[...])
    ckv_ref[...] = ckv
    ckv16 = ckv.astype(BF16)
    va_ref[...] = _dot(ckv16, wv_ref[...]).astype(BF16)

    kr = _seg_norm(z_ref[:, Z_KR:Z_KR + 128], bdkr_ref[...], 64, gkr_ref[...])
    kr = _rope(kr, cosa_ref[...], sina_ref[...], A_ROPE // 2)
    kr_ref[...] = kr[:, :A_ROPE]
    kr2 = kr + pltpu.roll(kr, 64, 1)
    kr_even = jnp.where(low, kr2, 0.0).astype(BF16)
    kr_odd = jnp.where(low, 0.0, kr2).astype(BF16)
    for half in range(2):
        kk = _dot(ckv16, wk_ref[:, half * 512:(half + 1) * 512])
        kn = _seg_norm(kk, bd128, 128, gkn_ref[:, half * 512:(half + 1) * 512])
        for hh in range(4):
            h = half * 4 + hh
            kf_ref[:, h * 256:h * 256 + 128] = kn[:, hh * 128:(hh + 1) * 128].astype(BF16)
            kf_ref[:, h * 256 + 128:h * 256 + 256] = kr_even if h % 2 == 0 else kr_odd

    bq = _seg_norm(z_ref[:, Z_BQ:Z_BQ + 512], bd128, 128, gbq_ref[...]) * (B_DIM ** -0.5 * LOG2E)
    bq_ref[...] = bq.astype(BF16)
    bk = _seg_norm(z_ref[:, Z_BK:Z_BK + 512], bd128, 128, gbk_ref[...])
    bk32_ref[...] = bk
    bk16_ref[...] = bk.astype(BF16)
    bv = z_ref[:, Z_BV:Z_BV + 512]
    bv32_ref[...] = bv
    bv16_ref[...] = bv.astype(BF16)
    f = z_ref[:, Z_BF:Z_BF + 128][:, 0:B_HEADS] + bfg_ref[:, 0:B_HEADS]
    lf_ref[...] = jnp.minimum(f, 0.0) - jnp.log1p(jnp.exp(-jnp.abs(f)))

    cq = _seg_norm(z_ref[:, Z_CQ:Z_CQ + 512], bd64, 64, gcq_ref[...])
    cq_ref[...] = (_rope(cq, cosc4, sinc4, C_ROT // 2) * (C_QK ** -0.5 * LOG2E)).astype(BF16)
    ck = _seg_norm(z_ref[:, Z_CK:Z_CK + 512], bd64, 64, gck_ref[...])
    ck = _rope(ck, cosc4, sinc4, C_ROT // 2)
    ck32_ref[...] = ck
    ck16_ref[...] = ck.astype(BF16)
    cv = z_ref[:, Z_CV:Z_CV + 512]
    cv32_ref[...] = cv
    cv16_ref[...] = cv.astype(BF16)


def _post(z, tabs, consts, tm):
    m = z.shape[0]
    tab_rows = tabs[0].shape[0]
    tab_blocks = tab_rows // tm

    def row(w):
        return pl.BlockSpec((tm, w), lambda i: (i, 0))

    def full(a):
        return pl.BlockSpec(a.shape, lambda i: (0, 0))

    tab_spec = pl.BlockSpec((tm, LANES), lambda i: (i % tab_blocks, 0))
    widths = [(2048, BF16), (2048, BF16), (1024, BF16), (512, F32), (64, F32), (512, BF16), (512, F32),
              (512, BF16), (512, F32), (512, BF16), (B_HEADS, F32), (512, BF16), (512, F32), (512, BF16),
              (512, F32), (512, BF16)]
    return pl.pallas_call(
        _post_kernel,
        grid=(m // tm,),
        in_specs=[row(Z_COLS)] + [tab_spec] * 4 + [full(c) for c in consts],
        out_specs=[row(w) for w, _ in widths],
        out_shape=[jax.ShapeDtypeStruct((m, w), dt) for w, dt in widths],
        compiler_params=_params(("arbitrary",)),
        name="post",
    )(z, *tabs, *consts)


def _split3(c):
    hi = c.astype(BF16).astype(F32)
    r1 = c - hi
    mid = r1.astype(BF16).astype(F32)
    lo = r1 - mid
    return hi, mid, lo


def _cumsum_block(lf, carry):
    n = lf.shape[0]
    r = lax.broadcasted_iota(jnp.int32, (n, n), 0)
    c = lax.broadcasted_iota(jnp.int32, (n, n), 1)
    tri = jnp.where(c <= r, 1.0, 0.0).astype(BF16)
    hi, mid, lo = _split3(lf)
    cum = _dot(tri, hi.astype(BF16)) + _dot(tri, mid.astype(BF16)) + _dot(tri, lo.astype(BF16))
    return cum + carry


def _aug_q(c, lane):
    hi, mid, lo = _split3(c)
    return jnp.where(lane == 0, hi, jnp.where(lane == 1, mid, jnp.where(lane == 2, lo,
                     jnp.where(lane < 6, 1.0, 0.0)))).astype(BF16)


def _aug_k(c, lane):
    hi, mid, lo = _split3(c)
    return jnp.where(lane < 3, 1.0, jnp.where(lane == 3, -hi, jnp.where(lane == 4, -mid,
                     jnp.where(lane == 5, -lo, 0.0)))).astype(BF16)


def _fox_prep_kernel(lf_ref, bq_ref, bk_ref, *out_refs, past, blocks):
    carry_scr = out_refs[-1]
    if past:
        qa_ref, ka_ref, kpast_ref = out_refs[:-1]
    else:
        qa_ref, ka_ref = out_refs[:-1]

    @pl.when(pl.program_id(1) == 0)
    def _():
        carry_scr[...] = jnp.zeros(carry_scr.shape, F32)

    carry = carry_scr[...]
    for r0, n in blocks:
        cum = _cumsum_block(lf_ref[0, r0:r0 + n, :], carry)
        carry = cum[n - 1:n, :]
        lane = _lane_iota((n, LANES))
        for h in range(B_HEADS):
            c = cum[:, h:h + 1] * LOG2E
            if r0 < past:
                kpast_ref[0, r0:r0 + n, h * 128:(h + 1) * 128] = _aug_k(c, lane)
            else:
                t0 = r0 - past
                qa_ref[t0:t0 + n, h * 256:h * 256 + 128] = bq_ref[t0:t0 + n, h * 128:(h + 1) * 128]
                qa_ref[t0:t0 + n, h * 256 + 128:h * 256 + 256] = _aug_q(c, lane)
                ka_ref[t0:t0 + n, h * 256:h * 256 + 128] = bk_ref[t0:t0 + n, h * 128:(h + 1) * 128]
                ka_ref[t0:t0 + n, h * 256 + 128:h * 256 + 256] = _aug_k(c, lane)
    carry_scr[...] = carry


def _fox_prep(lf_all, bq, bk, t_new, rows, blocks):
    s, t_tot, _ = lf_all.shape
    past = t_tot - t_new
    nb = t_tot // rows
    new_rows = rows - past
    m = bq.shape[0]
    new_spec = lambda w: pl.BlockSpec((new_rows, w), lambda i, j: (i * nb + j, 0))
    out_specs = [new_spec(1024), new_spec(1024)]
    out_shape = [jax.ShapeDtypeStruct((m, 1024), BF16), jax.ShapeDtypeStruct((m, 1024), BF16)]
    if past:
        out_specs.append(pl.BlockSpec((1, past, 512), lambda i, j: (i, 0, 0)))
        out_shape.append(jax.ShapeDtypeStruct((s, past, 512), BF16))
    return pl.pallas_call(
        functools.partial(_fox_prep_kernel, past=past, blocks=blocks),
        grid=(s, nb),
        in_specs=[pl.BlockSpec((1, rows, B_HEADS), lambda i, j: (i, j, 0)), new_spec(512), new_spec(512)],
        out_specs=out_specs,
        out_shape=out_shape,
        scratch_shapes=[pltpu.VMEM((1, B_HEADS), F32)],
        compiler_params=_params(("arbitrary", "arbitrary")),
        name="fox_prep",
    )(lf_all, bq, bk)


def _lanes(x, n):
    return x if n == 1 else jnp.concatenate([x] * n, axis=1)


def _with_ones(v):
    return jnp.concatenate([v, jnp.ones_like(v)], axis=1)


def _softmax_start(rows):
    return jnp.full((rows, LANES), NEG_INF, F32), jnp.zeros((rows, 2 * LANES), F32)


def _softmax_step(state, s, v1):
    m_old, acc = state
    tk = s.shape[1]
    m_new = jnp.maximum(m_old, jnp.max(s, axis=-1, keepdims=True))
    alpha = jnp.exp2(m_old - m_new)
    m_full = _lanes(m_new, tk // LANES) if tk >= LANES else m_new[:, :tk]
    p = jnp.exp2(s - m_full)
    return m_new, _lanes(alpha, 2) * acc + _dot(p.astype(BF16), v1)


def _softmax_out(state):
    acc = state[1]
    return acc[:, :LANES] / acc[:, LANES:]


def _visible(kind, q0, k0, tq, tk):
    qp = q0 + lax.broadcasted_iota(jnp.int32, (tq, tk), 0)
    kp = k0 + lax.broadcasted_iota(jnp.int32, (tq, tk), 1)
    if kind == "chunk":
        return (kp // CHUNK) <= (qp // CHUNK)
    return kp <= qp


def _split_maps(q, n_maps):
    if n_maps == 1:
        return [q]
    low = _lane_iota(q.shape) < C_QK
    zero = jnp.zeros_like(q)
    return [jnp.where(low, q, zero), jnp.where(low, zero, q)]


def _diff_lambda(lq1_ref, lk1_ref, lq2_ref, lk2_ref, lam_init):
    s1 = jnp.sum(lq1_ref[...] * lk1_ref[...], axis=-1, keepdims=True)
    s2 = jnp.sum(lq2_ref[...] * lk2_ref[...], axis=-1, keepdims=True)
    return jnp.exp(s1) - jnp.exp(s2) + lam_init


def _attn_finish(states, extra_refs, lam_init):
    if len(states) == 1:
        return _softmax_out(states[0])
    lq1_ref, lk1_ref, lq2_ref, lk2_ref, gout_ref = extra_refs
    lam = _diff_lambda(lq1_ref, lk1_ref, lq2_ref, lk2_ref, lam_init)
    o = _softmax_out(states[0]) - lam * _softmax_out(states[1])
    ms = jnp.mean(o * o, axis=-1, keepdims=True)
    return o * lax.rsqrt(ms + EPS) * gout_ref[...] * (1.0 - lam_init)


def _attn_prompt_kernel(*refs, n_maps, mask, tq, tk, sub, lam_init):
    q_ref, k_ref, v_ref = refs[:3]
    extra_refs = refs[3:-3]
    o_ref, m_scr, acc_scr = refs[-3:]
    qi = pl.program_id(2)
    n_sub = tq // sub
    m0, acc0 = _softmax_start(sub)
    for i in range(n_maps):
        for r in range(n_sub):
            m_scr[i, r] = m0
            acc_scr[i, r] = acc0

    def block(k0, nk, diag):
        k = k_ref[pl.ds(k0, nk), :]
        v1 = _with_ones(v_ref[pl.ds(k0, nk), :])
        todo = []
        for r in range(n_sub):
            vis = None
            if diag is not None:
                if diag > r * sub + sub - 1:
                    continue
                if diag + nk - 1 > r * sub:
                    vis = _visible(mask, r * sub, diag, sub, nk)
            todo.append((r, vis))
        scores = {r: [_dot_nt(qm, k) for qm in _split_maps(q_ref[r * sub:(r + 1) * sub, :], n_maps)]
                  for r, _ in todo}
        for r, vis in todo:
            for i in range(n_maps):
                s = scores[r][i] if vis is None else jnp.where(vis, scores[r][i], NEG_INF)
                m_scr[i, r], acc_scr[i, r] = _softmax_step((m_scr[i, r], acc_scr[i, r]), s, v1)

    def full_block(kb, carry):
        block(pl.multiple_of(kb * tq, tq), tq, None)
        return carry

    lax.fori_loop(0, qi, full_block, 0)
    for d in range(tq // tk):
        block(pl.multiple_of(qi * tq + d * tk, tk), tk, d * tk)
    for r in range(n_sub):
        states = [(m_scr[i, r], acc_scr[i, r]) for i in range(n_maps)]
        o_ref[r * sub:(r + 1) * sub, :] = _attn_finish(states, extra_refs, lam_init).astype(o_ref.dtype)


def _attn_prompt(q, k, v, extras, *, batch, heads, t, dqk, n_maps, mask, lam_init=0.0):
    tq, tk, sub = ATTN_TQ, ATTN_TK, ATTN_SUB
    nq = t // tq
    dv = 128
    extra_specs = [pl.BlockSpec(e.shape, lambda b, h, i: (0, 0)) for e in extras]
    return pl.pallas_call(
        functools.partial(_attn_prompt_kernel, n_maps=n_maps, mask=mask, tq=tq, tk=tk, sub=sub,
                          lam_init=lam_init),
        grid=(batch, heads, nq),
        in_specs=[pl.BlockSpec((tq, dqk), lambda b, h, i: (b * nq + i, h)),
                  pl.BlockSpec((t, dqk), lambda b, h, i: (b, h)),
                  pl.BlockSpec((t, dv), lambda b, h, i: (b, h))] + extra_specs,
        out_specs=pl.BlockSpec((tq, dv), lambda b, h, i: (b * nq + i, h)),
        out_shape=jax.ShapeDtypeStruct((batch * t, heads * dv), BF16),
        scratch_shapes=[pltpu.VMEM((n_maps, tq // sub, sub, LANES), F32),
                        pltpu.VMEM((n_maps, tq // sub, sub, 2 * LANES), F32)],
        compiler_params=_params(("arbitrary", "arbitrary", "arbitrary")),
        name="attn_prompt_" + mask + str(n_maps),
    )(q, k, v, *extras)


def _mla_sample_kernel(q_ref, kf_ref, va_ref, ckv_ref, kr_ref, wukv_ref, gkn_ref, o_ref, *, past, t):
    ckv_p = ckv_ref[0, 0]
    kr_p = kr_ref[0, 0]
    vis = _visible("chunk", past, past, t, t)
    up = lambda h: _dot(ckv_p, wukv_ref[:, h * 256:(h + 1) * 256])
    kv_next = up(0)
    for h in range(A_HEADS):
        kv = kv_next
        if h + 1 < A_HEADS:
            kv_next = up(h + 1)
        kk = kv[:, :A_NOPE]
        ms = jnp.mean(kk * kk, axis=-1, keepdims=True)
        kn = (kk * lax.rsqrt(ms + EPS) * gkn_ref[...]).astype(BF16)
        vp = kv[:, A_NOPE:].astype(BF16)
        q = q_ref[:, h * 256:(h + 1) * 256]
        r0 = 128 + (h % 2) * A_ROPE
        s = _dot_nt(q[:, :128], kn) + _dot_nt(q[:, r0:r0 + A_ROPE], kr_p)
        state = _softmax_step(_softmax_start(t), s, _with_ones(vp))
        s = jnp.where(vis, _dot_nt(q, kf_ref[:, h * 256:(h + 1) * 256]), NEG_INF)
        state = _softmax_step(state, s, _with_ones(va_ref[:, h * 128:(h + 1) * 128]))
        o_ref[:, h * 128:(h + 1) * 128] = _softmax_out(state).astype(o_ref.dtype)


def _mla_sample(q, kf, va, cache_ckv, cache_kr, w_ukv, gkn, *, layer, t):
    _, s, past, rank = cache_ckv.shape
    return pl.pallas_call(
        functools.partial(_mla_sample_kernel, past=past, t=t),
        grid=(s,),
        in_specs=[pl.BlockSpec((t, 2048), lambda i: (i, 0)),
                  pl.BlockSpec((t, 2048), lambda i: (i, 0)),
                  pl.BlockSpec((t, 1024), lambda i: (i, 0)),
                  pl.BlockSpec((1, 1, past, rank), lambda i: (layer, i, 0, 0)),
                  pl.BlockSpec((1, 1, past, A_ROPE), lambda i: (layer, i, 0, 0)),
                  pl.BlockSpec(w_ukv.shape, lambda i: (0, 0)),
                  pl.BlockSpec(gkn.shape, lambda i: (0, 0))],
        out_specs=pl.BlockSpec((t, 1024), lambda i: (i, 0)),
        out_shape=jax.ShapeDtypeStruct((s * t, 1024), BF16),
        compiler_params=_params(("arbitrary",)),
        name="mla_sample",
    )(q, kf, va, cache_ckv, cache_kr, w_ukv, gkn)


def _bc_sample_kernel(*refs, n_maps, mask, heads, past, t, lam_init, with_aug):
    q_ref, kn_ref, vn_ref, kp_ref, vp_ref = refs[:5]
    rest = refs[5:-1]
    o_ref = refs[-1]
    if with_aug:
        kaug_ref, extra_refs = rest[0], rest[1:]
    else:
        extra_refs = rest
    dq = q_ref.shape[1] // heads
    vis = _visible(mask, past, past, t, t)
    for h in range(heads):
        q = q_ref[:, h * dq:(h + 1) * dq]
        kp = kp_ref[0, 0, :, h * 128:(h + 1) * 128]
        vp = _with_ones(vp_ref[0, 0, :, h * 128:(h + 1) * 128])
        kn = kn_ref[:, h * dq:(h + 1) * dq]
        vn = _with_ones(vn_ref[:, h * 128:(h + 1) * 128])
        states = []
        if with_aug:
            s = _dot_nt(q[:, :128], kp) + _dot_nt(q[:, 128:], kaug_ref[0, :, h * 128:(h + 1) * 128])
            state = _softmax_step(_softmax_start(t), s, vp)
            states.append(_softmax_step(state, jnp.where(vis, _dot_nt(q, kn), NEG_INF), vn))
        else:
            for qm in _split_maps(q, n_maps):
                state = _softmax_step(_softmax_start(t), _dot_nt(qm, kp), vp)
                states.append(_softmax_step(state, jnp.where(vis, _dot_nt(qm, kn), NEG_INF), vn))
        o_ref[:, h * 128:(h + 1) * 128] = _attn_finish(states, extra_refs, lam_init).astype(o_ref.dtype)


def _bc_sample(q, kn, vn, cache_k, cache_v, kaug, extras, *, layer, heads, n_maps, mask, t, lam_init=0.0):
    _, s, past, _ = cache_k.shape
    dq = q.shape[1]
    cache_spec = pl.BlockSpec((1, 1, past, heads * 128), lambda i: (layer, i, 0, 0))
    in_specs = [pl.BlockSpec((t, dq), lambda i: (i, 0)),
                pl.BlockSpec((t, dq), lambda i: (i, 0)),
                pl.BlockSpec((t, heads * 128), lambda i: (i, 0)),
                cache_spec, cache_spec]
    args = [q, kn, vn, cache_k, cache_v]
    if kaug is not None:
        in_specs.append(pl.BlockSpec((1, past, heads * 128), lambda i: (i, 0, 0)))
        args.append(kaug)
    in_specs += [pl.BlockSpec(e.shape, lambda i: (0, 0)) for e in extras]
    return pl.pallas_call(
        functools.partial(_bc_sample_kernel, n_maps=n_maps, mask=mask, heads=heads, past=past, t=t,
                          lam_init=lam_init, with_aug=kaug is not None),
        grid=(s,),
        in_specs=in_specs,
        out_specs=pl.BlockSpec((t, heads * 128), lambda i: (i, 0)),
        out_shape=jax.ShapeDtypeStruct((s * t, heads * 128), BF16),
        compiler_params=_params(("arbitrary",)),
        name="sample_" + mask + str(n_maps),
    )(*args, *extras)


def _mm_res_kernel(*refs, n_in):
    res_ref = refs[2 * n_in]
    o_ref = refs[2 * n_in + 1]
    acc = res_ref[...]
    for i in range(n_in):
        acc = acc + _dot(refs[i][...], refs[n_in + i][...])
    o_ref[...] = acc


def _mm_res(res, a_list, w_list, tm, tn):
    m, n = res.shape
    n_in = len(a_list)
    in_specs = ([pl.BlockSpec((tm, a.shape[1]), lambda j, i: (i, 0)) for a in a_list]
                + [pl.BlockSpec((w.shape[0], tn), lambda j, i: (0, j)) for w in w_list]
                + [pl.BlockSpec((tm, tn), lambda j, i: (i, j))])
    return pl.pallas_call(
        functools.partial(_mm_res_kernel, n_in=n_in),
        grid=(n // tn, m // tm),
        in_specs=in_specs,
        out_specs=pl.BlockSpec((tm, tn), lambda j, i: (i, j)),
        out_shape=jax.ShapeDtypeStruct((m, n), F32),
        compiler_params=_params(("arbitrary", "arbitrary")),
        name="mm_res%d" % n_in,
    )(*a_list, *w_list, res)


def _ffn_up_kernel(x_ref, g_ref, wg_ref, wv_ref, cwg_ref, cwv_ref, cbg_ref, cbv_ref, sg_ref, sv_ref,
                   a_ref, ng_ref, nv_ref, h_scr, rawg0, rawv0, rawg1, rawv1, carg, carv,
                   *, rows, tiles_per_stream, nj):
    i = pl.program_id(0)
    j = pl.program_id(1)
    tm = x_ref.shape[0]
    n_sub = tm // rows
    tn = a_ref.shape[1]
    chunks = [(c, c + FFN_CHUNK) for c in range(0, tn, FFN_CHUNK)]
    raws = ((rawg0, rawv0), (rawg1, rawv1))

    def matmuls():
        h = h_scr[...]
        return [(_dot(h, wg_ref[:, c0:c1]), _dot(h, wv_ref[:, c0:c1])) for c0, c1 in chunks]

    def keep(dots, slot):
        for (c0, c1), us in zip(chunks, dots):
            for u, raw in zip(us, raws[slot]):
                for s in range(n_sub):
                    raw[s, HALO_ROW:HALO_ROW + rows, c0:c1] = u[s * rows:(s + 1) * rows]

    def epilogue(slot):
        jm = j - 1
        sides = ((raws[slot][0], carg, sg_ref, cwg_ref, cbg_ref, ng_ref),
                 (raws[slot][1], carv, sv_ref, cwv_ref, cbv_ref, nv_ref))
        for raw, car, s_ref, _, _, n_ref in sides:
            for s in range(n_sub):
                if tiles_per_stream == 1:
                    halo = s_ref[0, s]
                else:
                    halo = jnp.where((i % tiles_per_stream) == 0, s_ref[0, 0], car[jm])
                raw[s, HALO_ROW - 2:HALO_ROW, :] = halo
                n_ref[s] = raw[s, HALO_ROW + rows - 2:HALO_ROW + rows, :]
            if tiles_per_stream > 1:
                car[jm] = raw[n_sub - 1, HALO_ROW + rows - 2:HALO_ROW + rows, :]

        def conv(side, s, r0, c0, c1):
            raw, _, _, cw_ref, cb_ref, _ = side
            base = HALO_ROW + r0
            return (cb_ref[:, c0:c1]
                    + raw[s, base - 2:base - 2 + FFN_ROWS, c0:c1] * cw_ref[0:1, c0:c1]
                    + raw[s, base - 1:base - 1 + FFN_ROWS, c0:c1] * cw_ref[1:2, c0:c1]
                    + raw[s, base:base + FFN_ROWS, c0:c1] * cw_ref[2:3, c0:c1])

        for c0, c1 in chunks:
            for s in range(n_sub):
                for r0 in range(0, rows, FFN_ROWS):
                    gate = conv(sides[0], s, r0, c0, c1)
                    val = conv(sides[1], s, r0, c0, c1)
                    a_ref[s * rows + r0:s * rows + r0 + FFN_ROWS, c0:c1] = (
                        gate * (1.0 / (1.0 + jnp.exp(-gate))) * val).astype(BF16)

    @pl.when(j == 0)
    def _():
        x = x_ref[...]
        ms = jnp.mean(x * x, axis=-1, keepdims=True)
        h_scr[...] = (x * lax.rsqrt(ms + EPS) * g_ref[...]).astype(BF16)
        keep(matmuls(), 0)

    for slot in range(2):
        @pl.when((j > 0) & (j < nj) & (j % 2 == slot))
        def _():
            epilogue(1 - slot)
            keep(matmuls(), slot)

    @pl.when(j == nj)
    def _():
        epilogue((nj - 1) % 2)


def _ffn_up(x, g, w_up, conv_w, conv_b, state, *, layer, stream_len, tm, tn):
    m, d = x.shape
    nj = D_FF // tn
    if stream_len >= tm:
        rows, tiles_per_stream, n_sub = tm, stream_len // tm, 1
        stream_of = lambda i: i // tiles_per_stream
    else:
        rows, tiles_per_stream, n_sub = stream_len, 1, tm // stream_len
        stream_of = lambda i: i
    mm = lambda j: jnp.minimum(j, nj - 1)
    ep = lambda j: jnp.maximum(j - 1, 0)
    raw = pltpu.VMEM((n_sub, HALO_ROW + rows, tn), F32)
    outs = pl.pallas_call(
        functools.partial(_ffn_up_kernel, rows=rows, tiles_per_stream=tiles_per_stream, nj=nj),
        grid=(m // tm, nj + 1),
        in_specs=[pl.BlockSpec((tm, d), lambda i, j: (i, 0)),
                  pl.BlockSpec((1, d), lambda i, j: (0, 0)),
                  pl.BlockSpec((d, tn), lambda i, j: (0, mm(j))),
                  pl.BlockSpec((d, tn), lambda i, j: (0, nj + mm(j))),
                  pl.BlockSpec((CONV_W, tn), lambda i, j: (0, ep(j))),
                  pl.BlockSpec((CONV_W, tn), lambda i, j: (0, nj + ep(j))),
                  pl.BlockSpec((1, tn), lambda i, j: (0, ep(j))),
                  pl.BlockSpec((1, tn), lambda i, j: (0, nj + ep(j))),
                  pl.BlockSpec((1, n_sub, 2, tn), lambda i, j: (layer, stream_of(i), 0, ep(j))),
                  pl.BlockSpec((1, n_sub, 2, tn), lambda i, j: (layer, stream_of(i), 0, nj + ep(j)))],
        out_specs=[pl.BlockSpec((tm, tn), lambda i, j: (i, ep(j))),
                   pl.BlockSpec((n_sub, 2, tn), lambda i, j: (i, 0, ep(j))),
                   pl.BlockSpec((n_sub, 2, tn), lambda i, j: (i, 0, ep(j)))],
        out_shape=[jax.ShapeDtypeStruct((m, D_FF), BF16),
                   jax.ShapeDtypeStruct((m // rows, 2, D_FF), F32),
                   jax.ShapeDtypeStruct((m // rows, 2, D_FF), F32)],
        scratch_shapes=[pltpu.VMEM((tm, d), BF16), raw, raw, raw, raw,
                        pltpu.VMEM((nj, 2, tn), F32), pltpu.VMEM((nj, 2, tn), F32)],
        compiler_params=_params(("arbitrary", "arbitrary")),
        name="ffn_up",
    )(x, g, w_up, w_up, conv_w, conv_w, conv_b, conv_b, state, state)
    a, tail_g, tail_v = outs
    last = slice(tiles_per_stream - 1, None, tiles_per_stream)
    return a, tail_g[last], tail_v[last]


def _rope_tables(pos, reps):
    posf = pos.astype(F32)[:, None]

    def tab(n_rot):
        half = n_rot // 2
        inv = jnp.power(jnp.float32(ROPE_THETA), -jnp.arange(half, dtype=F32) * (2.0 / n_rot))
        ang = posf * inv[None, :]
        rest = 64 - n_rot
        cos = jnp.concatenate([jnp.cos(ang), jnp.cos(ang), jnp.ones((pos.shape[0], rest), F32)], axis=1)
        sin = jnp.concatenate([-jnp.sin(ang), jnp.sin(ang), jnp.zeros((pos.shape[0], rest), F32)], axis=1)
        return jnp.tile(cos, (reps, 2)), jnp.tile(sin, (reps, 2))

    cosa, sina = tab(A_ROPE)
    cosc, sinc = tab(C_ROT)
    return cosa, sina, cosc, sinc


def _block_diag(width, seg):
    r = jnp.arange(width)[:, None] // seg
    c = jnp.arange(width)[None, :] // seg
    return (r == c).astype(BF16)


def _layer_weights(l, attn_norm, w_in, b_forget, a_kv_norm, a_w_ukv, a_qn_nope, a_qn_rope, a_kn_nope,
                   a_kn_rope, b_qn, b_kn, c_qn, c_kn, c_lq1, c_lk1, c_lq2, c_lk2, c_out_norm, w_out,
                   ffn_norm, w_up, conv_w, conv_b, w_down):
    w_in_p = _permute_w_in(w_in, l, 256)
    ukv = a_w_ukv[l].reshape(A_KV_RANK, A_HEADS, A_NOPE + A_V)
    wk = ukv[:, :, :A_NOPE].reshape(A_KV_RANK, 1024).astype(BF16)
    wv = ukv[:, :, A_NOPE:].reshape(A_KV_RANK, 1024).astype(BF16)
    row = lambda v: v.reshape(1, -1).astype(F32)
    tile = lambda v, n: jnp.tile(v.astype(F32), n).reshape(1, -1)
    bd_kr = (jnp.arange(128)[:, None] < 64).astype(BF16) * jnp.ones((1, 128), BF16)
    post_consts = [
        tile(a_qn_nope[l], 8), tile(a_qn_rope[l], 8), row(a_kv_norm[l]),
        jnp.concatenate([a_kn_rope[l], jnp.zeros((64,), F32)]).reshape(1, 128),
        tile(a_kn_nope[l], 8), tile(b_qn[l], 4), tile(b_kn[l], 4), tile(c_qn[l], 8), tile(c_kn[l], 8),
        jnp.concatenate([b_forget[l], jnp.zeros((128 - B_HEADS,), F32)]).reshape(1, 128),
        _block_diag(512, 64), _block_diag(512, 128), jnp.ones((512, 512), BF16), bd_kr, wk, wv]
    wo = w_out[l].astype(BF16)
    return dict(
        attn_norm=row(attn_norm[l]), w_in=w_in_p, post_consts=post_consts,
        w_ukv=a_w_ukv[l].astype(BF16), gkn=row(a_kn_nope[l]),
        diff_extras=[row(c_lq1[l]), row(c_lk1[l]), row(c_lq2[l]), row(c_lk2[l]), row(c_out_norm[l])],
        wo_a=wo[:1024], wo_b=wo[1024:1536], wo_c=wo[1536:],
        ffn_norm=row(ffn_norm[l]), w_up=w_up[l].astype(BF16), conv_w=conv_w[l].astype(F32),
        conv_b=row(conv_b[l]), w_down=w_down[l].astype(BF16))


def _run_layer(x, lw, tabs, past, *, n_streams, t, lam_init, post_tm, ffn_tm):
    m = x.shape[0]
    z = _norm_mm(x, lw["attn_norm"], lw["w_in"], DENSE_TM, Z_TN)
    (aq, kf, va, ckv, kr, bq, bk32, bk16, bv32, bv16, lf, cq, ck32, ck16, cv32, cv16) = _post(
        z, tabs, lw["post_consts"], post_tm)

    if past is None:
        qa, ka = _fox_prep(lf.reshape(n_streams, t, B_HEADS), bq, bk16, t, 512, ((0, 512),))
        oa = _attn_prompt(aq, kf, va, [], batch=n_streams, heads=A_HEADS, t=t, dqk=256, n_maps=1,
                          mask="chunk")
        ob = _attn_prompt(qa, ka, bv16, [], batch=n_streams, heads=B_HEADS, t=t, dqk=256, n_maps=1,
                          mask="causal")
        oc = _attn_prompt(cq, ck16, cv16, lw["diff_extras"], batch=n_streams, heads=C_HEADS, t=t, dqk=128,
                          n_maps=2, mask="chunk", lam_init=lam_init)
        conv_state, conv_layer = jnp.zeros((1, n_streams, CONV_W - 1, 2 * D_FF), F32), 0
    else:
        layer, c_ckv, c_kr, c_bk, c_bv, c_lf, c_ck, c_cv, conv_state = past
        p_len = c_ckv.shape[2]
        conv_state, conv_layer = conv_state[layer][None], 0
        lf_all = jnp.concatenate([c_lf[layer], lf.reshape(n_streams, t, B_HEADS)], axis=1)
        blocks = tuple((r, 512) for r in range(0, p_len, 512)) + ((p_len, t),)
        qa, ka, kaug = _fox_prep(lf_all, bq, bk16, t, p_len + t, blocks)
        oa = _mla_sample(aq, kf, va, c_ckv, c_kr, lw["w_ukv"], lw["gkn"], layer=layer, t=t)
        ob = _bc_sample(qa, ka, bv16, c_bk, c_bv, kaug, [], layer=layer, heads=B_HEADS,
                        n_maps=1, mask="causal", t=t)
        oc = _bc_sample(cq, ck16, cv16, c_ck, c_cv, None, lw["diff_extras"], layer=layer,
                        heads=C_HEADS, n_maps=2, mask="chunk", t=t, lam_init=lam_init)

    x1 = _mm_res(x, [oa, ob, oc], [lw["wo_a"], lw["wo_b"], lw["wo_c"]], 512, 1024)
    a, conv_g, conv_v = _ffn_up(x1, lw["ffn_norm"], lw["w_up"], lw["conv_w"], lw["conv_b"], conv_state,
                                layer=conv_layer, stream_len=t, tm=ffn_tm, tn=512)
    x2 = _mm_res(x1, [a], [lw["w_down"]], 512, 1024)
    states = (ckv.reshape(n_streams, t, A_KV_RANK), kr.reshape(n_streams, t, A_ROPE),
              bk32.reshape(n_streams, t, B_HEADS, B_DIM), bv32.reshape(n_streams, t, B_HEADS, B_DIM),
              lf.reshape(n_streams, t, B_HEADS),
              ck32.reshape(n_streams, t, C_HEADS, 2 * C_QK), cv32.reshape(n_streams, t, C_HEADS, C_V),
              jnp.concatenate([conv_g, conv_v], axis=-1))
    return x2, states


def kernel(x_prompt, x_sample, cache_a_ckv, cache_a_krope, cache_b_k, cache_b_v, cache_b_logf, cache_c_k, cache_c_v, state_ffn_conv, attn_norm, w_in, b_forget, a_kv_norm, a_w_ukv, a_qn_nope, a_qn_rope, a_kn_nope, a_kn_rope, b_qn, b_kn, c_qn, c_kn, c_lq1, c_lk1, c_lq2, c_lk2, c_out_norm, w_out, ffn_norm, w_up, conv_w, conv_b, w_down):
    bp, tp, d = x_prompt.shape
    bs, ts, _ = x_sample.shape
    depth = w_in.shape[0]
    past_len = cache_a_ckv.shape[2]
    post_tm = 256
    tabs_p = _rope_tables(jnp.arange(tp, dtype=jnp.int32), 1)
    tabs_s = _rope_tables(past_len + jnp.arange(ts, dtype=jnp.int32), post_tm // ts)

    merge_heads = lambda c: c.reshape(c.shape[:3] + (c.shape[3] * c.shape[4],)).astype(BF16)
    caches = (cache_a_ckv.astype(BF16), cache_a_krope.astype(BF16), merge_heads(cache_b_k),
              merge_heads(cache_b_v), cache_b_logf, merge_heads(cache_c_k), merge_heads(cache_c_v),
              state_ffn_conv)
    yp = x_prompt.reshape(bp * tp, d)
    ys = x_sample.reshape(bs * ts, d)
    states_p, states_s = [], []
    for l in range(depth):
        lw = _layer_weights(l, attn_norm, w_in, b_forget, a_kv_norm, a_w_ukv, a_qn_nope, a_qn_rope,
                            a_kn_nope, a_kn_rope, b_qn, b_kn, c_qn, c_kn, c_lq1, c_lk1, c_lq2, c_lk2,
                            c_out_norm, w_out, ffn_norm, w_up, conv_w, conv_b, w_down)
        lam_init = 0.8 - 0.6 * math.exp(-0.3 * l)
        yp, st_p = _run_layer(yp, lw, tabs_p, None, n_streams=bp, t=tp, lam_init=lam_init,
                              post_tm=post_tm, ffn_tm=DENSE_TM)
        ys, st_s = _run_layer(ys, lw, tabs_s, (l,) + caches, n_streams=bs, t=ts, lam_init=lam_init,
                              post_tm=post_tm, ffn_tm=DENSE_TM)
        states_p.append(st_p)
        states_s.append(st_s)
    outs_p = [jnp.stack(s) for s in zip(*states_p)]
    outs_s = [jnp.stack(s) for s in zip(*states_s)]
    return (yp.reshape(bp, tp, d), ys.reshape(bs, ts, d), *outs_p, *outs_s)
```

```python
import functools
import math

import jax
import jax.numpy as jnp
from jax import lax
from jax.experimental import pallas as pl
from jax.experimental.pallas import tpu as pltpu

F32 = jnp.float32
BF16 = jnp.bfloat16

D_MODEL = 2048
CHUNK = 64
ROPE_THETA = 500000.0
EPS = 1e-6
NEG_INF = -1e30
LOG2E = 1.4426950408889634

A_HEADS = 8
A_NOPE = 128
A_ROPE = 64
A_V = 128
A_QK = A_NOPE + A_ROPE
A_KV_RANK = 512
B_HEADS = 4
B_DIM = 128
C_HEADS = 4
C_QK = 64
C_V = 128
C_ROT = 16
D_FF = 5632
CONV_W = 3

LANES = 128
HALO_ROW = 8
VMEM_LIMIT = 56 * 1024 * 1024

Z_QN, Z_QR, Z_CKV, Z_BQ, Z_BK, Z_BV, Z_CQ, Z_CK, Z_CV, Z_KR, Z_BF, Z_COLS = (
    0, 1024, 1536, 2048, 2560, 3072, 3584, 4096, 4608, 5120, 5248, 5376)
Z_TN = 896
DENSE_TM = 1024
FFN_CHUNK = 256
FFN_ROWS = 32
ATTN_TQ = 1024
ATTN_TK = 512
ATTN_SUB = 256


def _params(sem):
    return pltpu.CompilerParams(dimension_semantics=sem, vmem_limit_bytes=VMEM_LIMIT)


def _dot(a, b):
    return jnp.dot(a, b, preferred_element_type=F32)


def _dot_nt(a, b):
    return lax.dot_general(a, b, (((1,), (1,)), ((), ())), preferred_element_type=F32)


def _lane_iota(shape):
    return lax.broadcasted_iota(jnp.int32, shape, len(shape) - 1)


def _permute_w_in_kernel(w_ref, o_ref):
    def put(dst, src, n):
        o_ref[:, dst:dst + n] = w_ref[0, :, src:src + n].astype(BF16)

    def clear(dst, n):
        o_ref[:, dst:dst + n] = jnp.zeros((o_ref.shape[0], n), BF16)

    for h in range(A_HEADS):
        put(Z_QN + h * A_NOPE, h * A_QK, A_NOPE)
        put(Z_QR + h * A_ROPE, h * A_QK + A_NOPE, A_ROPE)
    ckv0 = A_HEADS * A_QK
    kr0 = ckv0 + A_KV_RANK
    bq0 = kr0 + A_ROPE
    bf0 = bq0 + 3 * B_HEADS * B_DIM
    cq0 = bf0 + B_HEADS
    put(Z_CKV, ckv0, A_KV_RANK)
    put(Z_BQ, bq0, 3 * B_HEADS * B_DIM)
    put(Z_CQ, cq0, 3 * C_HEADS * C_V)
    put(Z_KR, kr0, A_ROPE)
    clear(Z_KR + A_ROPE, LANES - A_ROPE)
    put(Z_BF, bf0, B_HEADS)
    clear(Z_BF + B_HEADS, Z_COLS - Z_BF - B_HEADS)


def _permute_w_in(w_in, layer, tr):
    _, d, cols = w_in.shape
    return pl.pallas_call(
        _permute_w_in_kernel,
        grid=(d // tr,),
        in_specs=[pl.BlockSpec((1, tr, cols), lambda i: (layer, i, 0))],
        out_specs=pl.BlockSpec((tr, Z_COLS), lambda i: (i, 0)),
        out_shape=jax.ShapeDtypeStruct((d, Z_COLS), BF16),
        compiler_params=_params(("arbitrary",)),
        name="permute_w_in",
    )(w_in)


def _norm_mm_kernel(x_ref, g_ref, w_ref, o_ref, h_scr):
    @pl.when(pl.program_id(1) == 0)
    def _():
        x = x_ref[...]
        ms = jnp.mean(x * x, axis=-1, keepdims=True)
        h_scr[...] = (x * lax.rsqrt(ms + EPS) * g_ref[...]).astype(BF16)

    o_ref[...] = _dot(h_scr[...], w_ref[...])


def _norm_mm(x, g, w, tm, tn):
    m, d = x.shape
    n = w.shape[1]
    return pl.pallas_call(
        _norm_mm_kernel,
        grid=(m // tm, n // tn),
        in_specs=[pl.BlockSpec((tm, d), lambda i, j: (i, 0)),
                  pl.BlockSpec((1, d), lambda i, j: (0, 0)),
                  pl.BlockSpec((d, tn), lambda i, j: (0, j))],
        out_specs=pl.BlockSpec((tm, tn), lambda i, j: (i, j)),
        out_shape=jax.ShapeDtypeStruct((m, n), F32),
        scratch_shapes=[pltpu.VMEM((tm, d), BF16)],
        compiler_params=_params(("arbitrary", "arbitrary")),
        name="norm_mm",
    )(x, g, w)


def _seg_sumsq(x, bd):
    sq = x * x
    hi = sq.astype(BF16)
    lo = (sq - hi.astype(F32)).astype(BF16)
    return _dot(hi, bd) + _dot(lo, bd)


def _seg_norm(x, bd, seg, g):
    return x * lax.rsqrt(_seg_sumsq(x, bd) * (1.0 / seg) + EPS) * g


def _rope(x, cos, sin, half):
    n = x.shape[-1]
    first = (_lane_iota(x.shape) % 64) < half
    partner = jnp.where(first, pltpu.roll(x, n - half, 1), pltpu.roll(x, half, 1))
    return x * cos + partner * sin


def _store_heads(o_ref, x):
    rows = x.shape[0]
    heads = x.shape[1] // LANES
    for h in range(heads):
        o_ref[pl.ds(h, rows, stride=heads), :] = x[:, h * LANES:(h + 1) * LANES]


def _post_kernel(z_ref, cosa_ref, sina_ref, cosc_ref, sinc_ref,
                 gqn_ref, gqr_ref, gckv_ref, gkr_ref, gkn_ref, gbq_ref, gbk_ref, gcq_ref, gck_ref,
                 bfg_ref, bd64_ref, bd128_ref, ones_ref, bdkr_ref, wk_ref, wv_ref,
                 aq_ref, kf_ref, va_ref, ckv_ref, kr_ref, bq_ref, bk32_ref, bk16_ref,
                 bv32_ref, bv16_ref, lf_ref, cq_ref, ck32_ref, ck16_ref, cv32_ref, cv16_ref):
    bd64 = bd64_ref[...]
    bd128 = bd128_ref[...]
    tm = z_ref.shape[0]
    lane = _lane_iota((tm, LANES))
    low = lane < 64

    cosa4 = jnp.concatenate([cosa_ref[...]] * 4, axis=1)
    sina4 = jnp.concatenate([sina_ref[...]] * 4, axis=1)
    cosc4 = jnp.concatenate([cosc_ref[...]] * 4, axis=1)
    sinc4 = jnp.concatenate([sinc_ref[...]] * 4, axis=1)

    a_scale = A_QK ** -0.5 * LOG2E
    qr = _seg_norm(z_ref[:, Z_QR:Z_QR + 512], bd64, 64, gqr_ref[...])
    qr = _rope(qr, cosa4, sina4, A_ROPE // 2) * a_scale
    for half in range(2):
        c0 = Z_QN + half * 512
        qn = _seg_norm(z_ref[:, c0:c0 + 512], bd128, 128, gqn_ref[:, half * 512:(half + 1) * 512]) * a_scale
        for hh in range(4):
            h = half * 4 + hh
            aq_ref[:, h * 256:h * 256 + 128] = qn[:, hh * 128:(hh + 1) * 128].astype(BF16)
    for h in range(A_HEADS):
        pair = qr[:, (h // 2) * 128:(h // 2 + 1) * 128]
        keep = low if h % 2 == 0 else jnp.logical_not(low)
        aq_ref[:, h * 256 + 128:h * 256 + 256] = jnp.where(keep, pair, 0.0).astype(BF16)

    ckv = _seg_norm(z_ref[:, Z_CKV:Z_CKV + 512], ones_ref[...], 512, gckv_ref[...])
    ckv_ref[...] = ckv
    ckv16 = ckv.astype(BF16)
    va_ref[...] = _dot(ckv16, wv_ref[...]).astype(BF16)

    kr = _seg_norm(z_ref[:, Z_KR:Z_KR + 128], bdkr_ref[...], 64, gkr_ref[...])
    kr = _rope(kr, cosa_ref[...], sina_ref[...], A_ROPE // 2)
    kr_ref[...] = kr[:, :A_ROPE]
    kr2 = kr + pltpu.roll(kr, 64, 1)
    kr_even = jnp.where(low, kr2, 0.0).astype(BF16)
    kr_odd = jnp.where(low, 0.0, kr2).astype(BF16)
    for half in range(2):
        kk = _dot(ckv16, wk_ref[:, half * 512:(half + 1) * 512])
        kn = _seg_norm(kk, bd128, 128, gkn_ref[:, half * 512:(half + 1) * 512])
        for hh in range(4):
            h = half * 4 + hh
            kf_ref[:, h * 256:h * 256 + 128] = kn[:, hh * 128:(hh + 1) * 128].astype(BF16)
            kf_ref[:, h * 256 + 128:h * 256 + 256] = kr_even if h % 2 == 0 else kr_odd

    bq = _seg_norm(z_ref[:, Z_BQ:Z_BQ + 512], bd128, 128, gbq_ref[...]) * (B_DIM ** -0.5 * LOG2E)
    bq_ref[...] = bq.astype(BF16)
    bk = _seg_norm(z_ref[:, Z_BK:Z_BK + 512], bd128, 128, gbk_ref[...])
    _store_heads(bk32_ref, bk)
    bk16_ref[...] = bk.astype(BF16)
    bv = z_ref[:, Z_BV:Z_BV + 512]
    _store_heads(bv32_ref, bv)
    bv16_ref[...] = bv.astype(BF16)
    f = z_ref[:, Z_BF:Z_BF + 128][:, 0:B_HEADS] + bfg_ref[:, 0:B_HEADS]
    lf_ref[...] = jnp.minimum(f, 0.0) - jnp.log1p(jnp.exp(-jnp.abs(f)))

    cq = _seg_norm(z_ref[:, Z_CQ:Z_CQ + 512], bd64, 64, gcq_ref[...])
    cq_ref[...] = (_rope(cq, cosc4, sinc4, C_ROT // 2) * (C_QK ** -0.5 * LOG2E)).astype(BF16)
    ck = _seg_norm(z_ref[:, Z_CK:Z_CK + 512], bd64, 64, gck_ref[...])
    ck = _rope(ck, cosc4, sinc4, C_ROT // 2)
    _store_heads(ck32_ref, ck)
    ck16_ref[...] = ck.astype(BF16)
    cv = z_ref[:, Z_CV:Z_CV + 512]
    _store_heads(cv32_ref, cv)
    cv16_ref[...] = cv.astype(BF16)


def _post(z, tabs, consts, tm):
    m = z.shape[0]
    tab_rows = tabs[0].shape[0]
    tab_blocks = tab_rows // tm

    def row(w):
        return pl.BlockSpec((tm, w), lambda i: (i, 0))

    def full(a):
        return pl.BlockSpec(a.shape, lambda i: (0, 0))

    tab_spec = pl.BlockSpec((tm, LANES), lambda i: (i % tab_blocks, 0))
    widths = [(2048, BF16, 0), (2048, BF16, 0), (1024, BF16, 0), (512, F32, 0), (64, F32, 0), (512, BF16, 0),
              (512, F32, B_HEADS), (512, BF16, 0), (512, F32, B_HEADS), (512, BF16, 0), (B_HEADS, F32, 0),
              (512, BF16, 0), (512, F32, C_HEADS), (512, BF16, 0), (512, F32, C_HEADS), (512, BF16, 0)]
    out_spec = lambda w, hd: row(w) if hd == 0 else pl.BlockSpec((tm * hd, w // hd), lambda i: (i, 0))
    out_sds = lambda w, dt, hd: jax.ShapeDtypeStruct((m, w) if hd == 0 else (m * hd, w // hd), dt)
    return pl.pallas_call(
        _post_kernel,
        grid=(m // tm,),
        in_specs=[row(Z_COLS)] + [tab_spec] * 4 + [full(c) for c in consts],
        out_specs=[out_spec(w, hd) for w, _, hd in widths],
        out_shape=[out_sds(w, dt, hd) for w, dt, hd in widths],
        compiler_params=_params(("arbitrary",)),
        name="post",
    )(z, *tabs, *consts)


def _split3(c):
    hi = c.astype(BF16).astype(F32)
    r1 = c - hi
    mid = r1.astype(BF16).astype(F32)
    lo = r1 - mid
    return hi, mid, lo


def _cumsum_block(lf, carry):
    n = lf.shape[0]
    r = lax.broadcasted_iota(jnp.int32, (n, n), 0)
    c = lax.broadcasted_iota(jnp.int32, (n, n), 1)
    tri = jnp.where(c <= r, 1.0, 0.0).astype(BF16)
    hi, mid, lo = _split3(lf)
    cum = _dot(tri, hi.astype(BF16)) + _dot(tri, mid.astype(BF16)) + _dot(tri, lo.astype(BF16))
    return cum + carry


def _aug_q(c, lane):
    hi, mid, lo = _split3(c)
    return jnp.where(lane == 0, hi, jnp.where(lane == 1, mid, jnp.where(lane == 2, lo,
                     jnp.where(lane < 6, 1.0, 0.0)))).astype(BF16)


def _aug_k(c, lane):
    hi, mid, lo = _split3(c)
    return jnp.where(lane < 3, 1.0, jnp.where(lane == 3, -hi, jnp.where(lane == 4, -mid,
                     jnp.where(lane == 5, -lo, 0.0)))).astype(BF16)


def _fox_prep_kernel(lf_ref, bq_ref, bk_ref, *out_refs, past, blocks):
    carry_scr = out_refs[-1]
    if past:
        qa_ref, ka_ref, kpast_ref = out_refs[:-1]
    else:
        qa_ref, ka_ref = out_refs[:-1]

    @pl.when(pl.program_id(1) == 0)
    def _():
        carry_scr[...] = jnp.zeros(carry_scr.shape, F32)

    carry = carry_scr[...]
    for r0, n in blocks:
        cum = _cumsum_block(lf_ref[0, r0:r0 + n, :], carry)
        carry = cum[n - 1:n, :]
        lane = _lane_iota((n, LANES))
        for h in range(B_HEADS):
            c = cum[:, h:h + 1] * LOG2E
            if r0 < past:
                kpast_ref[0, r0:r0 + n, h * 128:(h + 1) * 128] = _aug_k(c, lane)
            else:
                t0 = r0 - past
                qa_ref[t0:t0 + n, h * 256:h * 256 + 128] = bq_ref[t0:t0 + n, h * 128:(h + 1) * 128]
                qa_ref[t0:t0 + n, h * 256 + 128:h * 256 + 256] = _aug_q(c, lane)
                ka_ref[t0:t0 + n, h * 256:h * 256 + 128] = bk_ref[t0:t0 + n, h * 128:(h + 1) * 128]
                ka_ref[t0:t0 + n, h * 256 + 128:h * 256 + 256] = _aug_k(c, lane)
    carry_scr[...] = carry


def _fox_prep(lf_all, bq, bk, t_new, rows, blocks):
    s, t_tot, _ = lf_all.shape
    past = t_tot - t_new
    nb = t_tot // rows
    new_rows = rows - past
    m = bq.shape[0]
    new_spec = lambda w: pl.BlockSpec((new_rows, w), lambda i, j: (i * nb + j, 0))
    out_specs = [new_spec(1024), new_spec(1024)]
    out_shape = [jax.ShapeDtypeStruct((m, 1024), BF16), jax.ShapeDtypeStruct((m, 1024), BF16)]
    if past:
        out_specs.append(pl.BlockSpec((1, past, 512), lambda i, j: (i, 0, 0)))
        out_shape.append(jax.ShapeDtypeStruct((s, past, 512), BF16))
    return pl.pallas_call(
        functools.partial(_fox_prep_kernel, past=past, blocks=blocks),
        grid=(s, nb),
        in_specs=[pl.BlockSpec((1, rows, B_HEADS), lambda i, j: (i, j, 0)), new_spec(512), new_spec(512)],
        out_specs=out_specs,
        out_shape=out_shape,
        scratch_shapes=[pltpu.VMEM((1, B_HEADS), F32)],
        compiler_params=_params(("arbitrary", "arbitrary")),
        name="fox_prep",
    )(lf_all, bq, bk)


def _lanes(x, n):
    return x if n == 1 else jnp.concatenate([x] * n, axis=1)


def _with_ones(v):
    return jnp.concatenate([v, jnp.ones_like(v)], axis=1)


def _softmax_start(rows):
    return jnp.full((rows, LANES), NEG_INF, F32), jnp.zeros((rows, 2 * LANES), F32)


def _softmax_step(state, s, v1):
    m_old, acc = state
    tk = s.shape[1]
    m_new = jnp.maximum(m_old, jnp.max(s, axis=-1, keepdims=True))
    alpha = jnp.exp2(m_old - m_new)
    m_full = _lanes(m_new, tk // LANES) if tk >= LANES else m_new[:, :tk]
    p = jnp.exp2(s - m_full)
    return m_new, _lanes(alpha, 2) * acc + _dot(p.astype(BF16), v1)


def _softmax_out(state):
    acc = state[1]
    return acc[:, :LANES] / acc[:, LANES:]


def _visible(kind, q0, k0, tq, tk):
    qp = q0 + lax.broadcasted_iota(jnp.int32, (tq, tk), 0)
    kp = k0 + lax.broadcasted_iota(jnp.int32, (tq, tk), 1)
    if kind == "chunk":
        return (kp // CHUNK) <= (qp // CHUNK)
    return kp <= qp


def _split_maps(q, n_maps):
    if n_maps == 1:
        return [q]
    low = _lane_iota(q.shape) < C_QK
    zero = jnp.zeros_like(q)
    return [jnp.where(low, q, zero), jnp.where(low, zero, q)]


def _diff_lambda(lq1_ref, lk1_ref, lq2_ref, lk2_ref, lam_init):
    s1 = jnp.sum(lq1_ref[...] * lk1_ref[...], axis=-1, keepdims=True)
    s2 = jnp.sum(lq2_ref[...] * lk2_ref[...], axis=-1, keepdims=True)
    return jnp.exp(s1) - jnp.exp(s2) + lam_init


def _attn_finish(states, extra_refs, lam_init):
    if len(states) == 1:
        return _softmax_out(states[0])
    lq1_ref, lk1_ref, lq2_ref, lk2_ref, gout_ref = extra_refs
    lam = _diff_lambda(lq1_ref, lk1_ref, lq2_ref, lk2_ref, lam_init)
    o = _softmax_out(states[0]) - lam * _softmax_out(states[1])
    ms = jnp.mean(o * o, axis=-1, keepdims=True)
    return o * lax.rsqrt(ms + EPS) * gout_ref[...] * (1.0 - lam_init)


def _attn_prompt_kernel(*refs, n_maps, mask, tq, tk, sub, lam_init):
    q_ref, k_ref, v_ref = refs[:3]
    extra_refs = refs[3:-3]
    o_ref, m_scr, acc_scr = refs[-3:]
    qi = pl.program_id(2)
    n_sub = tq // sub
    m0, acc0 = _softmax_start(sub)
    for i in range(n_maps):
        for r in range(n_sub):
            m_scr[i, r] = m0
            acc_scr[i, r] = acc0

    def block(k0, nk, diag):
        k = k_ref[pl.ds(k0, nk), :]
        v1 = _with_ones(v_ref[pl.ds(k0, nk), :])
        todo = []
        for r in range(n_sub):
            vis = None
            if diag is not None:
                if diag > r * sub + sub - 1:
                    continue
                if diag + nk - 1 > r * sub:
                    vis = _visible(mask, r * sub, diag, sub, nk)
            todo.append((r, vis))
        scores = {r: [_dot_nt(qm, k) for qm in _split_maps(q_ref[r * sub:(r + 1) * sub, :], n_maps)]
                  for r, _ in todo}
        for r, vis in todo:
            for i in range(n_maps):
                s = scores[r][i] if vis is None else jnp.where(vis, scores[r][i], NEG_INF)
                m_scr[i, r], acc_scr[i, r] = _softmax_step((m_scr[i, r], acc_scr[i, r]), s, v1)

    def full_block(kb, carry):
        block(pl.multiple_of(kb * tq, tq), tq, None)
        return carry

    lax.fori_loop(0, qi, full_block, 0)
    for d in range(tq // tk):
        block(pl.multiple_of(qi * tq + d * tk, tk), tk, d * tk)
    for r in range(n_sub):
        states = [(m_scr[i, r], acc_scr[i, r]) for i in range(n_maps)]
        o_ref[r * sub:(r + 1) * sub, :] = _attn_finish(states, extra_refs, lam_init).astype(o_ref.dtype)


def _attn_prompt(q, k, v, extras, *, batch, heads, t, dqk, n_maps, mask, lam_init=0.0):
    tq, tk, sub = ATTN_TQ, ATTN_TK, ATTN_SUB
    nq = t // tq
    dv = 128
    extra_specs = [pl.BlockSpec(e.shape, lambda b, h, i: (0, 0)) for e in extras]
    return pl.pallas_call(
        functools.partial(_attn_prompt_kernel, n_maps=n_maps, mask=mask, tq=tq, tk=tk, sub=sub,
                          lam_init=lam_init),
        grid=(batch, heads, nq),
        in_specs=[pl.BlockSpec((tq, dqk), lambda b, h, i: (b * nq + i, h)),
                  pl.BlockSpec((t, dqk), lambda b, h, i: (b, h)),
                  pl.BlockSpec((t, dv), lambda b, h, i: (b, h))] + extra_specs,
        out_specs=pl.BlockSpec((tq, dv), lambda b, h, i: (b * nq + i, h)),
        out_shape=jax.ShapeDtypeStruct((batch * t, heads * dv), BF16),
        scratch_shapes=[pltpu.VMEM((n_maps, tq // sub, sub, LANES), F32),
                        pltpu.VMEM((n_maps, tq // sub, sub, 2 * LANES), F32)],
        compiler_params=_params(("arbitrary", "arbitrary", "arbitrary")),
        name="attn_prompt_" + mask + str(n_maps),
    )(q, k, v, *extras)


def _mla_sample_kernel(q_ref, kf_ref, va_ref, ckv_ref, kr_ref, wukv_ref, gkn_ref, o_ref, *, past, t):
    ckv_p = ckv_ref[0, 0]
    kr_p = kr_ref[0, 0]
    vis = _visible("chunk", past, past, t, t)
    up = lambda h: _dot(ckv_p, wukv_ref[:, h * 256:(h + 1) * 256])
    kv_next = up(0)
    for h in range(A_HEADS):
        kv = kv_next
        if h + 1 < A_HEADS:
            kv_next = up(h + 1)
        kk = kv[:, :A_NOPE]
        ms = jnp.mean(kk * kk, axis=-1, keepdims=True)
        kn = (kk * lax.rsqrt(ms + EPS) * gkn_ref[...]).astype(BF16)
        vp = kv[:, A_NOPE:].astype(BF16)
        q = q_ref[:, h * 256:(h + 1) * 256]
        r0 = 128 + (h % 2) * A_ROPE
        s = _dot_nt(q[:, :128], kn) + _dot_nt(q[:, r0:r0 + A_ROPE], kr_p)
        state = _softmax_step(_softmax_start(t), s, _with_ones(vp))
        s = jnp.where(vis, _dot_nt(q, kf_ref[:, h * 256:(h + 1) * 256]), NEG_INF)
        state = _softmax_step(state, s, _with_ones(va_ref[:, h * 128:(h + 1) * 128]))
        o_ref[:, h * 128:(h + 1) * 128] = _softmax_out(state).astype(o_ref.dtype)


def _mla_sample(q, kf, va, cache_ckv, cache_kr, w_ukv, gkn, *, layer, t):
    _, s, past, rank = cache_ckv.shape
    return pl.pallas_call(
        functools.partial(_mla_sample_kernel, past=past, t=t),
        grid=(s,),
        in_specs=[pl.BlockSpec((t, 2048), lambda i: (i, 0)),
                  pl.BlockSpec((t, 2048), lambda i: (i, 0)),
                  pl.BlockSpec((t, 1024), lambda i: (i, 0)),
                  pl.BlockSpec((1, 1, past, rank), lambda i: (layer, i, 0, 0)),
                  pl.BlockSpec((1, 1, past, A_ROPE), lambda i: (layer, i, 0, 0)),
                  pl.BlockSpec(w_ukv.shape, lambda i: (0, 0)),
                  pl.BlockSpec(gkn.shape, lambda i: (0, 0))],
        out_specs=pl.BlockSpec((t, 1024), lambda i: (i, 0)),
        out_shape=jax.ShapeDtypeStruct((s * t, 1024), BF16),
        compiler_params=_params(("arbitrary",)),
        name="mla_sample",
    )(q, kf, va, cache_ckv, cache_kr, w_ukv, gkn)


def _bc_sample_kernel(*refs, n_maps, mask, heads, past, t, lam_init, with_aug):
    q_ref, kn_ref, vn_ref, kp_ref, vp_ref = refs[:5]
    rest = refs[5:-1]
    o_ref = refs[-1]
    if with_aug:
        kaug_ref, extra_refs = rest[0], rest[1:]
    else:
        extra_refs = rest
    dq = q_ref.shape[1] // heads
    vis = _visible(mask, past, past, t, t)
    for h in range(heads):
        q = q_ref[:, h * dq:(h + 1) * dq]
        kp = kp_ref[0, 0, pl.ds(h, past, stride=heads), :].astype(BF16)
        vp = _with_ones(vp_ref[0, 0, pl.ds(h, past, stride=heads), :].astype(BF16))
        kn = kn_ref[:, h * dq:(h + 1) * dq]
        vn = _with_ones(vn_ref[:, h * 128:(h + 1) * 128])
        states = []
        if with_aug:
            s = _dot_nt(q[:, :128], kp) + _dot_nt(q[:, 128:], kaug_ref[0, :, h * 128:(h + 1) * 128])
            state = _softmax_step(_softmax_start(t), s, vp)
            states.append(_softmax_step(state, jnp.where(vis, _dot_nt(q, kn), NEG_INF), vn))
        else:
            for qm in _split_maps(q, n_maps):
                state = _softmax_step(_softmax_start(t), _dot_nt(qm, kp), vp)
                states.append(_softmax_step(state, jnp.where(vis, _dot_nt(qm, kn), NEG_INF), vn))
        o_ref[:, h * 128:(h + 1) * 128] = _attn_finish(states, extra_refs, lam_init).astype(o_ref.dtype)


def _bc_sample(q, kn, vn, cache_k, cache_v, kaug, extras, *, layer, heads, n_maps, mask, t, lam_init=0.0):
    _, s, rows, _ = cache_k.shape
    past = rows // heads
    dq = q.shape[1]
    cache_spec = pl.BlockSpec((1, 1, rows, 128), lambda i: (layer, i, 0, 0))
    in_specs = [pl.BlockSpec((t, dq), lambda i: (i, 0)),
                pl.BlockSpec((t, dq), lambda i: (i, 0)),
                pl.BlockSpec((t, heads * 128), lambda i: (i, 0)),
                cache_spec, cache_spec]
    args = [q, kn, vn, cache_k, cache_v]
    if kaug is not None:
        in_specs.append(pl.BlockSpec((1, past, heads * 128), lambda i: (i, 0, 0)))
        args.append(kaug)
    in_specs += [pl.BlockSpec(e.shape, lambda i: (0, 0)) for e in extras]
    return pl.pallas_call(
        functools.partial(_bc_sample_kernel, n_maps=n_maps, mask=mask, heads=heads, past=past, t=t,
                          lam_init=lam_init, with_aug=kaug is not None),
        grid=(s,),
        in_specs=in_specs,
        out_specs=pl.BlockSpec((t, heads * 128), lambda i: (i, 0)),
        out_shape=jax.ShapeDtypeStruct((s * t, heads * 128), BF16),
        compiler_params=_params(("arbitrary",)),
        name="sample_" + mask + str(n_maps),
    )(*args, *extras)


def _mm_res_kernel(*refs, n_in):
    res_ref = refs[2 * n_in]
    o_ref = refs[2 * n_in + 1]
    acc = res_ref[...]
    for i in range(n_in):
        acc = acc + _dot(refs[i][...], refs[n_in + i][...])
    o_ref[...] = acc


def _mm_res(res, a_list, w_list, tm, tn):
    m, n = res.shape
    n_in = len(a_list)
    in_specs = ([pl.BlockSpec((tm, a.shape[1]), lambda j, i: (i, 0)) for a in a_list]
                + [pl.BlockSpec((w.shape[0], tn), lambda j, i: (0, j)) for w in w_list]
                + [pl.BlockSpec((tm, tn), lambda j, i: (i, j))])
    return pl.pallas_call(
        functools.partial(_mm_res_kernel, n_in=n_in),
        grid=(n // tn, m // tm),
        in_specs=in_specs,
        out_specs=pl.BlockSpec((tm, tn), lambda j, i: (i, j)),
        out_shape=jax.ShapeDtypeStruct((m, n), F32),
        compiler_params=_params(("arbitrary", "arbitrary")),
        name="mm_res%d" % n_in,
    )(*a_list, *w_list, res)


def _ffn_up_kernel(x_ref, g_ref, wg_ref, wv_ref, cwg_ref, cwv_ref, cbg_ref, cbv_ref, sg_ref, sv_ref,
                   a_ref, ng_ref, nv_ref, h_scr, rawg0, rawv0, rawg1, rawv1, carg, carv,
                   *, rows, tiles_per_stream, nj):
    i = pl.program_id(0)
    j = pl.program_id(1)
    tm = x_ref.shape[0]
    n_sub = tm // rows
    tn = a_ref.shape[1]
    chunks = [(c, c + FFN_CHUNK) for c in range(0, tn, FFN_CHUNK)]
    raws = ((rawg0, rawv0), (rawg1, rawv1))

    def matmuls():
        h = h_scr[...]
        return [(_dot(h, wg_ref[:, c0:c1]), _dot(h, wv_ref[:, c0:c1])) for c0, c1 in chunks]

    def keep(dots, slot):
        for (c0, c1), us in zip(chunks, dots):
            for u, raw in zip(us, raws[slot]):
                for s in range(n_sub):
                    raw[s, HALO_ROW:HALO_ROW + rows, c0:c1] = u[s * rows:(s + 1) * rows]

    def epilogue(slot):
        jm = j - 1
        sides = ((raws[slot][0], carg, sg_ref, cwg_ref, cbg_ref, ng_ref),
                 (raws[slot][1], carv, sv_ref, cwv_ref, cbv_ref, nv_ref))
        for raw, car, s_ref, _, _, n_ref in sides:
            for s in range(n_sub):
                if tiles_per_stream == 1:
                    halo = s_ref[0, s]
                else:
                    halo = jnp.where((i % tiles_per_stream) == 0, s_ref[0, 0], car[jm])
                raw[s, HALO_ROW - 2:HALO_ROW, :] = halo
                n_ref[s] = raw[s, HALO_ROW + rows - 2:HALO_ROW + rows, :]
            if tiles_per_stream > 1:
                car[jm] = raw[n_sub - 1, HALO_ROW + rows - 2:HALO_ROW + rows, :]

        def conv(side, s, r0, c0, c1):
            raw, _, _, cw_ref, cb_ref, _ = side
            base = HALO_ROW + r0
            return (cb_ref[:, c0:c1]
                    + raw[s, base - 2:base - 2 + FFN_ROWS, c0:c1] * cw_ref[0:1, c0:c1]
                    + raw[s, base - 1:base - 1 + FFN_ROWS, c0:c1] * cw_ref[1:2, c0:c1]
                    + raw[s, base:base + FFN_ROWS, c0:c1] * cw_ref[2:3, c0:c1])

        for c0, c1 in chunks:
            for s in range(n_sub):
                for r0 in range(0, rows, FFN_ROWS):
                    gate = conv(sides[0], s, r0, c0, c1)
                    val = conv(sides[1], s, r0, c0, c1)
                    a_ref[s * rows + r0:s * rows + r0 + FFN_ROWS, c0:c1] = (
                        gate * (1.0 / (1.0 + jnp.exp(-gate))) * val).astype(BF16)

    @pl.when(j == 0)
    def _():
        x = x_ref[...]
        ms = jnp.mean(x * x, axis=-1, keepdims=True)
        h_scr[...] = (x * lax.rsqrt(ms + EPS) * g_ref[...]).astype(BF16)
        keep(matmuls(), 0)

    for slot in range(2):
        @pl.when((j > 0) & (j < nj) & (j % 2 == slot))
        def _():
            epilogue(1 - slot)
            keep(matmuls(), slot)

    @pl.when(j == nj)
    def _():
        epilogue((nj - 1) % 2)


def _ffn_up(x, g, w_up, conv_w, conv_b, state, *, layer, stream_len, tm, tn):
    m, d = x.shape
    nj = D_FF // tn
    if stream_len >= tm:
        rows, tiles_per_stream, n_sub = tm, stream_len // tm, 1
        stream_of = lambda i: i // tiles_per_stream
    else:
        rows, tiles_per_stream, n_sub = stream_len, 1, tm // stream_len
        stream_of = lambda i: i
    mm = lambda j: jnp.minimum(j, nj - 1)
    ep = lambda j: jnp.maximum(j - 1, 0)
    raw = pltpu.VMEM((n_sub, HALO_ROW + rows, tn), F32)
    outs = pl.pallas_call(
        functools.partial(_ffn_up_kernel, rows=rows, tiles_per_stream=tiles_per_stream, nj=nj),
        grid=(m // tm, nj + 1),
        in_specs=[pl.BlockSpec((tm, d), lambda i, j: (i, 0)),
                  pl.BlockSpec((1, d), lambda i, j: (0, 0)),
                  pl.BlockSpec((d, tn), lambda i, j: (0, mm(j))),
                  pl.BlockSpec((d, tn), lambda i, j: (0, nj + mm(j))),
                  pl.BlockSpec((CONV_W, tn), lambda i, j: (0, ep(j))),
                  pl.BlockSpec((CONV_W, tn), lambda i, j: (0, nj + ep(j))),
                  pl.BlockSpec((1, tn), lambda i, j: (0, ep(j))),
                  pl.BlockSpec((1, tn), lambda i, j: (0, nj + ep(j))),
                  pl.BlockSpec((1, n_sub, 2, tn), lambda i, j: (layer, stream_of(i), 0, ep(j))),
                  pl.BlockSpec((1, n_sub, 2, tn), lambda i, j: (layer, stream_of(i), 0, nj + ep(j)))],
        out_specs=[pl.BlockSpec((tm, tn), lambda i, j: (i, ep(j))),
                   pl.BlockSpec((n_sub, 2, tn), lambda i, j: (i, 0, ep(j))),
                   pl.BlockSpec((n_sub, 2, tn), lambda i, j: (i, 0, ep(j)))],
        out_shape=[jax.ShapeDtypeStruct((m, D_FF), BF16),
                   jax.ShapeDtypeStruct((m // rows, 2, D_FF), F32),
                   jax.ShapeDtypeStruct((m // rows, 2, D_FF), F32)],
        scratch_shapes=[pltpu.VMEM((tm, d), BF16), raw, raw, raw, raw,
                        pltpu.VMEM((nj, 2, tn), F32), pltpu.VMEM((nj, 2, tn), F32)],
        compiler_params=_params(("arbitrary", "arbitrary")),
        name="ffn_up",
    )(x, g, w_up, w_up, conv_w, conv_w, conv_b, conv_b, state, state)
    a, tail_g, tail_v = outs
    last = slice(tiles_per_stream - 1, None, tiles_per_stream)
    return a, tail_g[last], tail_v[last]


def _rope_tables(pos, reps):
    posf = pos.astype(F32)[:, None]

    def tab(n_rot):
        half = n_rot // 2
        inv = jnp.power(jnp.float32(ROPE_THETA), -jnp.arange(half, dtype=F32) * (2.0 / n_rot))
        ang = posf * inv[None, :]
        rest = 64 - n_rot
        cos = jnp.concatenate([jnp.cos(ang), jnp.cos(ang), jnp.ones((pos.shape[0], rest), F32)], axis=1)
        sin = jnp.concatenate([-jnp.sin(ang), jnp.sin(ang), jnp.zeros((pos.shape[0], rest), F32)], axis=1)
        return jnp.tile(cos, (reps, 2)), jnp.tile(sin, (reps, 2))

    cosa, sina = tab(A_ROPE)
    cosc, sinc = tab(C_ROT)
    return cosa, sina, cosc, sinc


def _block_diag(width, seg):
    r = jnp.arange(width)[:, None] // seg
    c = jnp.arange(width)[None, :] // seg
    return (r == c).astype(BF16)


def _layer_weights(l, attn_norm, w_in, b_forget, a_kv_norm, a_w_ukv, a_qn_nope, a_qn_rope, a_kn_nope,
                   a_kn_rope, b_qn, b_kn, c_qn, c_kn, c_lq1, c_lk1, c_lq2, c_lk2, c_out_norm, w_out,
                   ffn_norm, w_up, conv_w, conv_b, w_down):
    w_in_p = _permute_w_in(w_in, l, 256)
    ukv = a_w_ukv[l].reshape(A_KV_RANK, A_HEADS, A_NOPE + A_V)
    wk = ukv[:, :, :A_NOPE].reshape(A_KV_RANK, 1024).astype(BF16)
    wv = ukv[:, :, A_NOPE:].reshape(A_KV_RANK, 1024).astype(BF16)
    row = lambda v: v.reshape(1, -1).astype(F32)
    tile = lambda v, n: jnp.tile(v.astype(F32), n).reshape(1, -1)
    bd_kr = (jnp.arange(128)[:, None] < 64).astype(BF16) * jnp.ones((1, 128), BF16)
    post_consts = [
        tile(a_qn_nope[l], 8), tile(a_qn_rope[l], 8), row(a_kv_norm[l]),
        jnp.concatenate([a_kn_rope[l], jnp.zeros((64,), F32)]).reshape(1, 128),
        tile(a_kn_nope[l], 8), tile(b_qn[l], 4), tile(b_kn[l], 4), tile(c_qn[l], 8), tile(c_kn[l], 8),
        jnp.concatenate([b_forget[l], jnp.zeros((128 - B_HEADS,), F32)]).reshape(1, 128),
        _block_diag(512, 64), _block_diag(512, 128), jnp.ones((512, 512), BF16), bd_kr, wk, wv]
    wo = w_out[l].astype(BF16)
    return dict(
        attn_norm=row(attn_norm[l]), w_in=w_in_p, post_consts=post_consts,
        w_ukv=a_w_ukv[l].astype(BF16), gkn=row(a_kn_nope[l]),
        diff_extras=[row(c_lq1[l]), row(c_lk1[l]), row(c_lq2[l]), row(c_lk2[l]), row(c_out_norm[l])],
        wo_a=wo[:1024], wo_b=wo[1024:1536], wo_c=wo[1536:],
        ffn_norm=row(ffn_norm[l]), w_up=w_up[l].astype(BF16), conv_w=conv_w[l].astype(F32),
        conv_b=row(conv_b[l]), w_down=w_down[l].astype(BF16))


def _run_layer(x, lw, tabs, past, *, n_streams, t, lam_init, post_tm, ffn_tm):
    m = x.shape[0]
    z = _norm_mm(x, lw["attn_norm"], lw["w_in"], DENSE_TM, Z_TN)
    (aq, kf, va, ckv, kr, bq, bk32, bk16, bv32, bv16, lf, cq, ck32, ck16, cv32, cv16) = _post(
        z, tabs, lw["post_consts"], post_tm)

    if past is None:
        qa, ka = _fox_prep(lf.reshape(n_streams, t, B_HEADS), bq, bk16, t, 512, ((0, 512),))
        oa = _attn_prompt(aq, kf, va, [], batch=n_streams, heads=A_HEADS, t=t, dqk=256, n_maps=1,
                          mask="chunk")
        ob = _attn_prompt(qa, ka, bv16, [], batch=n_streams, heads=B_HEADS, t=t, dqk=256, n_maps=1,
                          mask="causal")
        oc = _attn_prompt(cq, ck16, cv16, lw["diff_extras"], batch=n_streams, heads=C_HEADS, t=t, dqk=128,
                          n_maps=2, mask="chunk", lam_init=lam_init)
        conv_state, conv_layer = jnp.zeros((1, n_streams, CONV_W - 1, 2 * D_FF), F32), 0
    else:
        layer, c_ckv, c_kr, c_bk, c_bv, c_lf, c_ck, c_cv, conv_state = past
        p_len = c_ckv.shape[2]
        conv_state, conv_layer = conv_state[layer][None], 0
        lf_all = jnp.concatenate([c_lf[layer], lf.reshape(n_streams, t, B_HEADS)], axis=1)
        blocks = tuple((r, 512) for r in range(0, p_len, 512)) + ((p_len, t),)
        qa, ka, kaug = _fox_prep(lf_all, bq, bk16, t, p_len + t, blocks)
        oa = _mla_sample(aq, kf, va, c_ckv, c_kr, lw["w_ukv"], lw["gkn"], layer=layer, t=t)
        ob = _bc_sample(qa, ka, bv16, c_bk, c_bv, kaug, [], layer=layer, heads=B_HEADS,
                        n_maps=1, mask="causal", t=t)
        oc = _bc_sample(cq, ck16, cv16, c_ck, c_cv, None, lw["diff_extras"], layer=layer,
                        heads=C_HEADS, n_maps=2, mask="chunk", t=t, lam_init=lam_init)

    x1 = _mm_res(x, [oa, ob, oc], [lw["wo_a"], lw["wo_b"], lw["wo_c"]], 512, 1024)
    a, conv_g, conv_v = _ffn_up(x1, lw["ffn_norm"], lw["w_up"], lw["conv_w"], lw["conv_b"], conv_state,
                                layer=conv_layer, stream_len=t, tm=ffn_tm, tn=512)
    x2 = _mm_res(x1, [a], [lw["w_down"]], 512, 1024)
    states = (ckv.reshape(n_streams, t, A_KV_RANK), kr.reshape(n_streams, t, A_ROPE),
              bk32.reshape(n_streams, t, B_HEADS, B_DIM), bv32.reshape(n_streams, t, B_HEADS, B_DIM),
              lf.reshape(n_streams, t, B_HEADS),
              ck32.reshape(n_streams, t, C_HEADS, 2 * C_QK), cv32.reshape(n_streams, t, C_HEADS, C_V),
              jnp.concatenate([conv_g, conv_v], axis=-1))
    return x2, states


def kernel(x_prompt, x_sample, cache_a_ckv, cache_a_krope, cache_b_k, cache_b_v, cache_b_logf, cache_c_k, cache_c_v, state_ffn_conv, attn_norm, w_in, b_forget, a_kv_norm, a_w_ukv, a_qn_nope, a_qn_rope, a_kn_nope, a_kn_rope, b_qn, b_kn, c_qn, c_kn, c_lq1, c_lk1, c_lq2, c_lk2, c_out_norm, w_out, ffn_norm, w_up, conv_w, conv_b, w_down):
    bp, tp, d = x_prompt.shape
    bs, ts, _ = x_sample.shape
    depth = w_in.shape[0]
    past_len = cache_a_ckv.shape[2]
    post_tm = 256
    tabs_p = _rope_tables(jnp.arange(tp, dtype=jnp.int32), 1)
    tabs_s = _rope_tables(past_len + jnp.arange(ts, dtype=jnp.int32), post_tm // ts)

    merge_heads = lambda c: c.reshape(c.shape[:2] + (c.shape[2] * c.shape[3], c.shape[4]))
    caches = (cache_a_ckv.astype(BF16), cache_a_krope.astype(BF16), merge_heads(cache_b_k),
              merge_heads(cache_b_v), cache_b_logf, merge_heads(cache_c_k), merge_heads(cache_c_v),
              state_ffn_conv)
    yp = x_prompt.reshape(bp * tp, d)
    ys = x_sample.reshape(bs * ts, d)
    states_p, states_s = [], []
    for l in range(depth):
        lw = _layer_weights(l, attn_norm, w_in, b_forget, a_kv_norm, a_w_ukv, a_qn_nope, a_qn_rope,
                            a_kn_nope, a_kn_rope, b_qn, b_kn, c_qn, c_kn, c_lq1, c_lk1, c_lq2, c_lk2,
                            c_out_norm, w_out, ffn_norm, w_up, conv_w, conv_b, w_down)
        lam_init = 0.8 - 0.6 * math.exp(-0.3 * l)
        yp, st_p = _run_layer(yp, lw, tabs_p, None, n_streams=bp, t=tp, lam_init=lam_init,
                              post_tm=post_tm, ffn_tm=DENSE_TM)
        ys, st_s = _run_layer(ys, lw, tabs_s, (l,) + caches, n_streams=bs, t=ts, lam_init=lam_init,
                              post_tm=post_tm, ffn_tm=DENSE_TM)
        states_p.append(st_p)
        states_s.append(st_s)
    outs_p = [jnp.stack(s) for s in zip(*states_p)]
    outs_s = [jnp.stack(s) for s in zip(*states_s)]
    return (yp.reshape(bp, tp, d), ys.reshape(bs, ts, d), *outs_p, *outs_s)
```

```python
import functools
import math

import jax
import jax.numpy as jnp
from jax import lax
from jax.experimental import pallas as pl
from jax.experimental.pallas import tpu as pltpu

F32 = jnp.float32
BF16 = jnp.bfloat16

D_MODEL = 2048
CHUNK = 64
ROPE_THETA = 500000.0
EPS = 1e-6
NEG_INF = -1e30
LOG2E = 1.4426950408889634

A_HEADS = 8
A_NOPE = 128
A_ROPE = 64
A_V = 128
A_QK = A_NOPE + A_ROPE
A_KV_RANK = 512
B_HEADS = 4
B_DIM = 128
C_HEADS = 4
C_QK = 64
C_V = 128
C_ROT = 16
D_FF = 5632
CONV_W = 3

LANES = 128
HALO_ROW = 8
VMEM_LIMIT = 56 * 1024 * 1024

Z_QN, Z_QR, Z_CKV, Z_BQ, Z_BK, Z_BV, Z_CQ, Z_CK, Z_CV, Z_KR, Z_BF, Z_COLS = (
    0, 1024, 1536, 2048, 2560, 3072, 3584, 4096, 4608, 5120, 5248, 5376)
Z_TN = 896
DENSE_TM = 1024
FFN_CHUNK = 256
FFN_ROWS = 32
ATTN_TQ = 1024
ATTN_TK = 512
ATTN_SUB = 256


def _params(sem):
    return pltpu.CompilerParams(dimension_semantics=sem, vmem_limit_bytes=VMEM_LIMIT)


def _dot(a, b):
    return jnp.dot(a, b, preferred_element_type=F32)


def _dot_nt(a, b):
    return lax.dot_general(a, b, (((1,), (1,)), ((), ())), preferred_element_type=F32)


def _lane_iota(shape):
    return lax.broadcasted_iota(jnp.int32, shape, len(shape) - 1)


def _permute_w_in_kernel(w_ref, o_ref):
    def put(dst, src, n):
        o_ref[:, dst:dst + n] = w_ref[0, :, src:src + n]

    def clear(dst, n):
        o_ref[:, dst:dst + n] = jnp.zeros((o_ref.shape[0], n), BF16)

    for h in range(A_HEADS):
        put(Z_QN + h * A_NOPE, h * A_QK, A_NOPE)
        put(Z_QR + h * A_ROPE, h * A_QK + A_NOPE, A_ROPE)
    ckv0 = A_HEADS * A_QK
    kr0 = ckv0 + A_KV_RANK
    bq0 = kr0 + A_ROPE
    bf0 = bq0 + 3 * B_HEADS * B_DIM
    cq0 = bf0 + B_HEADS
    put(Z_CKV, ckv0, A_KV_RANK)
    put(Z_BQ, bq0, 3 * B_HEADS * B_DIM)
    put(Z_CQ, cq0, 3 * C_HEADS * C_V)
    put(Z_KR, kr0, A_ROPE)
    clear(Z_KR + A_ROPE, LANES - A_ROPE)
    put(Z_BF, bf0, B_HEADS)
    clear(Z_BF + B_HEADS, Z_COLS - Z_BF - B_HEADS)


def _permute_w_in(w_in, layer, tr):
    _, d, cols = w_in.shape
    return pl.pallas_call(
        _permute_w_in_kernel,
        grid=(d // tr,),
        in_specs=[pl.BlockSpec((1, tr, cols), lambda i: (layer, i, 0))],
        out_specs=pl.BlockSpec((tr, Z_COLS), lambda i: (i, 0)),
        out_shape=jax.ShapeDtypeStruct((d, Z_COLS), BF16),
        compiler_params=_params(("arbitrary",)),
        name="permute_w_in",
    )(w_in)


def _norm_mm_kernel(x_ref, g_ref, w_ref, o_ref, h_scr):
    @pl.when(pl.program_id(1) == 0)
    def _():
        x = x_ref[...]
        ms = jnp.mean(x * x, axis=-1, keepdims=True)
        h_scr[...] = (x * lax.rsqrt(ms + EPS) * g_ref[...]).astype(BF16)

    o_ref[...] = _dot(h_scr[...], w_ref[...])


def _norm_mm(x, g, w, tm, tn):
    m, d = x.shape
    n = w.shape[1]
    return pl.pallas_call(
        _norm_mm_kernel,
        grid=(m // tm, n // tn),
        in_specs=[pl.BlockSpec((tm, d), lambda i, j: (i, 0)),
                  pl.BlockSpec((1, d), lambda i, j: (0, 0)),
                  pl.BlockSpec((d, tn), lambda i, j: (0, j))],
        out_specs=pl.BlockSpec((tm, tn), lambda i, j: (i, j)),
        out_shape=jax.ShapeDtypeStruct((m, n), F32),
        scratch_shapes=[pltpu.VMEM((tm, d), BF16)],
        compiler_params=_params(("arbitrary", "arbitrary")),
        name="norm_mm",
    )(x, g, w)


def _seg_sumsq(x, bd):
    sq = x * x
    hi = sq.astype(BF16)
    lo = (sq - hi.astype(F32)).astype(BF16)
    return _dot(hi, bd) + _dot(lo, bd)


def _seg_norm(x, bd, seg, g):
    return x * lax.rsqrt(_seg_sumsq(x, bd) * (1.0 / seg) + EPS) * g


def _rope(x, cos, sin, half):
    n = x.shape[-1]
    first = (_lane_iota(x.shape) % 64) < half
    partner = jnp.where(first, pltpu.roll(x, n - half, 1), pltpu.roll(x, half, 1))
    return x * cos + partner * sin


def _store_heads(o_ref, x):
    rows = x.shape[0]
    heads = x.shape[1] // LANES
    for h in range(heads):
        o_ref[pl.ds(h, rows, stride=heads), :] = x[:, h * LANES:(h + 1) * LANES]


def _post_kernel(z_ref, cosa_ref, sina_ref, cosc_ref, sinc_ref,
                 gqn_ref, gqr_ref, gckv_ref, gkr_ref, gkn_ref, gbq_ref, gbk_ref, gcq_ref, gck_ref,
                 bfg_ref, bd64_ref, bd128_ref, ones_ref, bdkr_ref, wk_ref, wv_ref,
                 aq_ref, kf_ref, va_ref, ckv_ref, kr_ref, bq_ref, bk32_ref, bk16_ref,
                 bv32_ref, bv16_ref, lf_ref, cq_ref, ck32_ref, ck16_ref, cv32_ref, cv16_ref):
    bd64 = bd64_ref[...]
    bd128 = bd128_ref[...]
    tm = z_ref.shape[0]
    lane = _lane_iota((tm, LANES))
    low = lane < 64

    cosa4 = jnp.concatenate([cosa_ref[...]] * 4, axis=1)
    sina4 = jnp.concatenate([sina_ref[...]] * 4, axis=1)
    cosc4 = jnp.concatenate([cosc_ref[...]] * 4, axis=1)
    sinc4 = jnp.concatenate([sinc_ref[...]] * 4, axis=1)

    a_scale = A_QK ** -0.5 * LOG2E
    qr = _seg_norm(z_ref[:, Z_QR:Z_QR + 512], bd64, 64, gqr_ref[...])
    qr = _rope(qr, cosa4, sina4, A_ROPE // 2) * a_scale
    for half in range(2):
        c0 = Z_QN + half * 512
        qn = _seg_norm(z_ref[:, c0:c0 + 512], bd128, 128, gqn_ref[:, half * 512:(half + 1) * 512]) * a_scale
        for hh in range(4):
            h = half * 4 + hh
            aq_ref[:, h * 256:h * 256 + 128] = qn[:, hh * 128:(hh + 1) * 128].astype(BF16)
    for h in range(A_HEADS):
        pair = qr[:, (h // 2) * 128:(h // 2 + 1) * 128]
        keep = low if h % 2 == 0 else jnp.logical_not(low)
        aq_ref[:, h * 256 + 128:h * 256 + 256] = jnp.where(keep, pair, 0.0).astype(BF16)

    ckv = _seg_norm(z_ref[:, Z_CKV:Z_CKV + 512], ones_ref[...], 512, gckv_ref[...])
    ckv_ref[...] = ckv
    ckv16 = ckv.astype(BF16)
    va_ref[...] = _dot(ckv16, wv_ref[...]).astype(BF16)

    kr = _seg_norm(z_ref[:, Z_KR:Z_KR + 128], bdkr_ref[...], 64, gkr_ref[...])
    kr = _rope(kr, cosa_ref[...], sina_ref[...], A_ROPE // 2)
    kr_ref[...] = kr[:, :A_ROPE]
    kr2 = kr + pltpu.roll(kr, 64, 1)
    kr_even = jnp.where(low, kr2, 0.0).astype(BF16)
    kr_odd = jnp.where(low, 0.0, kr2).astype(BF16)
    for half in range(2):
        kk = _dot(ckv16, wk_ref[:, half * 512:(half + 1) * 512])
        kn = _seg_norm(kk, bd128, 128, gkn_ref[:, half * 512:(half + 1) * 512])
        for hh in range(4):
            h = half * 4 + hh
            kf_ref[:, h * 256:h * 256 + 128] = kn[:, hh * 128:(hh + 1) * 128].astype(BF16)
            kf_ref[:, h * 256 + 128:h * 256 + 256] = kr_even if h % 2 == 0 else kr_odd

    bq = _seg_norm(z_ref[:, Z_BQ:Z_BQ + 512], bd128, 128, gbq_ref[...]) * (B_DIM ** -0.5 * LOG2E)
    bq_ref[...] = bq.astype(BF16)
    bk = _seg_norm(z_ref[:, Z_BK:Z_BK + 512], bd128, 128, gbk_ref[...])
    _store_heads(bk32_ref, bk)
    bk16_ref[...] = bk.astype(BF16)
    bv = z_ref[:, Z_BV:Z_BV + 512]
    _store_heads(bv32_ref, bv)
    bv16_ref[...] = bv.astype(BF16)
    f = z_ref[:, Z_BF:Z_BF + 128][:, 0:B_HEADS] + bfg_ref[:, 0:B_HEADS]
    lf_ref[...] = jnp.minimum(f, 0.0) - jnp.log1p(jnp.exp(-jnp.abs(f)))

    cq = _seg_norm(z_ref[:, Z_CQ:Z_CQ + 512], bd64, 64, gcq_ref[...])
    cq_ref[...] = (_rope(cq, cosc4, sinc4, C_ROT // 2) * (C_QK ** -0.5 * LOG2E)).astype(BF16)
    ck = _seg_norm(z_ref[:, Z_CK:Z_CK + 512], bd64, 64, gck_ref[...])
    ck = _rope(ck, cosc4, sinc4, C_ROT // 2)
    _store_heads(ck32_ref, ck)
    ck16_ref[...] = ck.astype(BF16)
    cv = z_ref[:, Z_CV:Z_CV + 512]
    _store_heads(cv32_ref, cv)
    cv16_ref[...] = cv.astype(BF16)


def _post(z, tabs, consts, tm):
    m = z.shape[0]
    tab_rows = tabs[0].shape[0]
    tab_blocks = tab_rows // tm

    def row(w):
        return pl.BlockSpec((tm, w), lambda i: (i, 0))

    def full(a):
        return pl.BlockSpec(a.shape, lambda i: (0, 0))

    tab_spec = pl.BlockSpec((tm, LANES), lambda i: (i % tab_blocks, 0))
    widths = [(2048, BF16, 0), (2048, BF16, 0), (1024, BF16, 0), (512, F32, 0), (64, F32, 0), (512, BF16, 0),
              (512, F32, B_HEADS), (512, BF16, 0), (512, F32, B_HEADS), (512, BF16, 0), (B_HEADS, F32, 0),
              (512, BF16, 0), (512, F32, C_HEADS), (512, BF16, 0), (512, F32, C_HEADS), (512, BF16, 0)]
    out_spec = lambda w, hd: row(w) if hd == 0 else pl.BlockSpec((tm * hd, w // hd), lambda i: (i, 0))
    out_sds = lambda w, dt, hd: jax.ShapeDtypeStruct((m, w) if hd == 0 else (m * hd, w // hd), dt)
    return pl.pallas_call(
        _post_kernel,
        grid=(m // tm,),
        in_specs=[row(Z_COLS)] + [tab_spec] * 4 + [full(c) for c in consts],
        out_specs=[out_spec(w, hd) for w, _, hd in widths],
        out_shape=[out_sds(w, dt, hd) for w, dt, hd in widths],
        compiler_params=_params(("arbitrary",)),
        name="post",
    )(z, *tabs, *consts)


def _split3(c):
    hi = c.astype(BF16).astype(F32)
    r1 = c - hi
    mid = r1.astype(BF16).astype(F32)
    lo = r1 - mid
    return hi, mid, lo


def _cumsum_block(lf, carry):
    n = lf.shape[0]
    r = lax.broadcasted_iota(jnp.int32, (n, n), 0)
    c = lax.broadcasted_iota(jnp.int32, (n, n), 1)
    tri = jnp.where(c <= r, 1.0, 0.0).astype(BF16)
    hi, mid, lo = _split3(lf)
    cum = _dot(tri, hi.astype(BF16)) + _dot(tri, mid.astype(BF16)) + _dot(tri, lo.astype(BF16))
    return cum + carry


def _aug_q(c, lane):
    hi, mid, lo = _split3(c)
    return jnp.where(lane == 0, hi, jnp.where(lane == 1, mid, jnp.where(lane == 2, lo,
                     jnp.where(lane < 6, 1.0, 0.0)))).astype(BF16)


def _aug_k(c, lane):
    hi, mid, lo = _split3(c)
    return jnp.where(lane < 3, 1.0, jnp.where(lane == 3, -hi, jnp.where(lane == 4, -mid,
                     jnp.where(lane == 5, -lo, 0.0)))).astype(BF16)


def _fox_prep_kernel(lf_ref, bq_ref, bk_ref, *out_refs, past, blocks):
    carry_scr = out_refs[-1]
    if past:
        qa_ref, ka_ref, kpast_ref = out_refs[:-1]
    else:
        qa_ref, ka_ref = out_refs[:-1]

    @pl.when(pl.program_id(1) == 0)
    def _():
        carry_scr[...] = jnp.zeros(carry_scr.shape, F32)

    carry = carry_scr[...]
    for r0, n in blocks:
        cum = _cumsum_block(lf_ref[0, r0:r0 + n, :], carry)
        carry = cum[n - 1:n, :]
        lane = _lane_iota((n, LANES))
        for h in range(B_HEADS):
            c = cum[:, h:h + 1] * LOG2E
            if r0 < past:
                kpast_ref[0, r0:r0 + n, h * 128:(h + 1) * 128] = _aug_k(c, lane)
            else:
                t0 = r0 - past
                qa_ref[t0:t0 + n, h * 256:h * 256 + 128] = bq_ref[t0:t0 + n, h * 128:(h + 1) * 128]
                qa_ref[t0:t0 + n, h * 256 + 128:h * 256 + 256] = _aug_q(c, lane)
                ka_ref[t0:t0 + n, h * 256:h * 256 + 128] = bk_ref[t0:t0 + n, h * 128:(h + 1) * 128]
                ka_ref[t0:t0 + n, h * 256 + 128:h * 256 + 256] = _aug_k(c, lane)
    carry_scr[...] = carry


def _fox_prep(lf_all, bq, bk, t_new, rows, blocks):
    s, t_tot, _ = lf_all.shape
    past = t_tot - t_new
    nb = t_tot // rows
    new_rows = rows - past
    m = bq.shape[0]
    new_spec = lambda w: pl.BlockSpec((new_rows, w), lambda i, j: (i * nb + j, 0))
    out_specs = [new_spec(1024), new_spec(1024)]
    out_shape = [jax.ShapeDtypeStruct((m, 1024), BF16), jax.ShapeDtypeStruct((m, 1024), BF16)]
    if past:
        out_specs.append(pl.BlockSpec((1, past, 512), lambda i, j: (i, 0, 0)))
        out_shape.append(jax.ShapeDtypeStruct((s, past, 512), BF16))
    return pl.pallas_call(
        functools.partial(_fox_prep_kernel, past=past, blocks=blocks),
        grid=(s, nb),
        in_specs=[pl.BlockSpec((1, rows, B_HEADS), lambda i, j: (i, j, 0)), new_spec(512), new_spec(512)],
        out_specs=out_specs,
        out_shape=out_shape,
        scratch_shapes=[pltpu.VMEM((1, B_HEADS), F32)],
        compiler_params=_params(("arbitrary", "arbitrary")),
        name="fox_prep",
    )(lf_all, bq, bk)


def _lanes(x, n):
    return x if n == 1 else jnp.concatenate([x] * n, axis=1)


def _with_ones(v):
    return jnp.concatenate([v, jnp.ones_like(v)], axis=1)


def _softmax_start(rows):
    return jnp.full((rows, LANES), NEG_INF, F32), jnp.zeros((rows, 2 * LANES), F32)


def _softmax_step(state, s, v1):
    m_old, acc = state
    tk = s.shape[1]
    m_new = jnp.maximum(m_old, jnp.max(s, axis=-1, keepdims=True))
    alpha = jnp.exp2(m_old - m_new)
    m_full = _lanes(m_new, tk // LANES) if tk >= LANES else m_new[:, :tk]
    p = jnp.exp2(s - m_full)
    return m_new, _lanes(alpha, 2) * acc + _dot(p.astype(BF16), v1)


def _softmax_out(state):
    acc = state[1]
    return acc[:, :LANES] / acc[:, LANES:]


def _visible(kind, q0, k0, tq, tk):
    qp = q0 + lax.broadcasted_iota(jnp.int32, (tq, tk), 0)
    kp = k0 + lax.broadcasted_iota(jnp.int32, (tq, tk), 1)
    if kind == "chunk":
        return (kp // CHUNK) <= (qp // CHUNK)
    return kp <= qp


def _split_maps(q, n_maps):
    if n_maps == 1:
        return [q]
    low = _lane_iota(q.shape) < C_QK
    zero = jnp.zeros_like(q)
    return [jnp.where(low, q, zero), jnp.where(low, zero, q)]


def _diff_lambda(lq1_ref, lk1_ref, lq2_ref, lk2_ref, lam_init):
    s1 = jnp.sum(lq1_ref[...] * lk1_ref[...], axis=-1, keepdims=True)
    s2 = jnp.sum(lq2_ref[...] * lk2_ref[...], axis=-1, keepdims=True)
    return jnp.exp(s1) - jnp.exp(s2) + lam_init


def _attn_finish(states, extra_refs, lam_init):
    if len(states) == 1:
        return _softmax_out(states[0])
    lq1_ref, lk1_ref, lq2_ref, lk2_ref, gout_ref = extra_refs
    lam = _diff_lambda(lq1_ref, lk1_ref, lq2_ref, lk2_ref, lam_init)
    o = _softmax_out(states[0]) - lam * _softmax_out(states[1])
    ms = jnp.mean(o * o, axis=-1, keepdims=True)
    return o * lax.rsqrt(ms + EPS) * gout_ref[...] * (1.0 - lam_init)


def _attn_prompt_kernel(*refs, n_maps, mask, tq, tk, sub, lam_init):
    q_ref, k_ref, v_ref = refs[:3]
    extra_refs = refs[3:-3]
    o_ref, m_scr, acc_scr = refs[-3:]
    qi = pl.program_id(2)
    n_sub = tq // sub
    m0, acc0 = _softmax_start(sub)
    for i in range(n_maps):
        for r in range(n_sub):
            m_scr[i, r] = m0
            acc_scr[i, r] = acc0

    def blocks(key_blocks):
        todo = []
        for k0, nk, diag in key_blocks:
            k = k_ref[pl.ds(k0, nk), :]
            v1 = _with_ones(v_ref[pl.ds(k0, nk), :])
            for r in range(n_sub):
                vis = None
                if diag is not None:
                    if diag > r * sub + sub - 1:
                        continue
                    if diag + nk - 1 > r * sub:
                        vis = _visible(mask, r * sub, diag, sub, nk)
                scores = [_dot_nt(qm, k) for qm in _split_maps(q_ref[r * sub:(r + 1) * sub, :], n_maps)]
                todo.append((r, vis, scores, v1))
        for r, vis, scores, v1 in todo:
            for i in range(n_maps):
                s = scores[i] if vis is None else jnp.where(vis, scores[i], NEG_INF)
                m_scr[i, r], acc_scr[i, r] = _softmax_step((m_scr[i, r], acc_scr[i, r]), s, v1)

    def full_block(kb, carry):
        blocks([(pl.multiple_of(kb * tq, tq), tq, None)])
        return carry

    lax.fori_loop(0, qi, full_block, 0)
    blocks([(pl.multiple_of(qi * tq + d * tk, tk), tk, d * tk) for d in range(tq // tk)])
    for r in range(n_sub):
        states = [(m_scr[i, r], acc_scr[i, r]) for i in range(n_maps)]
        o_ref[r * sub:(r + 1) * sub, :] = _attn_finish(states, extra_refs, lam_init).astype(o_ref.dtype)


def _attn_prompt(q, k, v, extras, *, batch, heads, t, dqk, n_maps, mask, lam_init=0.0):
    tq, tk, sub = ATTN_TQ, ATTN_TK, ATTN_SUB
    nq = t // tq
    dv = 128
    extra_specs = [pl.BlockSpec(e.shape, lambda b, h, i: (0, 0)) for e in extras]
    return pl.pallas_call(
        functools.partial(_attn_prompt_kernel, n_maps=n_maps, mask=mask, tq=tq, tk=tk, sub=sub,
                          lam_init=lam_init),
        grid=(batch, heads, nq),
        in_specs=[pl.BlockSpec((tq, dqk), lambda b, h, i: (b * nq + i, h)),
                  pl.BlockSpec((t, dqk), lambda b, h, i: (b, h)),
                  pl.BlockSpec((t, dv), lambda b, h, i: (b, h))] + extra_specs,
        out_specs=pl.BlockSpec((tq, dv), lambda b, h, i: (b * nq + i, h)),
        out_shape=jax.ShapeDtypeStruct((batch * t, heads * dv), BF16),
        scratch_shapes=[pltpu.VMEM((n_maps, tq // sub, sub, LANES), F32),
                        pltpu.VMEM((n_maps, tq // sub, sub, 2 * LANES), F32)],
        compiler_params=_params(("arbitrary", "arbitrary", "arbitrary")),
        name="attn_prompt_" + mask + str(n_maps),
    )(q, k, v, *extras)


def _mla_sample_kernel(q_ref, kf_ref, va_ref, ckv_ref, kr_ref, wukv_ref, gkn_ref, o_ref, *, past, t):
    ckv_p = ckv_ref[0, 0]
    kr_p = kr_ref[0, 0]
    vis = _visible("chunk", past, past, t, t)
    heads = range(A_HEADS)
    kvs = [_dot(ckv_p, wukv_ref[:, h * 256:(h + 1) * 256]) for h in heads]
    kns, vps = [], []
    for h in heads:
        kk = kvs[h][:, :A_NOPE]
        ms = jnp.mean(kk * kk, axis=-1, keepdims=True)
        kns.append((kk * lax.rsqrt(ms + EPS) * gkn_ref[...]).astype(BF16))
        vps.append(_with_ones(kvs[h][:, A_NOPE:].astype(BF16)))
    s_past, s_new = [], []
    for h in heads:
        q = q_ref[:, h * 256:(h + 1) * 256]
        r0 = 128 + (h % 2) * A_ROPE
        s_past.append(_dot_nt(q[:, :128], kns[h]) + _dot_nt(q[:, r0:r0 + A_ROPE], kr_p))
        s_new.append(jnp.where(vis, _dot_nt(q, kf_ref[:, h * 256:(h + 1) * 256]), NEG_INF))
    for h in heads:
        state = _softmax_step(_softmax_start(t), s_past[h], vps[h])
        state = _softmax_step(state, s_new[h], _with_ones(va_ref[:, h * 128:(h + 1) * 128]))
        o_ref[:, h * 128:(h + 1) * 128] = _softmax_out(state).astype(o_ref.dtype)


def _mla_sample(q, kf, va, cache_ckv, cache_kr, w_ukv, gkn, *, layer, t):
    _, s, past, rank = cache_ckv.shape
    return pl.pallas_call(
        functools.partial(_mla_sample_kernel, past=past, t=t),
        grid=(s,),
        in_specs=[pl.BlockSpec((t, 2048), lambda i: (i, 0)),
                  pl.BlockSpec((t, 2048), lambda i: (i, 0)),
                  pl.BlockSpec((t, 1024), lambda i: (i, 0)),
                  pl.BlockSpec((1, 1, past, rank), lambda i: (layer, i, 0, 0)),
                  pl.BlockSpec((1, 1, past, A_ROPE), lambda i: (layer, i, 0, 0)),
                  pl.BlockSpec(w_ukv.shape, lambda i: (0, 0)),
                  pl.BlockSpec(gkn.shape, lambda i: (0, 0))],
        out_specs=pl.BlockSpec((t, 1024), lambda i: (i, 0)),
        out_shape=jax.ShapeDtypeStruct((s * t, 1024), BF16),
        compiler_params=_params(("arbitrary",)),
        name="mla_sample",
    )(q, kf, va, cache_ckv, cache_kr, w_ukv, gkn)


def _bc_sample_kernel(*refs, n_maps, mask, heads, past, t, lam_init, with_aug):
    q_ref, kn_ref, vn_ref, kp_ref, vp_ref = refs[:5]
    rest = refs[5:-1]
    o_ref = refs[-1]
    if with_aug:
        kaug_ref, extra_refs = rest[0], rest[1:]
    else:
        extra_refs = rest
    dq = q_ref.shape[1] // heads
    vis = _visible(mask, past, past, t, t)
    scores, vps = [], []
    for h in range(heads):
        q = q_ref[:, h * dq:(h + 1) * dq]
        kp = kp_ref[0, 0, pl.ds(h, past, stride=heads), :].astype(BF16)
        vps.append(_with_ones(vp_ref[0, 0, pl.ds(h, past, stride=heads), :].astype(BF16)))
        kn = kn_ref[:, h * dq:(h + 1) * dq]
        if with_aug:
            s_past = _dot_nt(q[:, :128], kp) + _dot_nt(q[:, 128:], kaug_ref[0, :, h * 128:(h + 1) * 128])
            scores.append([(s_past, jnp.where(vis, _dot_nt(q, kn), NEG_INF))])
        else:
            scores.append([(_dot_nt(qm, kp), jnp.where(vis, _dot_nt(qm, kn), NEG_INF))
                           for qm in _split_maps(q, n_maps)])
    for h in range(heads):
        vn = _with_ones(vn_ref[:, h * 128:(h + 1) * 128])
        states = [_softmax_step(_softmax_step(_softmax_start(t), s_past, vps[h]), s_new, vn)
                  for s_past, s_new in scores[h]]
        o_ref[:, h * 128:(h + 1) * 128] = _attn_finish(states, extra_refs, lam_init).astype(o_ref.dtype)


def _bc_sample(q, kn, vn, cache_k, cache_v, kaug, extras, *, layer, heads, n_maps, mask, t, lam_init=0.0):
    _, s, rows, _ = cache_k.shape
    past = rows // heads
    dq = q.shape[1]
    cache_spec = pl.BlockSpec((1, 1, rows, 128), lambda i: (layer, i, 0, 0))
    in_specs = [pl.BlockSpec((t, dq), lambda i: (i, 0)),
                pl.BlockSpec((t, dq), lambda i: (i, 0)),
                pl.BlockSpec((t, heads * 128), lambda i: (i, 0)),
                cache_spec, cache_spec]
    args = [q, kn, vn, cache_k, cache_v]
    if kaug is not None:
        in_specs.append(pl.BlockSpec((1, past, heads * 128), lambda i: (i, 0, 0)))
        args.append(kaug)
    in_specs += [pl.BlockSpec(e.shape, lambda i: (0, 0)) for e in extras]
    return pl.pallas_call(
        functools.partial(_bc_sample_kernel, n_maps=n_maps, mask=mask, heads=heads, past=past, t=t,
                          lam_init=lam_init, with_aug=kaug is not None),
        grid=(s,),
        in_specs=in_specs,
        out_specs=pl.BlockSpec((t, heads * 128), lambda i: (i, 0)),
        out_shape=jax.ShapeDtypeStruct((s * t, heads * 128), BF16),
        compiler_params=_params(("arbitrary",)),
        name="sample_" + mask + str(n_maps),
    )(*args, *extras)


def _mm_res_kernel(*refs, n_in):
    res_ref = refs[2 * n_in]
    o_ref = refs[2 * n_in + 1]
    acc = res_ref[...]
    for i in range(n_in):
        acc = acc + _dot(refs[i][...], refs[n_in + i][...])
    o_ref[...] = acc


def _mm_res(res, a_list, w_list, tm, tn):
    m, n = res.shape
    n_in = len(a_list)
    in_specs = ([pl.BlockSpec((tm, a.shape[1]), lambda j, i: (i, 0)) for a in a_list]
                + [pl.BlockSpec((w.shape[0], tn), lambda j, i: (0, j)) for w in w_list]
                + [pl.BlockSpec((tm, tn), lambda j, i: (i, j))])
    return pl.pallas_call(
        functools.partial(_mm_res_kernel, n_in=n_in),
        grid=(n // tn, m // tm),
        in_specs=in_specs,
        out_specs=pl.BlockSpec((tm, tn), lambda j, i: (i, j)),
        out_shape=jax.ShapeDtypeStruct((m, n), F32),
        compiler_params=_params(("arbitrary", "arbitrary")),
        name="mm_res%d" % n_in,
    )(*a_list, *w_list, res)


def _ffn_up_kernel(x_ref, g_ref, wg_ref, wv_ref, cwg_ref, cwv_ref, cbg_ref, cbv_ref, sg_ref, sv_ref,
                   a_ref, ng_ref, nv_ref, h_scr, rawg0, rawv0, rawg1, rawv1, carg, carv,
                   *, rows, tiles_per_stream, nj):
    i = pl.program_id(0)
    j = pl.program_id(1)
    tm = x_ref.shape[0]
    n_sub = tm // rows
    tn = a_ref.shape[1]
    chunks = [(c, c + FFN_CHUNK) for c in range(0, tn, FFN_CHUNK)]
    raws = ((rawg0, rawv0), (rawg1, rawv1))

    def matmuls():
        h = h_scr[...]
        return [(_dot(h, wg_ref[:, c0:c1]), _dot(h, wv_ref[:, c0:c1])) for c0, c1 in chunks]

    def keep(dots, slot):
        for (c0, c1), us in zip(chunks, dots):
            for u, raw in zip(us, raws[slot]):
                for s in range(n_sub):
                    raw[s, HALO_ROW:HALO_ROW + rows, c0:c1] = u[s * rows:(s + 1) * rows]

    def epilogue(slot):
        jm = j - 1
        sides = ((raws[slot][0], carg, sg_ref, cwg_ref, cbg_ref, ng_ref),
                 (raws[slot][1], carv, sv_ref, cwv_ref, cbv_ref, nv_ref))
        for raw, car, s_ref, _, _, n_ref in sides:
            for s in range(n_sub):
                if tiles_per_stream == 1:
                    halo = s_ref[0, s]
                else:
                    halo = jnp.where((i % tiles_per_stream) == 0, s_ref[0, 0], car[jm])
                raw[s, HALO_ROW - 2:HALO_ROW, :] = halo
                n_ref[s] = raw[s, HALO_ROW + rows - 2:HALO_ROW + rows, :]
            if tiles_per_stream > 1:
                car[jm] = raw[n_sub - 1, HALO_ROW + rows - 2:HALO_ROW + rows, :]

        def conv(side, s, r0, c0, c1):
            raw, _, _, cw_ref, cb_ref, _ = side
            base = HALO_ROW + r0
            return (cb_ref[:, c0:c1]
                    + raw[s, base - 2:base - 2 + FFN_ROWS, c0:c1] * cw_ref[0:1, c0:c1]
                    + raw[s, base - 1:base - 1 + FFN_ROWS, c0:c1] * cw_ref[1:2, c0:c1]
                    + raw[s, base:base + FFN_ROWS, c0:c1] * cw_ref[2:3, c0:c1])

        for c0, c1 in chunks:
            for s in range(n_sub):
                for r0 in range(0, rows, FFN_ROWS):
                    gate = conv(sides[0], s, r0, c0, c1)
                    val = conv(sides[1], s, r0, c0, c1)
                    a_ref[s * rows + r0:s * rows + r0 + FFN_ROWS, c0:c1] = (
                        gate * (1.0 / (1.0 + jnp.exp(-gate))) * val).astype(BF16)

    @pl.when(j == 0)
    def _():
        x = x_ref[...]
        ms = jnp.mean(x * x, axis=-1, keepdims=True)
        h_scr[...] = (x * lax.rsqrt(ms + EPS) * g_ref[...]).astype(BF16)
        keep(matmuls(), 0)

    for slot in range(2):
        @pl.when((j > 0) & (j < nj) & (j % 2 == slot))
        def _():
            epilogue(1 - slot)
            keep(matmuls(), slot)

    @pl.when(j == nj)
    def _():
        epilogue((nj - 1) % 2)


def _ffn_up(x, g, w_up, conv_w, conv_b, state, *, layer, stream_len, tm, tn):
    m, d = x.shape
    nj = D_FF // tn
    if stream_len >= tm:
        rows, tiles_per_stream, n_sub = tm, stream_len // tm, 1
        stream_of = lambda i: i // tiles_per_stream
    else:
        rows, tiles_per_stream, n_sub = stream_len, 1, tm // stream_len
        stream_of = lambda i: i
    mm = lambda j: jnp.minimum(j, nj - 1)
    ep = lambda j: jnp.maximum(j - 1, 0)
    raw = pltpu.VMEM((n_sub, HALO_ROW + rows, tn), F32)
    outs = pl.pallas_call(
        functools.partial(_ffn_up_kernel, rows=rows, tiles_per_stream=tiles_per_stream, nj=nj),
        grid=(m // tm, nj + 1),
        in_specs=[pl.BlockSpec((tm, d), lambda i, j: (i, 0)),
                  pl.BlockSpec((1, d), lambda i, j: (0, 0)),
                  pl.BlockSpec((d, tn), lambda i, j: (0, mm(j))),
                  pl.BlockSpec((d, tn), lambda i, j: (0, nj + mm(j))),
                  pl.BlockSpec((CONV_W, tn), lambda i, j: (0, ep(j))),
                  pl.BlockSpec((CONV_W, tn), lambda i, j: (0, nj + ep(j))),
                  pl.BlockSpec((1, tn), lambda i, j: (0, ep(j))),
                  pl.BlockSpec((1, tn), lambda i, j: (0, nj + ep(j))),
                  pl.BlockSpec((1, n_sub, 2, tn), lambda i, j: (layer, stream_of(i), 0, ep(j))),
                  pl.BlockSpec((1, n_sub, 2, tn), lambda i, j: (layer, stream_of(i), 0, nj + ep(j)))],
        out_specs=[pl.BlockSpec((tm, tn), lambda i, j: (i, ep(j))),
                   pl.BlockSpec((n_sub, 2, tn), lambda i, j: (i, 0, ep(j))),
                   pl.BlockSpec((n_sub, 2, tn), lambda i, j: (i, 0, ep(j)))],
        out_shape=[jax.ShapeDtypeStruct((m, D_FF), BF16),
                   jax.ShapeDtypeStruct((m // rows, 2, D_FF), F32),
                   jax.ShapeDtypeStruct((m // rows, 2, D_FF), F32)],
        scratch_shapes=[pltpu.VMEM((tm, d), BF16), raw, raw, raw, raw,
                        pltpu.VMEM((nj, 2, tn), F32), pltpu.VMEM((nj, 2, tn), F32)],
        compiler_params=_params(("arbitrary", "arbitrary")),
        name="ffn_up",
    )(x, g, w_up, w_up, conv_w, conv_w, conv_b, conv_b, state, state)
    a, tail_g, tail_v = outs
    last = slice(tiles_per_stream - 1, None, tiles_per_stream)
    return a, tail_g[last], tail_v[last]


def _rope_tables(pos, reps):
    posf = pos.astype(F32)[:, None]

    def tab(n_rot):
        half = n_rot // 2
        inv = jnp.power(jnp.float32(ROPE_THETA), -jnp.arange(half, dtype=F32) * (2.0 / n_rot))
        ang = posf * inv[None, :]
        rest = 64 - n_rot
        cos = jnp.concatenate([jnp.cos(ang), jnp.cos(ang), jnp.ones((pos.shape[0], rest), F32)], axis=1)
        sin = jnp.concatenate([-jnp.sin(ang), jnp.sin(ang), jnp.zeros((pos.shape[0], rest), F32)], axis=1)
        return jnp.tile(cos, (reps, 2)), jnp.tile(sin, (reps, 2))

    cosa, sina = tab(A_ROPE)
    cosc, sinc = tab(C_ROT)
    return cosa, sina, cosc, sinc


def _block_diag(width, seg):
    r = jnp.arange(width)[:, None] // seg
    c = jnp.arange(width)[None, :] // seg
    return (r == c).astype(BF16)


def _layer_weights(l, attn_norm, w_in, b_forget, a_kv_norm, a_w_ukv, a_qn_nope, a_qn_rope, a_kn_nope,
                   a_kn_rope, b_qn, b_kn, c_qn, c_kn, c_lq1, c_lk1, c_lq2, c_lk2, c_out_norm, w_out,
                   ffn_norm, w_up, conv_w, conv_b, w_down):
    w_in_p = _permute_w_in(w_in.astype(BF16), l, 256)
    ukv = a_w_ukv[l].reshape(A_KV_RANK, A_HEADS, A_NOPE + A_V)
    wk = ukv[:, :, :A_NOPE].reshape(A_KV_RANK, 1024).astype(BF16)
    wv = ukv[:, :, A_NOPE:].reshape(A_KV_RANK, 1024).astype(BF16)
    row = lambda v: v.reshape(1, -1).astype(F32)
    tile = lambda v, n: jnp.tile(v.astype(F32), n).reshape(1, -1)
    bd_kr = (jnp.arange(128)[:, None] < 64).astype(BF16) * jnp.ones((1, 128), BF16)
    post_consts = [
        tile(a_qn_nope[l], 8), tile(a_qn_rope[l], 8), row(a_kv_norm[l]),
        jnp.concatenate([a_kn_rope[l], jnp.zeros((64,), F32)]).reshape(1, 128),
        tile(a_kn_nope[l], 8), tile(b_qn[l], 4), tile(b_kn[l], 4), tile(c_qn[l], 8), tile(c_kn[l], 8),
        jnp.concatenate([b_forget[l], jnp.zeros((128 - B_HEADS,), F32)]).reshape(1, 128),
        _block_diag(512, 64), _block_diag(512, 128), jnp.ones((512, 512), BF16), bd_kr, wk, wv]
    wo = w_out[l].astype(BF16)
    return dict(
        attn_norm=row(attn_norm[l]), w_in=w_in_p, post_consts=post_consts,
        w_ukv=a_w_ukv[l].astype(BF16), gkn=row(a_kn_nope[l]),
        diff_extras=[row(c_lq1[l]), row(c_lk1[l]), row(c_lq2[l]), row(c_lk2[l]), row(c_out_norm[l])],
        wo_a=wo[:1024], wo_b=wo[1024:1536], wo_c=wo[1536:],
        ffn_norm=row(ffn_norm[l]), w_up=w_up[l].astype(BF16), conv_w=conv_w[l].astype(F32),
        conv_b=row(conv_b[l]), w_down=w_down[l].astype(BF16))


def _run_layer(x, lw, tabs, past, *, n_streams, t, lam_init, post_tm, ffn_tm):
    m = x.shape[0]
    z = _norm_mm(x, lw["attn_norm"], lw["w_in"], DENSE_TM, Z_TN)
    (aq, kf, va, ckv, kr, bq, bk32, bk16, bv32, bv16, lf, cq, ck32, ck16, cv32, cv16) = _post(
        z, tabs, lw["post_consts"], post_tm)

    if past is None:
        qa, ka = _fox_prep(lf.reshape(n_streams, t, B_HEADS), bq, bk16, t, 512, ((0, 512),))
        oa = _attn_prompt(aq, kf, va, [], batch=n_streams, heads=A_HEADS, t=t, dqk=256, n_maps=1,
                          mask="chunk")
        ob = _attn_prompt(qa, ka, bv16, [], batch=n_streams, heads=B_HEADS, t=t, dqk=256, n_maps=1,
                          mask="causal")
        oc = _attn_prompt(cq, ck16, cv16, lw["diff_extras"], batch=n_streams, heads=C_HEADS, t=t, dqk=128,
                          n_maps=2, mask="chunk", lam_init=lam_init)
        conv_state, conv_layer = jnp.zeros((1, n_streams, CONV_W - 1, 2 * D_FF), F32), 0
    else:
        layer, c_ckv, c_kr, c_bk, c_bv, c_lf, c_ck, c_cv, conv_state = past
        p_len = c_ckv.shape[2]
        conv_state, conv_layer = conv_state[layer][None], 0
        lf_all = jnp.concatenate([c_lf[layer], lf.reshape(n_streams, t, B_HEADS)], axis=1)
        blocks = tuple((r, 512) for r in range(0, p_len, 512)) + ((p_len, t),)
        qa, ka, kaug = _fox_prep(lf_all, bq, bk16, t, p_len + t, blocks)
        oa = _mla_sample(aq, kf, va, c_ckv, c_kr, lw["w_ukv"], lw["gkn"], layer=layer, t=t)
        ob = _bc_sample(qa, ka, bv16, c_bk, c_bv, kaug, [], layer=layer, heads=B_HEADS,
                        n_maps=1, mask="causal", t=t)
        oc = _bc_sample(cq, ck16, cv16, c_ck, c_cv, None, lw["diff_extras"], layer=layer,
                        heads=C_HEADS, n_maps=2, mask="chunk", t=t, lam_init=lam_init)

    x1 = _mm_res(x, [oa, ob, oc], [lw["wo_a"], lw["wo_b"], lw["wo_c"]], 512, 1024)
    a, conv_g, conv_v = _ffn_up(x1, lw["ffn_norm"], lw["w_up"], lw["conv_w"], lw["conv_b"], conv_state,
                                layer=conv_layer, stream_len=t, tm=ffn_tm, tn=512)
    x2 = _mm_res(x1, [a], [lw["w_down"]], 512, 1024)
    states = (ckv.reshape(n_streams, t, A_KV_RANK), kr.reshape(n_streams, t, A_ROPE),
              bk32.reshape(n_streams, t, B_HEADS, B_DIM), bv32.reshape(n_streams, t, B_HEADS, B_DIM),
              lf.reshape(n_streams, t, B_HEADS),
              ck32.reshape(n_streams, t, C_HEADS, 2 * C_QK), cv32.reshape(n_streams, t, C_HEADS, C_V),
              jnp.concatenate([conv_g, conv_v], axis=-1))
    return x2, states


def kernel(x_prompt, x_sample, cache_a_ckv, cache_a_krope, cache_b_k, cache_b_v, cache_b_logf, cache_c_k, cache_c_v, state_ffn_conv, attn_norm, w_in, b_forget, a_kv_norm, a_w_ukv, a_qn_nope, a_qn_rope, a_kn_nope, a_kn_rope, b_qn, b_kn, c_qn, c_kn, c_lq1, c_lk1, c_lq2, c_lk2, c_out_norm, w_out, ffn_norm, w_up, conv_w, conv_b, w_down):
    bp, tp, d = x_prompt.shape
    bs, ts, _ = x_sample.shape
    depth = w_in.shape[0]
    past_len = cache_a_ckv.shape[2]
    post_tm = 256
    tabs_p = _rope_tables(jnp.arange(tp, dtype=jnp.int32), 1)
    tabs_s = _rope_tables(past_len + jnp.arange(ts, dtype=jnp.int32), post_tm // ts)

    merge_heads = lambda c: c.reshape(c.shape[:2] + (c.shape[2] * c.shape[3], c.shape[4]))
    caches = (cache_a_ckv.astype(BF16), cache_a_krope.astype(BF16), merge_heads(cache_b_k),
              merge_heads(cache_b_v), cache_b_logf, merge_heads(cache_c_k), merge_heads(cache_c_v),
              state_ffn_conv)
    yp = x_prompt.reshape(bp * tp, d)
    ys = x_sample.reshape(bs * ts, d)
    states_p, states_s = [], []
    for l in range(depth):
        lw = _layer_weights(l, attn_norm, w_in, b_forget, a_kv_norm, a_w_ukv, a_qn_nope, a_qn_rope,
                            a_kn_nope, a_kn_rope, b_qn, b_kn, c_qn, c_kn, c_lq1, c_lk1, c_lq2, c_lk2,
                            c_out_norm, w_out, ffn_norm, w_up, conv_w, conv_b, w_down)
        lam_init = 0.8 - 0.6 * math.exp(-0.3 * l)
        yp, st_p = _run_layer(yp, lw, tabs_p, None, n_streams=bp, t=tp, lam_init=lam_init,
                              post_tm=post_tm, ffn_tm=DENSE_TM)
        ys, st_s = _run_layer(ys, lw, tabs_s, (l,) + caches, n_streams=bs, t=ts, lam_init=lam_init,
                              post_tm=post_tm, ffn_tm=DENSE_TM)
        states_p.append(st_p)
        states_s.append(st_s)
    outs_p = [jnp.stack(s) for s in zip(*states_p)]
    outs_s = [jnp.stack(s) for s in zip(*states_s)]
    return (yp.reshape(bp, tp, d), ys.reshape(bs, ts, d), *outs_p, *outs_s)
```

```python
import functools
import math

import jax
import jax.numpy as jnp
from jax import lax
from jax.experimental import pallas as pl
from jax.experimental.pallas import tpu as pltpu

F32 = jnp.float32
BF16 = jnp.bfloat16

D_MODEL = 2048
CHUNK = 64
ROPE_THETA = 500000.0
EPS = 1e-6
NEG_INF = -1e30
LOG2E = 1.4426950408889634

A_HEADS = 8
A_NOPE = 128
A_ROPE = 64
A_V = 128
A_QK = A_NOPE + A_ROPE
A_KV_RANK = 512
B_HEADS = 4
B_DIM = 128
C_HEADS = 4
C_QK = 64
C_V = 128
C_ROT = 16
D_FF = 5632
CONV_W = 3

LANES = 128
HALO_ROW = 8
VMEM_LIMIT = 56 * 1024 * 1024

Z_QN, Z_QR, Z_CKV, Z_BQ, Z_BK, Z_BV, Z_CQ, Z_CK, Z_CV, Z_KR, Z_BF, Z_COLS = (
    0, 1024, 1536, 2048, 2560, 3072, 3584, 4096, 4608, 5120, 5248, 5376)
Z_TN = 896
DENSE_TM = 1024
FFN_CHUNK = 256
FFN_ROWS = 32
ATTN_TQ = 1024
ATTN_TK = 512
ATTN_SUB = 256


def _params(sem):
    return pltpu.CompilerParams(dimension_semantics=sem, vmem_limit_bytes=VMEM_LIMIT)


def _dot(a, b):
    return jnp.dot(a, b, preferred_element_type=F32)


def _dot_nt(a, b):
    return lax.dot_general(a, b, (((1,), (1,)), ((), ())), preferred_element_type=F32)


def _lane_iota(shape):
    return lax.broadcasted_iota(jnp.int32, shape, len(shape) - 1)


def _permute_w_in_kernel(w_ref, o_ref):
    def put(dst, src, n):
        o_ref[:, dst:dst + n] = w_ref[0, :, src:src + n]

    def clear(dst, n):
        o_ref[:, dst:dst + n] = jnp.zeros((o_ref.shape[0], n), BF16)

    for h in range(A_HEADS):
        put(Z_QN + h * A_NOPE, h * A_QK, A_NOPE)
        put(Z_QR + h * A_ROPE, h * A_QK + A_NOPE, A_ROPE)
    ckv0 = A_HEADS * A_QK
    kr0 = ckv0 + A_KV_RANK
    bq0 = kr0 + A_ROPE
    bf0 = bq0 + 3 * B_HEADS * B_DIM
    cq0 = bf0 + B_HEADS
    put(Z_CKV, ckv0, A_KV_RANK)
    put(Z_BQ, bq0, 3 * B_HEADS * B_DIM)
    put(Z_CQ, cq0, 3 * C_HEADS * C_V)
    put(Z_KR, kr0, A_ROPE)
    clear(Z_KR + A_ROPE, LANES - A_ROPE)
    put(Z_BF, bf0, B_HEADS)
    clear(Z_BF + B_HEADS, Z_COLS - Z_BF - B_HEADS)


def _permute_w_in(w_in, layer, tr):
    _, d, cols = w_in.shape
    return pl.pallas_call(
        _permute_w_in_kernel,
        grid=(d // tr,),
        in_specs=[pl.BlockSpec((1, tr, cols), lambda i: (layer, i, 0))],
        out_specs=pl.BlockSpec((tr, Z_COLS), lambda i: (i, 0)),
        out_shape=jax.ShapeDtypeStruct((d, Z_COLS), BF16),
        compiler_params=_params(("arbitrary",)),
        name="permute_w_in",
    )(w_in)


def _norm_mm_kernel(x_ref, g_ref, w_ref, o_ref, h_scr):
    @pl.when(pl.program_id(1) == 0)
    def _():
        x = x_ref[...]
        ms = jnp.mean(x * x, axis=-1, keepdims=True)
        h_scr[...] = (x * lax.rsqrt(ms + EPS) * g_ref[...]).astype(BF16)

    o_ref[...] = _dot(h_scr[...], w_ref[...])


def _norm_mm(x, g, w, tm, tn):
    m, d = x.shape
    n = w.shape[1]
    return pl.pallas_call(
        _norm_mm_kernel,
        grid=(m // tm, n // tn),
        in_specs=[pl.BlockSpec((tm, d), lambda i, j: (i, 0)),
                  pl.BlockSpec((1, d), lambda i, j: (0, 0)),
                  pl.BlockSpec((d, tn), lambda i, j: (0, j))],
        out_specs=pl.BlockSpec((tm, tn), lambda i, j: (i, j)),
        out_shape=jax.ShapeDtypeStruct((m, n), F32),
        scratch_shapes=[pltpu.VMEM((tm, d), BF16)],
        compiler_params=_params(("arbitrary", "arbitrary")),
        name="norm_mm",
    )(x, g, w)


def _seg_sumsq(x, bd):
    sq = x * x
    hi = sq.astype(BF16)
    lo = (sq - hi.astype(F32)).astype(BF16)
    return _dot(hi, bd) + _dot(lo, bd)


def _seg_norm(x, bd, seg, g):
    return x * lax.rsqrt(_seg_sumsq(x, bd) * (1.0 / seg) + EPS) * g


def _rope(x, cos, sin, half):
    n = x.shape[-1]
    first = (_lane_iota(x.shape) % 64) < half
    partner = jnp.where(first, pltpu.roll(x, n - half, 1), pltpu.roll(x, half, 1))
    return x * cos + partner * sin


def _store_heads(o_ref, x):
    rows = x.shape[0]
    heads = x.shape[1] // LANES
    for h in range(heads):
        o_ref[pl.ds(h, rows, stride=heads), :] = x[:, h * LANES:(h + 1) * LANES]


def _post_kernel(z_ref, cosa_ref, sina_ref, cosc_ref, sinc_ref,
                 gqn_ref, gqr_ref, gckv_ref, gkr_ref, gkn_ref, gbq_ref, gbk_ref, gcq_ref, gck_ref,
                 bfg_ref, bd64_ref, bd128_ref, ones_ref, bdkr_ref, wk_ref, wv_ref,
                 aq_ref, kf_ref, va_ref, ckv_ref, kr_ref, bq_ref, bk32_ref, bk16_ref,
                 bv32_ref, bv16_ref, lf_ref, cq_ref, ck32_ref, ck16_ref, cv32_ref, cv16_ref):
    bd64 = bd64_ref[...]
    bd128 = bd128_ref[...]
    tm = z_ref.shape[0]
    lane = _lane_iota((tm, LANES))
    low = lane < 64

    cosa4 = jnp.concatenate([cosa_ref[...]] * 4, axis=1)
    sina4 = jnp.concatenate([sina_ref[...]] * 4, axis=1)
    cosc4 = jnp.concatenate([cosc_ref[...]] * 4, axis=1)
    sinc4 = jnp.concatenate([sinc_ref[...]] * 4, axis=1)

    a_scale = A_QK ** -0.5 * LOG2E
    qr = _seg_norm(z_ref[:, Z_QR:Z_QR + 512], bd64, 64, gqr_ref[...])
    qr = _rope(qr, cosa4, sina4, A_ROPE // 2) * a_scale
    for half in range(2):
        c0 = Z_QN + half * 512
        qn = _seg_norm(z_ref[:, c0:c0 + 512], bd128, 128, gqn_ref[:, half * 512:(half + 1) * 512]) * a_scale
        for hh in range(4):
            h = half * 4 + hh
            aq_ref[:, h * 256:h * 256 + 128] = qn[:, hh * 128:(hh + 1) * 128].astype(BF16)
    for h in range(A_HEADS):
        pair = qr[:, (h // 2) * 128:(h // 2 + 1) * 128]
        keep = low if h % 2 == 0 else jnp.logical_not(low)
        aq_ref[:, h * 256 + 128:h * 256 + 256] = jnp.where(keep, pair, 0.0).astype(BF16)

    ckv = _seg_norm(z_ref[:, Z_CKV:Z_CKV + 512], ones_ref[...], 512, gckv_ref[...])
    ckv_ref[...] = ckv
    ckv16 = ckv.astype(BF16)
    va_ref[...] = _dot(ckv16, wv_ref[...]).astype(BF16)

    kr = _seg_norm(z_ref[:, Z_KR:Z_KR + 128], bdkr_ref[...], 64, gkr_ref[...])
    kr = _rope(kr, cosa_ref[...], sina_ref[...], A_ROPE // 2)
    kr_ref[...] = kr[:, :A_ROPE]
    kr2 = kr + pltpu.roll(kr, 64, 1)
    kr_even = jnp.where(low, kr2, 0.0).astype(BF16)
    kr_odd = jnp.where(low, 0.0, kr2).astype(BF16)
    for half in range(2):
        kk = _dot(ckv16, wk_ref[:, half * 512:(half + 1) * 512])
        kn = _seg_norm(kk, bd128, 128, gkn_ref[:, half * 512:(half + 1) * 512])
        for hh in range(4):
            h = half * 4 + hh
            kf_ref[:, h * 256:h * 256 + 128] = kn[:, hh * 128:(hh + 1) * 128].astype(BF16)
            kf_ref[:, h * 256 + 128:h * 256 + 256] = kr_even if h % 2 == 0 else kr_odd

    bq = _seg_norm(z_ref[:, Z_BQ:Z_BQ + 512], bd128, 128, gbq_ref[...]) * (B_DIM ** -0.5 * LOG2E)
    bq_ref[...] = bq.astype(BF16)
    bk = _seg_norm(z_ref[:, Z_BK:Z_BK + 512], bd128, 128, gbk_ref[...])
    _store_heads(bk32_ref, bk)
    bk16_ref[...] = bk.astype(BF16)
    bv = z_ref[:, Z_BV:Z_BV + 512]
    _store_heads(bv32_ref, bv)
    bv16_ref[...] = bv.astype(BF16)
    f = z_ref[:, Z_BF:Z_BF + 128][:, 0:B_HEADS] + bfg_ref[:, 0:B_HEADS]
    lf_ref[...] = jnp.minimum(f, 0.0) - jnp.log1p(jnp.exp(-jnp.abs(f)))

    cq = _seg_norm(z_ref[:, Z_CQ:Z_CQ + 512], bd64, 64, gcq_ref[...])
    cq_ref[...] = (_rope(cq, cosc4, sinc4, C_ROT // 2) * (C_QK ** -0.5 * LOG2E)).astype(BF16)
    ck = _seg_norm(z_ref[:, Z_CK:Z_CK + 512], bd64, 64, gck_ref[...])
    ck = _rope(ck, cosc4, sinc4, C_ROT // 2)
    _store_heads(ck32_ref, ck)
    ck16_ref[...] = ck.astype(BF16)
    cv = z_ref[:, Z_CV:Z_CV + 512]
    _store_heads(cv32_ref, cv)
    cv16_ref[...] = cv.astype(BF16)


def _post(z, tabs, consts, tm):
    m = z.shape[0]
    tab_rows = tabs[0].shape[0]
    tab_blocks = tab_rows // tm

    def row(w):
        return pl.BlockSpec((tm, w), lambda i: (i, 0))

    def full(a):
        return pl.BlockSpec(a.shape, lambda i: (0, 0))

    tab_spec = pl.BlockSpec((tm, LANES), lambda i: (i % tab_blocks, 0))
    widths = [(2048, BF16, 0), (2048, BF16, 0), (1024, BF16, 0), (512, F32, 0), (64, F32, 0), (512, BF16, 0),
              (512, F32, B_HEADS), (512, BF16, 0), (512, F32, B_HEADS), (512, BF16, 0), (B_HEADS, F32, 0),
              (512, BF16, 0), (512, F32, C_HEADS), (512, BF16, 0), (512, F32, C_HEADS), (512, BF16, 0)]
    out_spec = lambda w, hd: row(w) if hd == 0 else pl.BlockSpec((tm * hd, w // hd), lambda i: (i, 0))
    out_sds = lambda w, dt, hd: jax.ShapeDtypeStruct((m, w) if hd == 0 else (m * hd, w // hd), dt)
    return pl.pallas_call(
        _post_kernel,
        grid=(m // tm,),
        in_specs=[row(Z_COLS)] + [tab_spec] * 4 + [full(c) for c in consts],
        out_specs=[out_spec(w, hd) for w, _, hd in widths],
        out_shape=[out_sds(w, dt, hd) for w, dt, hd in widths],
        compiler_params=_params(("arbitrary",)),
        name="post",
    )(z, *tabs, *consts)


def _split3(c):
    hi = c.astype(BF16).astype(F32)
    r1 = c - hi
    mid = r1.astype(BF16).astype(F32)
    lo = r1 - mid
    return hi, mid, lo


def _cumsum_block(lf, carry):
    n = lf.shape[0]
    r = lax.broadcasted_iota(jnp.int32, (n, n), 0)
    c = lax.broadcasted_iota(jnp.int32, (n, n), 1)
    tri = jnp.where(c <= r, 1.0, 0.0).astype(BF16)
    hi, mid, lo = _split3(lf)
    cum = _dot(tri, hi.astype(BF16)) + _dot(tri, mid.astype(BF16)) + _dot(tri, lo.astype(BF16))
    return cum + carry


def _aug_q(c, lane):
    hi, mid, lo = _split3(c)
    return jnp.where(lane == 0, hi, jnp.where(lane == 1, mid, jnp.where(lane == 2, lo,
                     jnp.where(lane < 6, 1.0, 0.0)))).astype(BF16)


def _aug_k(c, lane):
    hi, mid, lo = _split3(c)
    return jnp.where(lane < 3, 1.0, jnp.where(lane == 3, -hi, jnp.where(lane == 4, -mid,
                     jnp.where(lane == 5, -lo, 0.0)))).astype(BF16)


def _fox_prep_kernel(lf_ref, bq_ref, bk_ref, *out_refs, past, blocks):
    carry_scr = out_refs[-1]
    if past:
        qa_ref, ka_ref, kpast_ref = out_refs[:-1]
    else:
        qa_ref, ka_ref = out_refs[:-1]

    @pl.when(pl.program_id(1) == 0)
    def _():
        carry_scr[...] = jnp.zeros(carry_scr.shape, F32)

    carry = carry_scr[...]
    for r0, n in blocks:
        cum = _cumsum_block(lf_ref[0, r0:r0 + n, :], carry)
        carry = cum[n - 1:n, :]
        lane = _lane_iota((n, LANES))
        for h in range(B_HEADS):
            c = cum[:, h:h + 1] * LOG2E
            if r0 < past:
                kpast_ref[0, r0:r0 + n, h * 128:(h + 1) * 128] = _aug_k(c, lane)
            else:
                t0 = r0 - past
                qa_ref[t0:t0 + n, h * 256:h * 256 + 128] = bq_ref[t0:t0 + n, h * 128:(h + 1) * 128]
                qa_ref[t0:t0 + n, h * 256 + 128:h * 256 + 256] = _aug_q(c, lane)
                ka_ref[t0:t0 + n, h * 256:h * 256 + 128] = bk_ref[t0:t0 + n, h * 128:(h + 1) * 128]
                ka_ref[t0:t0 + n, h * 256 + 128:h * 256 + 256] = _aug_k(c, lane)
    carry_scr[...] = carry


def _fox_prep(lf_all, bq, bk, t_new, rows, blocks):
    s, t_tot, _ = lf_all.shape
    past = t_tot - t_new
    nb = t_tot // rows
    new_rows = rows - past
    m = bq.shape[0]
    new_spec = lambda w: pl.BlockSpec((new_rows, w), lambda i, j: (i * nb + j, 0))
    out_specs = [new_spec(1024), new_spec(1024)]
    out_shape = [jax.ShapeDtypeStruct((m, 1024), BF16), jax.ShapeDtypeStruct((m, 1024), BF16)]
    if past:
        out_specs.append(pl.BlockSpec((1, past, 512), lambda i, j: (i, 0, 0)))
        out_shape.append(jax.ShapeDtypeStruct((s, past, 512), BF16))
    return pl.pallas_call(
        functools.partial(_fox_prep_kernel, past=past, blocks=blocks),
        grid=(s, nb),
        in_specs=[pl.BlockSpec((1, rows, B_HEADS), lambda i, j: (i, j, 0)), new_spec(512), new_spec(512)],
        out_specs=out_specs,
        out_shape=out_shape,
        scratch_shapes=[pltpu.VMEM((1, B_HEADS), F32)],
        compiler_params=_params(("arbitrary", "arbitrary")),
        name="fox_prep",
    )(lf_all, bq, bk)


def _lanes(x, n):
    return x if n == 1 else jnp.concatenate([x] * n, axis=1)


def _with_ones(v):
    return jnp.concatenate([v, jnp.ones_like(v)], axis=1)


def _softmax_start(rows):
    return jnp.full((rows, LANES), NEG_INF, F32), jnp.zeros((rows, 2 * LANES), F32)


def _softmax_step(state, s, v1):
    m_old, acc = state
    tk = s.shape[1]
    m_new = jnp.maximum(m_old, jnp.max(s, axis=-1, keepdims=True))
    alpha = jnp.exp2(m_old - m_new)
    m_full = _lanes(m_new, tk // LANES) if tk >= LANES else m_new[:, :tk]
    p = jnp.exp2(s - m_full)
    return m_new, _lanes(alpha, 2) * acc + _dot(p.astype(BF16), v1)


def _softmax_out(state):
    acc = state[1]
    return acc[:, :LANES] / acc[:, LANES:]


def _visible(kind, q0, k0, tq, tk):
    qp = q0 + lax.broadcasted_iota(jnp.int32, (tq, tk), 0)
    kp = k0 + lax.broadcasted_iota(jnp.int32, (tq, tk), 1)
    if kind == "chunk":
        return (kp // CHUNK) <= (qp // CHUNK)
    return kp <= qp


def _split_maps(q, n_maps):
    if n_maps == 1:
        return [q]
    low = _lane_iota(q.shape) < C_QK
    zero = jnp.zeros_like(q)
    return [jnp.where(low, q, zero), jnp.where(low, zero, q)]


def _diff_lambda(lq1_ref, lk1_ref, lq2_ref, lk2_ref, lam_init):
    s1 = jnp.sum(lq1_ref[...] * lk1_ref[...], axis=-1, keepdims=True)
    s2 = jnp.sum(lq2_ref[...] * lk2_ref[...], axis=-1, keepdims=True)
    return jnp.exp(s1) - jnp.exp(s2) + lam_init


def _attn_finish(states, extra_refs, lam_init):
    if len(states) == 1:
        return _softmax_out(states[0])
    lq1_ref, lk1_ref, lq2_ref, lk2_ref, gout_ref = extra_refs
    lam = _diff_lambda(lq1_ref, lk1_ref, lq2_ref, lk2_ref, lam_init)
    o = _softmax_out(states[0]) - lam * _softmax_out(states[1])
    ms = jnp.mean(o * o, axis=-1, keepdims=True)
    return o * lax.rsqrt(ms + EPS) * gout_ref[...] * (1.0 - lam_init)


def _attn_prompt_kernel(*refs, n_maps, mask, tq, tk, sub, lam_init):
    q_ref, k_ref, v_ref = refs[:3]
    extra_refs = refs[3:-3]
    o_ref, m_scr, acc_scr = refs[-3:]
    qi = pl.program_id(2)
    n_sub = tq // sub
    m0, acc0 = _softmax_start(sub)
    for i in range(n_maps):
        for r in range(n_sub):
            m_scr[i, r] = m0
            acc_scr[i, r] = acc0

    def blocks(key_blocks):
        todo = []
        for k0, nk, diag in key_blocks:
            k = k_ref[pl.ds(k0, nk), :]
            v1 = _with_ones(v_ref[pl.ds(k0, nk), :])
            for r in range(n_sub):
                vis = None
                if diag is not None:
                    if diag > r * sub + sub - 1:
                        continue
                    if diag + nk - 1 > r * sub:
                        vis = _visible(mask, r * sub, diag, sub, nk)
                scores = [_dot_nt(qm, k) for qm in _split_maps(q_ref[r * sub:(r + 1) * sub, :], n_maps)]
                todo.append((r, vis, scores, v1))
        for r, vis, scores, v1 in todo:
            for i in range(n_maps):
                s = scores[i] if vis is None else jnp.where(vis, scores[i], NEG_INF)
                m_scr[i, r], acc_scr[i, r] = _softmax_step((m_scr[i, r], acc_scr[i, r]), s, v1)

    def full_block(kb, carry):
        blocks([(pl.multiple_of(kb * tq, tq), tq, None)])
        return carry

    lax.fori_loop(0, qi, full_block, 0)
    blocks([(pl.multiple_of(qi * tq + d * tk, tk), tk, d * tk) for d in range(tq // tk)])
    for r in range(n_sub):
        states = [(m_scr[i, r], acc_scr[i, r]) for i in range(n_maps)]
        o_ref[r * sub:(r + 1) * sub, :] = _attn_finish(states, extra_refs, lam_init).astype(o_ref.dtype)


def _attn_prompt(q, k, v, extras, *, batch, heads, t, dqk, n_maps, mask, lam_init=0.0):
    tq, tk, sub = ATTN_TQ, ATTN_TK, ATTN_SUB
    nq = t // tq
    dv = 128
    extra_specs = [pl.BlockSpec(e.shape, lambda b, h, i: (0, 0)) for e in extras]
    return pl.pallas_call(
        functools.partial(_attn_prompt_kernel, n_maps=n_maps, mask=mask, tq=tq, tk=tk, sub=sub,
                          lam_init=lam_init),
        grid=(batch, heads, nq),
        in_specs=[pl.BlockSpec((tq, dqk), lambda b, h, i: (b * nq + i, h)),
                  pl.BlockSpec((t, dqk), lambda b, h, i: (b, h)),
                  pl.BlockSpec((t, dv), lambda b, h, i: (b, h))] + extra_specs,
        out_specs=pl.BlockSpec((tq, dv), lambda b, h, i: (b * nq + i, h)),
        out_shape=jax.ShapeDtypeStruct((batch * t, heads * dv), BF16),
        scratch_shapes=[pltpu.VMEM((n_maps, tq // sub, sub, LANES), F32),
                        pltpu.VMEM((n_maps, tq // sub, sub, 2 * LANES), F32)],
        compiler_params=_params(("arbitrary", "arbitrary", "arbitrary")),
        name="attn_prompt_" + mask + str(n_maps),
    )(q, k, v, *extras)


def _mla_sample_kernel(q_ref, kf_ref, va_ref, ckv_ref, kr_ref, wukv_ref, gkn_ref, o_ref, *, past, t):
    ckv_p = ckv_ref[0, 0]
    kr_p = kr_ref[0, 0]
    vis = _visible("chunk", past, past, t, t)
    heads = range(A_HEADS)
    kvs = [_dot(ckv_p, wukv_ref[:, h * 256:(h + 1) * 256]) for h in heads]
    kns, vps = [], []
    for h in heads:
        kk = kvs[h][:, :A_NOPE]
        ms = jnp.mean(kk * kk, axis=-1, keepdims=True)
        kns.append((kk * lax.rsqrt(ms + EPS) * gkn_ref[...]).astype(BF16))
        vps.append(_with_ones(kvs[h][:, A_NOPE:].astype(BF16)))
    s_past, s_new = [], []
    for h in heads:
        q = q_ref[:, h * 256:(h + 1) * 256]
        r0 = 128 + (h % 2) * A_ROPE
        s_past.append(_dot_nt(q[:, :128], kns[h]) + _dot_nt(q[:, r0:r0 + A_ROPE], kr_p))
        s_new.append(jnp.where(vis, _dot_nt(q, kf_ref[:, h * 256:(h + 1) * 256]), NEG_INF))
    for h in heads:
        state = _softmax_step(_softmax_start(t), s_past[h], vps[h])
        state = _softmax_step(state, s_new[h], _with_ones(va_ref[:, h * 128:(h + 1) * 128]))
        o_ref[:, h * 128:(h + 1) * 128] = _softmax_out(state).astype(o_ref.dtype)


def _mla_sample(q, kf, va, cache_ckv, cache_kr, w_ukv, gkn, *, layer, t):
    _, s, past, rank = cache_ckv.shape
    return pl.pallas_call(
        functools.partial(_mla_sample_kernel, past=past, t=t),
        grid=(s,),
        in_specs=[pl.BlockSpec((t, 2048), lambda i: (i, 0)),
                  pl.BlockSpec((t, 2048), lambda i: (i, 0)),
                  pl.BlockSpec((t, 1024), lambda i: (i, 0)),
                  pl.BlockSpec((1, 1, past, rank), lambda i: (layer, i, 0, 0)),
                  pl.BlockSpec((1, 1, past, A_ROPE), lambda i: (layer, i, 0, 0)),
                  pl.BlockSpec(w_ukv.shape, lambda i: (0, 0)),
                  pl.BlockSpec(gkn.shape, lambda i: (0, 0))],
        out_specs=pl.BlockSpec((t, 1024), lambda i: (i, 0)),
        out_shape=jax.ShapeDtypeStruct((s * t, 1024), BF16),
        compiler_params=_params(("arbitrary",)),
        name="mla_sample",
    )(q, kf, va, cache_ckv, cache_kr, w_ukv, gkn)


def _bc_sample_kernel(*refs, n_maps, mask, heads, past, t, lam_init, with_aug):
    q_ref, kn_ref, vn_ref, kp_ref, vp_ref = refs[:5]
    rest = refs[5:-1]
    o_ref = refs[-1]
    if with_aug:
        kaug_ref, extra_refs = rest[0], rest[1:]
    else:
        extra_refs = rest
    dq = q_ref.shape[1] // heads
    vis = _visible(mask, past, past, t, t)
    scores, vps = [], []
    for h in range(heads):
        q = q_ref[:, h * dq:(h + 1) * dq]
        kp = kp_ref[0, 0, pl.ds(h, past, stride=heads), :].astype(BF16)
        vps.append(_with_ones(vp_ref[0, 0, pl.ds(h, past, stride=heads), :].astype(BF16)))
        kn = kn_ref[:, h * dq:(h + 1) * dq]
        if with_aug:
            s_past = _dot_nt(q[:, :128], kp) + _dot_nt(q[:, 128:], kaug_ref[0, :, h * 128:(h + 1) * 128])
            scores.append([(s_past, jnp.where(vis, _dot_nt(q, kn), NEG_INF))])
        else:
            scores.append([(_dot_nt(qm, kp), jnp.where(vis, _dot_nt(qm, kn), NEG_INF))
                           for qm in _split_maps(q, n_maps)])
    for h in range(heads):
        vn = _with_ones(vn_ref[:, h * 128:(h + 1) * 128])
        states = [_softmax_step(_softmax_step(_softmax_start(t), s_past, vps[h]), s_new, vn)
                  for s_past, s_new in scores[h]]
        o_ref[:, h * 128:(h + 1) * 128] = _attn_finish(states, extra_refs, lam_init).astype(o_ref.dtype)


def _bc_sample(q, kn, vn, cache_k, cache_v, kaug, extras, *, layer, heads, n_maps, mask, t, lam_init=0.0):
    _, s, rows, _ = cache_k.shape
    past = rows // heads
    dq = q.shape[1]
    cache_spec = pl.BlockSpec((1, 1, rows, 128), lambda i: (layer, i, 0, 0))
    in_specs = [pl.BlockSpec((t, dq), lambda i: (i, 0)),
                pl.BlockSpec((t, dq), lambda i: (i, 0)),
                pl.BlockSpec((t, heads * 128), lambda i: (i, 0)),
                cache_spec, cache_spec]
    args = [q, kn, vn, cache_k, cache_v]
    if kaug is not None:
        in_specs.append(pl.BlockSpec((1, past, heads * 128), lambda i: (i, 0, 0)))
        args.append(kaug)
    in_specs += [pl.BlockSpec(e.shape, lambda i: (0, 0)) for e in extras]
    return pl.pallas_call(
        functools.partial(_bc_sample_kernel, n_maps=n_maps, mask=mask, heads=heads, past=past, t=t,
                          lam_init=lam_init, with_aug=kaug is not None),
        grid=(s,),
        in_specs=in_specs,
        out_specs=pl.BlockSpec((t, heads * 128), lambda i: (i, 0)),
        out_shape=jax.ShapeDtypeStruct((s * t, heads * 128), BF16),
        compiler_params=_params(("arbitrary",)),
        name="sample_" + mask + str(n_maps),
    )(*args, *extras)


def _mm_res_kernel(*refs, n_in):
    res_ref = refs[2 * n_in]
    o_ref = refs[2 * n_in + 1]
    acc = res_ref[...]
    for i in range(n_in):
        acc = acc + _dot(refs[i][...], refs[n_in + i][...])
    o_ref[...] = acc


def _mm_res(res, a_list, w_list, tm, tn):
    m, n = res.shape
    n_in = len(a_list)
    in_specs = ([pl.BlockSpec((tm, a.shape[1]), lambda j, i: (i, 0)) for a in a_list]
                + [pl.BlockSpec((w.shape[0], tn), lambda j, i: (0, j)) for w in w_list]
                + [pl.BlockSpec((tm, tn), lambda j, i: (i, j))])
    return pl.pallas_call(
        functools.partial(_mm_res_kernel, n_in=n_in),
        grid=(n // tn, m // tm),
        in_specs=in_specs,
        out_specs=pl.BlockSpec((tm, tn), lambda j, i: (i, j)),
        out_shape=jax.ShapeDtypeStruct((m, n), F32),
        compiler_params=_params(("arbitrary", "arbitrary")),
        name="mm_res%d" % n_in,
    )(*a_list, *w_list, res)


def _ffn_up_kernel(x_ref, g_ref, wg_ref, wv_ref, cwg_ref, cwv_ref, cbg_ref, cbv_ref, sg_ref, sv_ref,
                   a_ref, ng_ref, nv_ref, h_scr, rawg, rawv, carg, carv, *, rows, tiles_per_stream):
    i = pl.program_id(0)
    j = pl.program_id(1)
    tm = x_ref.shape[0]
    n_sub = tm // rows
    tn = a_ref.shape[1]
    chunks = [(c, c + FFN_CHUNK) for c in range(0, tn, FFN_CHUNK)]

    @pl.when(j == 0)
    def _():
        x = x_ref[...]
        ms = jnp.mean(x * x, axis=-1, keepdims=True)
        h_scr[...] = (x * lax.rsqrt(ms + EPS) * g_ref[...]).astype(BF16)

    half = tm // 2
    items = [(c0, c1, lo, lo + half) for c0, c1 in chunks for lo in (0, half)]
    dots = [(_dot(h_scr[lo:hi, :], wg_ref[:, c0:c1]), _dot(h_scr[lo:hi, :], wv_ref[:, c0:c1]))
            for c0, c1, lo, hi in items]
    for (c0, c1, lo, hi), us in zip(items, dots):
        sides = ((us[0], rawg, carg, sg_ref, cwg_ref, cbg_ref, ng_ref),
                 (us[1], rawv, carv, sv_ref, cwv_ref, cbv_ref, nv_ref))
        segs = [(s, max(lo, s * rows), min(hi, (s + 1) * rows)) for s in range(n_sub)
                if s * rows < hi and (s + 1) * rows > lo]
        for u, raw, car, s_ref, _, _, n_ref in sides:
            for s, a, b in segs:
                raw[s, HALO_ROW + a - s * rows:HALO_ROW + b - s * rows, c0:c1] = u[a - lo:b - lo]
                if a == s * rows:
                    if tiles_per_stream == 1:
                        halo = s_ref[0, s, :, c0:c1]
                    else:
                        halo = jnp.where((i % tiles_per_stream) == 0, s_ref[0, 0, :, c0:c1], car[j, :, c0:c1])
                    raw[s, HALO_ROW - 2:HALO_ROW, c0:c1] = halo
                if b == (s + 1) * rows:
                    n_ref[s, :, c0:c1] = u[b - lo - 2:b - lo]
            if tiles_per_stream > 1 and hi == tm:
                car[j, :, c0:c1] = u[hi - lo - 2:hi - lo]

        def conv(side, s, r0):
            _, raw, _, _, cw_ref, cb_ref, _ = side
            base = HALO_ROW + r0
            return (cb_ref[:, c0:c1]
                    + raw[s, base - 2:base - 2 + FFN_ROWS, c0:c1] * cw_ref[0:1, c0:c1]
                    + raw[s, base - 1:base - 1 + FFN_ROWS, c0:c1] * cw_ref[1:2, c0:c1]
                    + raw[s, base:base + FFN_ROWS, c0:c1] * cw_ref[2:3, c0:c1])

        for s, a, b in segs:
            for r in range(a, b, FFN_ROWS):
                gate = conv(sides[0], s, r - s * rows)
                val = conv(sides[1], s, r - s * rows)
                a_ref[r:r + FFN_ROWS, c0:c1] = (gate * (1.0 / (1.0 + jnp.exp(-gate))) * val).astype(BF16)


def _ffn_up(x, g, w_up, conv_w, conv_b, state, *, layer, stream_len, tm, tn):
    m, d = x.shape
    nj = D_FF // tn
    if stream_len >= tm:
        rows, tiles_per_stream, n_sub = tm, stream_len // tm, 1
        stream_of = lambda i: i // tiles_per_stream
    else:
        rows, tiles_per_stream, n_sub = stream_len, 1, tm // stream_len
        stream_of = lambda i: i
    gcol = lambda i, j: (0, j)
    vcol = lambda i, j: (0, nj + j)
    raw = pltpu.VMEM((n_sub, HALO_ROW + rows, tn), F32)
    outs = pl.pallas_call(
        functools.partial(_ffn_up_kernel, rows=rows, tiles_per_stream=tiles_per_stream),
        grid=(m // tm, nj),
        in_specs=[pl.BlockSpec((tm, d), lambda i, j: (i, 0)),
                  pl.BlockSpec((1, d), lambda i, j: (0, 0)),
                  pl.BlockSpec((d, tn), gcol), pl.BlockSpec((d, tn), vcol),
                  pl.BlockSpec((CONV_W, tn), gcol), pl.BlockSpec((CONV_W, tn), vcol),
                  pl.BlockSpec((1, tn), gcol), pl.BlockSpec((1, tn), vcol),
                  pl.BlockSpec((1, n_sub, 2, tn), lambda i, j: (layer, stream_of(i), 0, j)),
                  pl.BlockSpec((1, n_sub, 2, tn), lambda i, j: (layer, stream_of(i), 0, nj + j))],
        out_specs=[pl.BlockSpec((tm, tn), lambda i, j: (i, j)),
                   pl.BlockSpec((n_sub, 2, tn), lambda i, j: (i, 0, j)),
                   pl.BlockSpec((n_sub, 2, tn), lambda i, j: (i, 0, j))],
        out_shape=[jax.ShapeDtypeStruct((m, D_FF), BF16),
                   jax.ShapeDtypeStruct((m // rows, 2, D_FF), F32),
                   jax.ShapeDtypeStruct((m // rows, 2, D_FF), F32)],
        scratch_shapes=[pltpu.VMEM((tm, d), BF16), raw, raw,
                        pltpu.VMEM((nj, 2, tn), F32), pltpu.VMEM((nj, 2, tn), F32)],
        compiler_params=_params(("arbitrary", "arbitrary")),
        name="ffn_up",
    )(x, g, w_up, w_up, conv_w, conv_w, conv_b, conv_b, state, state)
    a, tail_g, tail_v = outs
    last = slice(tiles_per_stream - 1, None, tiles_per_stream)
    return a, tail_g[last], tail_v[last]


def _rope_tables(pos, reps):
    posf = pos.astype(F32)[:, None]

    def tab(n_rot):
        half = n_rot // 2
        inv = jnp.power(jnp.float32(ROPE_THETA), -jnp.arange(half, dtype=F32) * (2.0 / n_rot))
        ang = posf * inv[None, :]
        rest = 64 - n_rot
        cos = jnp.concatenate([jnp.cos(ang), jnp.cos(ang), jnp.ones((pos.shape[0], rest), F32)], axis=1)
        sin = jnp.concatenate([-jnp.sin(ang), jnp.sin(ang), jnp.zeros((pos.shape[0], rest), F32)], axis=1)
        return jnp.tile(cos, (reps, 2)), jnp.tile(sin, (reps, 2))

    cosa, sina = tab(A_ROPE)
    cosc, sinc = tab(C_ROT)
    return cosa, sina, cosc, sinc


def _block_diag(width, seg):
    r = jnp.arange(width)[:, None] // seg
    c = jnp.arange(width)[None, :] // seg
    return (r == c).astype(BF16)


def _layer_weights(l, attn_norm, w_in, b_forget, a_kv_norm, a_w_ukv, a_qn_nope, a_qn_rope, a_kn_nope,
                   a_kn_rope, b_qn, b_kn, c_qn, c_kn, c_lq1, c_lk1, c_lq2, c_lk2, c_out_norm, w_out,
                   ffn_norm, w_up, conv_w, conv_b, w_down):
    w_in_p = _permute_w_in(w_in.astype(BF16), l, 256)
    ukv = a_w_ukv[l].reshape(A_KV_RANK, A_HEADS, A_NOPE + A_V)
    wk = ukv[:, :, :A_NOPE].reshape(A_KV_RANK, 1024).astype(BF16)
    wv = ukv[:, :, A_NOPE:].reshape(A_KV_RANK, 1024).astype(BF16)
    row = lambda v: v.reshape(1, -1).astype(F32)
    tile = lambda v, n: jnp.tile(v.astype(F32), n).reshape(1, -1)
    bd_kr = (jnp.arange(128)[:, None] < 64).astype(BF16) * jnp.ones((1, 128), BF16)
    post_consts = [
        tile(a_qn_nope[l], 8), tile(a_qn_rope[l], 8), row(a_kv_norm[l]),
        jnp.concatenate([a_kn_rope[l], jnp.zeros((64,), F32)]).reshape(1, 128),
        tile(a_kn_nope[l], 8), tile(b_qn[l], 4), tile(b_kn[l], 4), tile(c_qn[l], 8), tile(c_kn[l], 8),
        jnp.concatenate([b_forget[l], jnp.zeros((128 - B_HEADS,), F32)]).reshape(1, 128),
        _block_diag(512, 64), _block_diag(512, 128), jnp.ones((512, 512), BF16), bd_kr, wk, wv]
    wo = w_out[l].astype(BF16)
    return dict(
        attn_norm=row(attn_norm[l]), w_in=w_in_p, post_consts=post_consts,
        w_ukv=a_w_ukv[l].astype(BF16), gkn=row(a_kn_nope[l]),
        diff_extras=[row(c_lq1[l]), row(c_lk1[l]), row(c_lq2[l]), row(c_lk2[l]), row(c_out_norm[l])],
        wo_a=wo[:1024], wo_b=wo[1024:1536], wo_c=wo[1536:],
        ffn_norm=row(ffn_norm[l]), w_up=w_up[l].astype(BF16), conv_w=conv_w[l].astype(F32),
        conv_b=row(conv_b[l]), w_down=w_down[l].astype(BF16))


def _run_layer(x, lw, tabs, past, *, n_streams, t, lam_init, post_tm, ffn_tm):
    m = x.shape[0]
    z = _norm_mm(x, lw["attn_norm"], lw["w_in"], DENSE_TM, Z_TN)
    (aq, kf, va, ckv, kr, bq, bk32, bk16, bv32, bv16, lf, cq, ck32, ck16, cv32, cv16) = _post(
        z, tabs, lw["post_consts"], post_tm)

    if past is None:
        qa, ka = _fox_prep(lf.reshape(n_streams, t, B_HEADS), bq, bk16, t, 512, ((0, 512),))
        oa = _attn_prompt(aq, kf, va, [], batch=n_streams, heads=A_HEADS, t=t, dqk=256, n_maps=1,
                          mask="chunk")
        ob = _attn_prompt(qa, ka, bv16, [], batch=n_streams, heads=B_HEADS, t=t, dqk=256, n_maps=1,
                          mask="causal")
        oc = _attn_prompt(cq, ck16, cv16, lw["diff_extras"], batch=n_streams, heads=C_HEADS, t=t, dqk=128,
                          n_maps=2, mask="chunk", lam_init=lam_init)
        conv_state, conv_layer = jnp.zeros((1, n_streams, CONV_W - 1, 2 * D_FF), F32), 0
    else:
        layer, c_ckv, c_kr, c_bk, c_bv, c_lf, c_ck, c_cv, conv_state = past
        p_len = c_ckv.shape[2]
        conv_state, conv_layer = conv_state[layer][None], 0
        lf_all = jnp.concatenate([c_lf[layer], lf.reshape(n_streams, t, B_HEADS)], axis=1)
        blocks = tuple((r, 512) for r in range(0, p_len, 512)) + ((p_len, t),)
        qa, ka, kaug = _fox_prep(lf_all, bq, bk16, t, p_len + t, blocks)
        oa = _mla_sample(aq, kf, va, c_ckv, c_kr, lw["w_ukv"], lw["gkn"], layer=layer, t=t)
        ob = _bc_sample(qa, ka, bv16, c_bk, c_bv, kaug, [], layer=layer, heads=B_HEADS,
                        n_maps=1, mask="causal", t=t)
        oc = _bc_sample(cq, ck16, cv16, c_ck, c_cv, None, lw["diff_extras"], layer=layer,
                        heads=C_HEADS, n_maps=2, mask="chunk", t=t, lam_init=lam_init)

    x1 = _mm_res(x, [oa, ob, oc], [lw["wo_a"], lw["wo_b"], lw["wo_c"]], 512, 1024)
    a, conv_g, conv_v = _ffn_up(x1, lw["ffn_norm"], lw["w_up"], lw["conv_w"], lw["conv_b"], conv_state,
                                layer=conv_layer, stream_len=t, tm=ffn_tm, tn=512)
    x2 = _mm_res(x1, [a], [lw["w_down"]], 512, 1024)
    states = (ckv.reshape(n_streams, t, A_KV_RANK), kr.reshape(n_streams, t, A_ROPE),
              bk32.reshape(n_streams, t, B_HEADS, B_DIM), bv32.reshape(n_streams, t, B_HEADS, B_DIM),
              lf.reshape(n_streams, t, B_HEADS),
              ck32.reshape(n_streams, t, C_HEADS, 2 * C_QK), cv32.reshape(n_streams, t, C_HEADS, C_V),
              jnp.concatenate([conv_g, conv_v], axis=-1))
    return x2, states


def kernel(x_prompt, x_sample, cache_a_ckv, cache_a_krope, cache_b_k, cache_b_v, cache_b_logf, cache_c_k, cache_c_v, state_ffn_conv, attn_norm, w_in, b_forget, a_kv_norm, a_w_ukv, a_qn_nope, a_qn_rope, a_kn_nope, a_kn_rope, b_qn, b_kn, c_qn, c_kn, c_lq1, c_lk1, c_lq2, c_lk2, c_out_norm, w_out, ffn_norm, w_up, conv_w, conv_b, w_down):
    bp, tp, d = x_prompt.shape
    bs, ts, _ = x_sample.shape
    depth = w_in.shape[0]
    past_len = cache_a_ckv.shape[2]
    post_tm = 256
    tabs_p = _rope_tables(jnp.arange(tp, dtype=jnp.int32), 1)
    tabs_s = _rope_tables(past_len + jnp.arange(ts, dtype=jnp.int32), post_tm // ts)

    merge_heads = lambda c: c.reshape(c.shape[:2] + (c.shape[2] * c.shape[3], c.shape[4]))
    caches = (cache_a_ckv.astype(BF16), cache_a_krope.astype(BF16), merge_heads(cache_b_k),
              merge_heads(cache_b_v), cache_b_logf, merge_heads(cache_c_k), merge_heads(cache_c_v),
              state_ffn_conv)
    yp = x_prompt.reshape(bp * tp, d)
    ys = x_sample.reshape(bs * ts, d)
    states_p, states_s = [], []
    for l in range(depth):
        lw = _layer_weights(l, attn_norm, w_in, b_forget, a_kv_norm, a_w_ukv, a_qn_nope, a_qn_rope,
                            a_kn_nope, a_kn_rope, b_qn, b_kn, c_qn, c_kn, c_lq1, c_lk1, c_lq2, c_lk2,
                            c_out_norm, w_out, ffn_norm, w_up, conv_w, conv_b, w_down)
        lam_init = 0.8 - 0.6 * math.exp(-0.3 * l)
        yp, st_p = _run_layer(yp, lw, tabs_p, None, n_streams=bp, t=tp, lam_init=lam_init,
                              post_tm=post_tm, ffn_tm=DENSE_TM)
        ys, st_s = _run_layer(ys, lw, tabs_s, (l,) + caches, n_streams=bs, t=ts, lam_init=lam_init,
                              post_tm=post_tm, ffn_tm=DENSE_TM)
        states_p.append(st_p)
        states_s.append(st_s)
    outs_p = [jnp.stack(s) for s in zip(*states_p)]
    outs_s = [jnp.stack(s) for s in zip(*states_s)]
    return (yp.reshape(bp, tp, d), ys.reshape(bs, ts, d), *outs_p, *outs_s)
```

```python
import functools
import math

import jax
import jax.numpy as jnp
from jax import lax
from jax.experimental import pallas as pl
from jax.experimental.pallas import tpu as pltpu

F32 = jnp.float32
BF16 = jnp.bfloat16

D_MODEL = 2048
CHUNK = 64
ROPE_THETA = 500000.0
EPS = 1e-6
NEG_INF = -1e30
LOG2E = 1.4426950408889634

A_HEADS = 8
A_NOPE = 128
A_ROPE = 64
A_V = 128
A_QK = A_NOPE + A_ROPE
A_KV_RANK = 512
B_HEADS = 4
B_DIM = 128
C_HEADS = 4
C_QK = 64
C_V = 128
C_ROT = 16
D_FF = 5632
CONV_W = 3

LANES = 128
HALO_ROW = 8
VMEM_LIMIT = 56 * 1024 * 1024

Z_QN, Z_QR, Z_CKV, Z_BQ, Z_BK, Z_BV, Z_CQ, Z_CK, Z_CV, Z_KR, Z_BF, Z_COLS = (
    0, 1024, 1536, 2048, 2560, 3072, 3584, 4096, 4608, 5120, 5248, 5376)
Z_TN = 896
DENSE_TM = 1024
FFN_CHUNK = 256
FFN_PARTS = 4
FFN_ROWS = 32
ATTN_TQ = 1024
ATTN_TK = 512
ATTN_SUB = 256


def _params(sem):
    return pltpu.CompilerParams(dimension_semantics=sem, vmem_limit_bytes=VMEM_LIMIT)


def _dot(a, b):
    return jnp.dot(a, b, preferred_element_type=F32)


def _dot_nt(a, b):
    return lax.dot_general(a, b, (((1,), (1,)), ((), ())), preferred_element_type=F32)


def _lane_iota(shape):
    return lax.broadcasted_iota(jnp.int32, shape, len(shape) - 1)


def _permute_w_in_kernel(w_ref, o_ref):
    def put(dst, src, n):
        o_ref[:, dst:dst + n] = w_ref[0, :, src:src + n]

    def clear(dst, n):
        o_ref[:, dst:dst + n] = jnp.zeros((o_ref.shape[0], n), BF16)

    for h in range(A_HEADS):
        put(Z_QN + h * A_NOPE, h * A_QK, A_NOPE)
        put(Z_QR + h * A_ROPE, h * A_QK + A_NOPE, A_ROPE)
    ckv0 = A_HEADS * A_QK
    kr0 = ckv0 + A_KV_RANK
    bq0 = kr0 + A_ROPE
    bf0 = bq0 + 3 * B_HEADS * B_DIM
    cq0 = bf0 + B_HEADS
    put(Z_CKV, ckv0, A_KV_RANK)
    put(Z_BQ, bq0, 3 * B_HEADS * B_DIM)
    put(Z_CQ, cq0, 3 * C_HEADS * C_V)
    put(Z_KR, kr0, A_ROPE)
    clear(Z_KR + A_ROPE, LANES - A_ROPE)
    put(Z_BF, bf0, B_HEADS)
    clear(Z_BF + B_HEADS, Z_COLS - Z_BF - B_HEADS)


def _permute_w_in(w_in, layer, tr):
    _, d, cols = w_in.shape
    return pl.pallas_call(
        _permute_w_in_kernel,
        grid=(d // tr,),
        in_specs=[pl.BlockSpec((1, tr, cols), lambda i: (layer, i, 0))],
        out_specs=pl.BlockSpec((tr, Z_COLS), lambda i: (i, 0)),
        out_shape=jax.ShapeDtypeStruct((d, Z_COLS), BF16),
        compiler_params=_params(("arbitrary",)),
        name="permute_w_in",
    )(w_in)


def _norm_mm_kernel(x_ref, g_ref, w_ref, o_ref, h_scr):
    @pl.when(pl.program_id(1) == 0)
    def _():
        x = x_ref[...]
        ms = jnp.mean(x * x, axis=-1, keepdims=True)
        h_scr[...] = (x * lax.rsqrt(ms + EPS) * g_ref[...]).astype(BF16)

    o_ref[...] = _dot(h_scr[...], w_ref[...])


def _norm_mm(x, g, w, tm, tn):
    m, d = x.shape
    n = w.shape[1]
    return pl.pallas_call(
        _norm_mm_kernel,
        grid=(m // tm, n // tn),
        in_specs=[pl.BlockSpec((tm, d), lambda i, j: (i, 0)),
                  pl.BlockSpec((1, d), lambda i, j: (0, 0)),
                  pl.BlockSpec((d, tn), lambda i, j: (0, j))],
        out_specs=pl.BlockSpec((tm, tn), lambda i, j: (i, j)),
        out_shape=jax.ShapeDtypeStruct((m, n), F32),
        scratch_shapes=[pltpu.VMEM((tm, d), BF16)],
        compiler_params=_params(("arbitrary", "arbitrary")),
        name="norm_mm",
    )(x, g, w)


def _seg_sumsq(x, bd):
    sq = x * x
    hi = sq.astype(BF16)
    lo = (sq - hi.astype(F32)).astype(BF16)
    return _dot(hi, bd) + _dot(lo, bd)


def _seg_norm(x, bd, seg, g):
    return x * lax.rsqrt(_seg_sumsq(x, bd) * (1.0 / seg) + EPS) * g


def _rope(x, cos, sin, half):
    n = x.shape[-1]
    first = (_lane_iota(x.shape) % 64) < half
    partner = jnp.where(first, pltpu.roll(x, n - half, 1), pltpu.roll(x, half, 1))
    return x * cos + partner * sin


def _store_heads(o_ref, x):
    rows = x.shape[0]
    heads = x.shape[1] // LANES
    for h in range(heads):
        o_ref[pl.ds(h, rows, stride=heads), :] = x[:, h * LANES:(h + 1) * LANES]


def _post_kernel(z_ref, cosa_ref, sina_ref, cosc_ref, sinc_ref,
                 gqn_ref, gqr_ref, gckv_ref, gkr_ref, gkn_ref, gbq_ref, gbk_ref, gcq_ref, gck_ref,
                 bfg_ref, bd64_ref, bd128_ref, ones_ref, bdkr_ref, wk_ref, wv_ref,
                 aq_ref, kf_ref, va_ref, ckv_ref, kr_ref, bq_ref, bk32_ref, bk16_ref,
                 bv32_ref, bv16_ref, lf_ref, cq_ref, ck32_ref, ck16_ref, cv32_ref, cv16_ref):
    bd64 = bd64_ref[...]
    bd128 = bd128_ref[...]
    tm = z_ref.shape[0]
    lane = _lane_iota((tm, LANES))
    low = lane < 64

    cosa4 = jnp.concatenate([cosa_ref[...]] * 4, axis=1)
    sina4 = jnp.concatenate([sina_ref[...]] * 4, axis=1)
    cosc4 = jnp.concatenate([cosc_ref[...]] * 4, axis=1)
    sinc4 = jnp.concatenate([sinc_ref[...]] * 4, axis=1)

    a_scale = A_QK ** -0.5 * LOG2E
    qr = _seg_norm(z_ref[:, Z_QR:Z_QR + 512], bd64, 64, gqr_ref[...])
    qr = _rope(qr, cosa4, sina4, A_ROPE // 2) * a_scale
    for half in range(2):
        c0 = Z_QN + half * 512
        qn = _seg_norm(z_ref[:, c0:c0 + 512], bd128, 128, gqn_ref[:, half * 512:(half + 1) * 512]) * a_scale
        for hh in range(4):
            h = half * 4 + hh
            aq_ref[:, h * 256:h * 256 + 128] = qn[:, hh * 128:(hh + 1) * 128].astype(BF16)
    for h in range(A_HEADS):
        pair = qr[:, (h // 2) * 128:(h // 2 + 1) * 128]
        keep = low if h % 2 == 0 else jnp.logical_not(low)
        aq_ref[:, h * 256 + 128:h * 256 + 256] = jnp.where(keep, pair, 0.0).astype(BF16)

    ckv = _seg_norm(z_ref[:, Z_CKV:Z_CKV + 512], ones_ref[...], 512, gckv_ref[...])
    ckv_ref[...] = ckv
    ckv16 = ckv.astype(BF16)
    va_ref[...] = _dot(ckv16, wv_ref[...]).astype(BF16)

    kr = _seg_norm(z_ref[:, Z_KR:Z_KR + 128], bdkr_ref[...], 64, gkr_ref[...])
    kr = _rope(kr, cosa_ref[...], sina_ref[...], A_ROPE // 2)
    kr_ref[...] = kr[:, :A_ROPE]
    kr2 = kr + pltpu.roll(kr, 64, 1)
    kr_even = jnp.where(low, kr2, 0.0).astype(BF16)
    kr_odd = jnp.where(low, 0.0, kr2).astype(BF16)
    for half in range(2):
        kk = _dot(ckv16, wk_ref[:, half * 512:(half + 1) * 512])
        kn = _seg_norm(kk, bd128, 128, gkn_ref[:, half * 512:(half + 1) * 512])
        for hh in range(4):
            h = half * 4 + hh
            kf_ref[:, h * 256:h * 256 + 128] = kn[:, hh * 128:(hh + 1) * 128].astype(BF16)
            kf_ref[:, h * 256 + 128:h * 256 + 256] = kr_even if h % 2 == 0 else kr_odd

    bq = _seg_norm(z_ref[:, Z_BQ:Z_BQ + 512], bd128, 128, gbq_ref[...]) * (B_DIM ** -0.5 * LOG2E)
    bq_ref[...] = bq.astype(BF16)
    bk = _seg_norm(z_ref[:, Z_BK:Z_BK + 512], bd128, 128, gbk_ref[...])
    _store_heads(bk32_ref, bk)
    bk16_ref[...] = bk.astype(BF16)
    bv = z_ref[:, Z_BV:Z_BV + 512]
    _store_heads(bv32_ref, bv)
    bv16_ref[...] = bv.astype(BF16)
    f = z_ref[:, Z_BF:Z_BF + 128][:, 0:B_HEADS] + bfg_ref[:, 0:B_HEADS]
    lf_ref[...] = jnp.minimum(f, 0.0) - jnp.log1p(jnp.exp(-jnp.abs(f)))

    cq = _seg_norm(z_ref[:, Z_CQ:Z_CQ + 512], bd64, 64, gcq_ref[...])
    cq_ref[...] = (_rope(cq, cosc4, sinc4, C_ROT // 2) * (C_QK ** -0.5 * LOG2E)).astype(BF16)
    ck = _seg_norm(z_ref[:, Z_CK:Z_CK + 512], bd64, 64, gck_ref[...])
    ck = _rope(ck, cosc4, sinc4, C_ROT // 2)
    _store_heads(ck32_ref, ck)
    ck16_ref[...] = ck.astype(BF16)
    cv = z_ref[:, Z_CV:Z_CV + 512]
    _store_heads(cv32_ref, cv)
    cv16_ref[...] = cv.astype(BF16)


def _post(z, tabs, consts, tm):
    m = z.shape[0]
    tab_rows = tabs[0].shape[0]
    tab_blocks = tab_rows // tm

    def row(w):
        return pl.BlockSpec((tm, w), lambda i: (i, 0))

    def full(a):
        return pl.BlockSpec(a.shape, lambda i: (0, 0))

    tab_spec = pl.BlockSpec((tm, LANES), lambda i: (i % tab_blocks, 0))
    widths = [(2048, BF16, 0), (2048, BF16, 0), (1024, BF16, 0), (512, F32, 0), (64, F32, 0), (512, BF16, 0),
              (512, F32, B_HEADS), (512, BF16, 0), (512, F32, B_HEADS), (512, BF16, 0), (B_HEADS, F32, 0),
              (512, BF16, 0), (512, F32, C_HEADS), (512, BF16, 0), (512, F32, C_HEADS), (512, BF16, 0)]
    out_spec = lambda w, hd: row(w) if hd == 0 else pl.BlockSpec((tm * hd, w // hd), lambda i: (i, 0))
    out_sds = lambda w, dt, hd: jax.ShapeDtypeStruct((m, w) if hd == 0 else (m * hd, w // hd), dt)
    return pl.pallas_call(
        _post_kernel,
        grid=(m // tm,),
        in_specs=[row(Z_COLS)] + [tab_spec] * 4 + [full(c) for c in consts],
        out_specs=[out_spec(w, hd) for w, _, hd in widths],
        out_shape=[out_sds(w, dt, hd) for w, dt, hd in widths],
        compiler_params=_params(("arbitrary",)),
        name="post",
    )(z, *tabs, *consts)


def _split3(c):
    hi = c.astype(BF16).astype(F32)
    r1 = c - hi
    mid = r1.astype(BF16).astype(F32)
    lo = r1 - mid
    return hi, mid, lo


def _cumsum_block(lf, carry):
    n = lf.shape[0]
    r = lax.broadcasted_iota(jnp.int32, (n, n), 0)
    c = lax.broadcasted_iota(jnp.int32, (n, n), 1)
    tri = jnp.where(c <= r, 1.0, 0.0).astype(BF16)
    hi, mid, lo = _split3(lf)
    cum = _dot(tri, hi.astype(BF16)) + _dot(tri, mid.astype(BF16)) + _dot(tri, lo.astype(BF16))
    return cum + carry


def _aug_q(c, lane):
    hi, mid, lo = _split3(c)
    return jnp.where(lane == 0, hi, jnp.where(lane == 1, mid, jnp.where(lane == 2, lo,
                     jnp.where(lane < 6, 1.0, 0.0)))).astype(BF16)


def _aug_k(c, lane):
    hi, mid, lo = _split3(c)
    return jnp.where(lane < 3, 1.0, jnp.where(lane == 3, -hi, jnp.where(lane == 4, -mid,
                     jnp.where(lane == 5, -lo, 0.0)))).astype(BF16)


def _fox_prep_kernel(lf_ref, bq_ref, bk_ref, *out_refs, past, blocks):
    carry_scr = out_refs[-1]
    if past:
        qa_ref, ka_ref, kpast_ref = out_refs[:-1]
    else:
        qa_ref, ka_ref = out_refs[:-1]

    @pl.when(pl.program_id(1) == 0)
    def _():
        carry_scr[...] = jnp.zeros(carry_scr.shape, F32)

    carry = carry_scr[...]
    for r0, n in blocks:
        cum = _cumsum_block(lf_ref[0, r0:r0 + n, :], carry)
        carry = cum[n - 1:n, :]
        lane = _lane_iota((n, LANES))
        for h in range(B_HEADS):
            c = cum[:, h:h + 1] * LOG2E
            if r0 < past:
                kpast_ref[0, r0:r0 + n, h * 128:(h + 1) * 128] = _aug_k(c, lane)
            else:
                t0 = r0 - past
                qa_ref[t0:t0 + n, h * 256:h * 256 + 128] = bq_ref[t0:t0 + n, h * 128:(h + 1) * 128]
                qa_ref[t0:t0 + n, h * 256 + 128:h * 256 + 256] = _aug_q(c, lane)
                ka_ref[t0:t0 + n, h * 256:h * 256 + 128] = bk_ref[t0:t0 + n, h * 128:(h + 1) * 128]
                ka_ref[t0:t0 + n, h * 256 + 128:h * 256 + 256] = _aug_k(c, lane)
    carry_scr[...] = carry


def _fox_prep(lf_all, bq, bk, t_new, rows, blocks):
    s, t_tot, _ = lf_all.shape
    past = t_tot - t_new
    nb = t_tot // rows
    new_rows = rows - past
    m = bq.shape[0]
    new_spec = lambda w: pl.BlockSpec((new_rows, w), lambda i, j: (i * nb + j, 0))
    out_specs = [new_spec(1024), new_spec(1024)]
    out_shape = [jax.ShapeDtypeStruct((m, 1024), BF16), jax.ShapeDtypeStruct((m, 1024), BF16)]
    if past:
        out_specs.append(pl.BlockSpec((1, past, 512), lambda i, j: (i, 0, 0)))
        out_shape.append(jax.ShapeDtypeStruct((s, past, 512), BF16))
    return pl.pallas_call(
        functools.partial(_fox_prep_kernel, past=past, blocks=blocks),
        grid=(s, nb),
        in_specs=[pl.BlockSpec((1, rows, B_HEADS), lambda i, j: (i, j, 0)), new_spec(512), new_spec(512)],
        out_specs=out_specs,
        out_shape=out_shape,
        scratch_shapes=[pltpu.VMEM((1, B_HEADS), F32)],
        compiler_params=_params(("arbitrary", "arbitrary")),
        name="fox_prep",
    )(lf_all, bq, bk)


def _lanes(x, n):
    return x if n == 1 else jnp.concatenate([x] * n, axis=1)


def _with_ones(v):
    return jnp.concatenate([v, jnp.ones_like(v)], axis=1)


def _softmax_start(rows):
    return jnp.full((rows, LANES), NEG_INF, F32), jnp.zeros((rows, 2 * LANES), F32)


def _softmax_step(state, s, v1):
    m_old, acc = state
    tk = s.shape[1]
    m_new = jnp.maximum(m_old, jnp.max(s, axis=-1, keepdims=True))
    alpha = jnp.exp2(m_old - m_new)
    m_full = _lanes(m_new, tk // LANES) if tk >= LANES else m_new[:, :tk]
    p = jnp.exp2(s - m_full)
    return m_new, _lanes(alpha, 2) * acc + _dot(p.astype(BF16), v1)


def _softmax_out(state):
    acc = state[1]
    return acc[:, :LANES] / acc[:, LANES:]


def _visible(kind, q0, k0, tq, tk):
    qp = q0 + lax.broadcasted_iota(jnp.int32, (tq, tk), 0)
    kp = k0 + lax.broadcasted_iota(jnp.int32, (tq, tk), 1)
    if kind == "chunk":
        return (kp // CHUNK) <= (qp // CHUNK)
    return kp <= qp


def _split_maps(q, n_maps):
    if n_maps == 1:
        return [q]
    low = _lane_iota(q.shape) < C_QK
    zero = jnp.zeros_like(q)
    return [jnp.where(low, q, zero), jnp.where(low, zero, q)]


def _diff_lambda(lq1_ref, lk1_ref, lq2_ref, lk2_ref, lam_init):
    s1 = jnp.sum(lq1_ref[...] * lk1_ref[...], axis=-1, keepdims=True)
    s2 = jnp.sum(lq2_ref[...] * lk2_ref[...], axis=-1, keepdims=True)
    return jnp.exp(s1) - jnp.exp(s2) + lam_init


def _attn_finish(states, extra_refs, lam_init):
    if len(states) == 1:
        return _softmax_out(states[0])
    lq1_ref, lk1_ref, lq2_ref, lk2_ref, gout_ref = extra_refs
    lam = _diff_lambda(lq1_ref, lk1_ref, lq2_ref, lk2_ref, lam_init)
    o = _softmax_out(states[0]) - lam * _softmax_out(states[1])
    ms = jnp.mean(o * o, axis=-1, keepdims=True)
    return o * lax.rsqrt(ms + EPS) * gout_ref[...] * (1.0 - lam_init)


def _attn_prompt_kernel(*refs, n_maps, mask, tq, tk, sub, lam_init):
    q_ref, k_ref, v_ref = refs[:3]
    extra_refs = refs[3:-3]
    o_ref, m_scr, acc_scr = refs[-3:]
    qi = pl.program_id(2)
    n_sub = tq // sub
    m0, acc0 = _softmax_start(sub)
    for i in range(n_maps):
        for r in range(n_sub):
            m_scr[i, r] = m0
            acc_scr[i, r] = acc0

    def blocks(key_blocks):
        todo = []
        for k0, nk, diag in key_blocks:
            k = k_ref[pl.ds(k0, nk), :]
            v1 = _with_ones(v_ref[pl.ds(k0, nk), :])
            for r in range(n_sub):
                vis = None
                if diag is not None:
                    if diag > r * sub + sub - 1:
                        continue
                    if diag + nk - 1 > r * sub:
                        vis = _visible(mask, r * sub, diag, sub, nk)
                scores = [_dot_nt(qm, k) for qm in _split_maps(q_ref[r * sub:(r + 1) * sub, :], n_maps)]
                todo.append((r, vis, scores, v1))
        for r, vis, scores, v1 in todo:
            for i in range(n_maps):
                s = scores[i] if vis is None else jnp.where(vis, scores[i], NEG_INF)
                m_scr[i, r], acc_scr[i, r] = _softmax_step((m_scr[i, r], acc_scr[i, r]), s, v1)

    def full_block(kb, carry):
        blocks([(pl.multiple_of(kb * tq, tq), tq, None)])
        return carry

    lax.fori_loop(0, qi, full_block, 0)
    blocks([(pl.multiple_of(qi * tq + d * tk, tk), tk, d * tk) for d in range(tq // tk)])
    for r in range(n_sub):
        states = [(m_scr[i, r], acc_scr[i, r]) for i in range(n_maps)]
        o_ref[r * sub:(r + 1) * sub, :] = _attn_finish(states, extra_refs, lam_init).astype(o_ref.dtype)


def _attn_prompt(q, k, v, extras, *, batch, heads, t, dqk, n_maps, mask, lam_init=0.0):
    tq, tk, sub = ATTN_TQ, ATTN_TK, ATTN_SUB
    nq = t // tq
    dv = 128
    extra_specs = [pl.BlockSpec(e.shape, lambda b, h, i: (0, 0)) for e in extras]
    return pl.pallas_call(
        functools.partial(_attn_prompt_kernel, n_maps=n_maps, mask=mask, tq=tq, tk=tk, sub=sub,
                          lam_init=lam_init),
        grid=(batch, heads, nq),
        in_specs=[pl.BlockSpec((tq, dqk), lambda b, h, i: (b * nq + i, h)),
                  pl.BlockSpec((t, dqk), lambda b, h, i: (b, h)),
                  pl.BlockSpec((t, dv), lambda b, h, i: (b, h))] + extra_specs,
        out_specs=pl.BlockSpec((tq, dv), lambda b, h, i: (b * nq + i, h)),
        out_shape=jax.ShapeDtypeStruct((batch * t, heads * dv), BF16),
        scratch_shapes=[pltpu.VMEM((n_maps, tq // sub, sub, LANES), F32),
                        pltpu.VMEM((n_maps, tq // sub, sub, 2 * LANES), F32)],
        compiler_params=_params(("arbitrary", "arbitrary", "arbitrary")),
        name="attn_prompt_" + mask + str(n_maps),
    )(q, k, v, *extras)


def _mla_sample_kernel(q_ref, kf_ref, va_ref, ckv_ref, kr_ref, wukv_ref, gkn_ref, o_ref, *, past, t):
    ckv_p = ckv_ref[0, 0]
    kr_p = kr_ref[0, 0]
    vis = _visible("chunk", past, past, t, t)
    heads = range(A_HEADS)
    kvs = [_dot(ckv_p, wukv_ref[:, h * 256:(h + 1) * 256]) for h in heads]
    kns, vps = [], []
    for h in heads:
        kk = kvs[h][:, :A_NOPE]
        ms = jnp.mean(kk * kk, axis=-1, keepdims=True)
        kns.append((kk * lax.rsqrt(ms + EPS) * gkn_ref[...]).astype(BF16))
        vps.append(_with_ones(kvs[h][:, A_NOPE:].astype(BF16)))
    s_past, s_new = [], []
    for h in heads:
        q = q_ref[:, h * 256:(h + 1) * 256]
        r0 = 128 + (h % 2) * A_ROPE
        s_past.append(_dot_nt(q[:, :128], kns[h]) + _dot_nt(q[:, r0:r0 + A_ROPE], kr_p))
        s_new.append(jnp.where(vis, _dot_nt(q, kf_ref[:, h * 256:(h + 1) * 256]), NEG_INF))
    for h in heads:
        state = _softmax_step(_softmax_start(t), s_past[h], vps[h])
        state = _softmax_step(state, s_new[h], _with_ones(va_ref[:, h * 128:(h + 1) * 128]))
        o_ref[:, h * 128:(h + 1) * 128] = _softmax_out(state).astype(o_ref.dtype)


def _mla_sample(q, kf, va, cache_ckv, cache_kr, w_ukv, gkn, *, layer, t):
    _, s, past, rank = cache_ckv.shape
    return pl.pallas_call(
        functools.partial(_mla_sample_kernel, past=past, t=t),
        grid=(s,),
        in_specs=[pl.BlockSpec((t, 2048), lambda i: (i, 0)),
                  pl.BlockSpec((t, 2048), lambda i: (i, 0)),
                  pl.BlockSpec((t, 1024), lambda i: (i, 0)),
                  pl.BlockSpec((1, 1, past, rank), lambda i: (layer, i, 0, 0)),
                  pl.BlockSpec((1, 1, past, A_ROPE), lambda i: (layer, i, 0, 0)),
                  pl.BlockSpec(w_ukv.shape, lambda i: (0, 0)),
                  pl.BlockSpec(gkn.shape, lambda i: (0, 0))],
        out_specs=pl.BlockSpec((t, 1024), lambda i: (i, 0)),
        out_shape=jax.ShapeDtypeStruct((s * t, 1024), BF16),
        compiler_params=_params(("arbitrary",)),
        name="mla_sample",
    )(q, kf, va, cache_ckv, cache_kr, w_ukv, gkn)


def _bc_sample_kernel(*refs, n_maps, mask, heads, past, t, lam_init, with_aug):
    q_ref, kn_ref, vn_ref, kp_ref, vp_ref = refs[:5]
    rest = refs[5:-1]
    o_ref = refs[-1]
    if with_aug:
        kaug_ref, extra_refs = rest[0], rest[1:]
    else:
        extra_refs = rest
    dq = q_ref.shape[1] // heads
    vis = _visible(mask, past, past, t, t)
    scores, vps = [], []
    for h in range(heads):
        q = q_ref[:, h * dq:(h + 1) * dq]
        kp = kp_ref[0, 0, pl.ds(h, past, stride=heads), :].astype(BF16)
        vps.append(_with_ones(vp_ref[0, 0, pl.ds(h, past, stride=heads), :].astype(BF16)))
        kn = kn_ref[:, h * dq:(h + 1) * dq]
        if with_aug:
            s_past = _dot_nt(q[:, :128], kp) + _dot_nt(q[:, 128:], kaug_ref[0, :, h * 128:(h + 1) * 128])
            scores.append([(s_past, jnp.where(vis, _dot_nt(q, kn), NEG_INF))])
        else:
            scores.append([(_dot_nt(qm, kp), jnp.where(vis, _dot_nt(qm, kn), NEG_INF))
                           for qm in _split_maps(q, n_maps)])
    for h in range(heads):
        vn = _with_ones(vn_ref[:, h * 128:(h + 1) * 128])
        states = [_softmax_step(_softmax_step(_softmax_start(t), s_past, vps[h]), s_new, vn)
                  for s_past, s_new in scores[h]]
        o_ref[:, h * 128:(h + 1) * 128] = _attn_finish(states, extra_refs, lam_init).astype(o_ref.dtype)


def _bc_sample(q, kn, vn, cache_k, cache_v, kaug, extras, *, layer, heads, n_maps, mask, t, lam_init=0.0):
    _, s, rows, _ = cache_k.shape
    past = rows // heads
    dq = q.shape[1]
    cache_spec = pl.BlockSpec((1, 1, rows, 128), lambda i: (layer, i, 0, 0))
    in_specs = [pl.BlockSpec((t, dq), lambda i: (i, 0)),
                pl.BlockSpec((t, dq), lambda i: (i, 0)),
                pl.BlockSpec((t, heads * 128), lambda i: (i, 0)),
                cache_spec, cache_spec]
    args = [q, kn, vn, cache_k, cache_v]
    if kaug is not None:
        in_specs.append(pl.BlockSpec((1, past, heads * 128), lambda i: (i, 0, 0)))
        args.append(kaug)
    in_specs += [pl.BlockSpec(e.shape, lambda i: (0, 0)) for e in extras]
    return pl.pallas_call(
        functools.partial(_bc_sample_kernel, n_maps=n_maps, mask=mask, heads=heads, past=past, t=t,
                          lam_init=lam_init, with_aug=kaug is not None),
        grid=(s,),
        in_specs=in_specs,
        out_specs=pl.BlockSpec((t, heads * 128), lambda i: (i, 0)),
        out_shape=jax.ShapeDtypeStruct((s * t, heads * 128), BF16),
        compiler_params=_params(("arbitrary",)),
        name="sample_" + mask + str(n_maps),
    )(*args, *extras)


def _mm_res_kernel(*refs, n_in):
    res_ref = refs[2 * n_in]
    o_ref = refs[2 * n_in + 1]
    acc = res_ref[...]
    for i in range(n_in):
        acc = acc + _dot(refs[i][...], refs[n_in + i][...])
    o_ref[...] = acc


def _mm_res(res, a_list, w_list, tm, tn):
    m, n = res.shape
    n_in = len(a_list)
    in_specs = ([pl.BlockSpec((tm, a.shape[1]), lambda j, i: (i, 0)) for a in a_list]
                + [pl.BlockSpec((w.shape[0], tn), lambda j, i: (0, j)) for w in w_list]
                + [pl.BlockSpec((tm, tn), lambda j, i: (i, j))])
    return pl.pallas_call(
        functools.partial(_mm_res_kernel, n_in=n_in),
        grid=(n // tn, m // tm),
        in_specs=in_specs,
        out_specs=pl.BlockSpec((tm, tn), lambda j, i: (i, j)),
        out_shape=jax.ShapeDtypeStruct((m, n), F32),
        compiler_params=_params(("arbitrary", "arbitrary")),
        name="mm_res%d" % n_in,
    )(*a_list, *w_list, res)


def _ffn_up_kernel(x_ref, g_ref, wg_ref, wv_ref, cwg_ref, cwv_ref, cbg_ref, cbv_ref, sg_ref, sv_ref,
                   a_ref, ng_ref, nv_ref, h_scr, rawg, rawv, carg, carv, *, rows, tiles_per_stream):
    i = pl.program_id(0)
    j = pl.program_id(1)
    tm = x_ref.shape[0]
    n_sub = tm // rows
    tn = a_ref.shape[1]
    chunks = [(c, c + FFN_CHUNK) for c in range(0, tn, FFN_CHUNK)]

    @pl.when(j == 0)
    def _():
        x = x_ref[...]
        ms = jnp.mean(x * x, axis=-1, keepdims=True)
        h_scr[...] = (x * lax.rsqrt(ms + EPS) * g_ref[...]).astype(BF16)

    part = tm // FFN_PARTS
    items = [(c0, c1, lo, lo + part) for c0, c1 in chunks for lo in range(0, tm, part)]
    dots = [(_dot(h_scr[lo:hi, :], wg_ref[:, c0:c1]), _dot(h_scr[lo:hi, :], wv_ref[:, c0:c1]))
            for c0, c1, lo, hi in items]
    for (c0, c1, lo, hi), us in zip(items, dots):
        sides = ((us[0], rawg, carg, sg_ref, cwg_ref, cbg_ref, ng_ref),
                 (us[1], rawv, carv, sv_ref, cwv_ref, cbv_ref, nv_ref))
        segs = [(s, max(lo, s * rows), min(hi, (s + 1) * rows)) for s in range(n_sub)
                if s * rows < hi and (s + 1) * rows > lo]
        for u, raw, car, s_ref, _, _, n_ref in sides:
            for s, a, b in segs:
                raw[s, HALO_ROW + a - s * rows:HALO_ROW + b - s * rows, c0:c1] = u[a - lo:b - lo]
                if a == s * rows:
                    if tiles_per_stream == 1:
                        halo = s_ref[0, s, :, c0:c1]
                    else:
                        halo = jnp.where((i % tiles_per_stream) == 0, s_ref[0, 0, :, c0:c1], car[j, :, c0:c1])
                    raw[s, HALO_ROW - 2:HALO_ROW, c0:c1] = halo
                if b == (s + 1) * rows:
                    n_ref[s, :, c0:c1] = u[b - lo - 2:b - lo]
            if tiles_per_stream > 1 and hi == tm:
                car[j, :, c0:c1] = u[hi - lo - 2:hi - lo]

        def conv(side, s, r0):
            _, raw, _, _, cw_ref, cb_ref, _ = side
            base = HALO_ROW + r0
            return (cb_ref[:, c0:c1]
                    + raw[s, base - 2:base - 2 + FFN_ROWS, c0:c1] * cw_ref[0:1, c0:c1]
                    + raw[s, base - 1:base - 1 + FFN_ROWS, c0:c1] * cw_ref[1:2, c0:c1]
                    + raw[s, base:base + FFN_ROWS, c0:c1] * cw_ref[2:3, c0:c1])

        for s, a, b in segs:
            for r in range(a, b, FFN_ROWS):
                gate = conv(sides[0], s, r - s * rows)
                val = conv(sides[1], s, r - s * rows)
                a_ref[r:r + FFN_ROWS, c0:c1] = (gate * (1.0 / (1.0 + jnp.exp(-gate))) * val).astype(BF16)


def _ffn_up(x, g, w_up, conv_w, conv_b, state, *, layer, stream_len, tm, tn):
    m, d = x.shape
    nj = D_FF // tn
    if stream_len >= tm:
        rows, tiles_per_stream, n_sub = tm, stream_len // tm, 1
        stream_of = lambda i: i // tiles_per_stream
    else:
        rows, tiles_per_stream, n_sub = stream_len, 1, tm // stream_len
        stream_of = lambda i: i
    gcol = lambda i, j: (0, j)
    vcol = lambda i, j: (0, nj + j)
    raw = pltpu.VMEM((n_sub, HALO_ROW + rows, tn), F32)
    outs = pl.pallas_call(
        functools.partial(_ffn_up_kernel, rows=rows, tiles_per_stream=tiles_per_stream),
        grid=(m // tm, nj),
        in_specs=[pl.BlockSpec((tm, d), lambda i, j: (i, 0)),
                  pl.BlockSpec((1, d), lambda i, j: (0, 0)),
                  pl.BlockSpec((d, tn), gcol), pl.BlockSpec((d, tn), vcol),
                  pl.BlockSpec((CONV_W, tn), gcol), pl.BlockSpec((CONV_W, tn), vcol),
                  pl.BlockSpec((1, tn), gcol), pl.BlockSpec((1, tn), vcol),
                  pl.BlockSpec((1, n_sub, 2, tn), lambda i, j: (layer, stream_of(i), 0, j)),
                  pl.BlockSpec((1, n_sub, 2, tn), lambda i, j: (layer, stream_of(i), 0, nj + j))],
        out_specs=[pl.BlockSpec((tm, tn), lambda i, j: (i, j)),
                   pl.BlockSpec((n_sub, 2, tn), lambda i, j: (i, 0, j)),
                   pl.BlockSpec((n_sub, 2, tn), lambda i, j: (i, 0, j))],
        out_shape=[jax.ShapeDtypeStruct((m, D_FF), BF16),
                   jax.ShapeDtypeStruct((m // rows, 2, D_FF), F32),
                   jax.ShapeDtypeStruct((m // rows, 2, D_FF), F32)],
        scratch_shapes=[pltpu.VMEM((tm, d), BF16), raw, raw,
                        pltpu.VMEM((nj, 2, tn), F32), pltpu.VMEM((nj, 2, tn), F32)],
        compiler_params=_params(("arbitrary", "arbitrary")),
        name="ffn_up",
    )(x, g, w_up, w_up, conv_w, conv_w, conv_b, conv_b, state, state)
    a, tail_g, tail_v = outs
    last = slice(tiles_per_stream - 1, None, tiles_per_stream)
    return a, tail_g[last], tail_v[last]


def _rope_tables(pos, reps):
    posf = pos.astype(F32)[:, None]

    def tab(n_rot):
        half = n_rot // 2
        inv = jnp.power(jnp.float32(ROPE_THETA), -jnp.arange(half, dtype=F32) * (2.0 / n_rot))
        ang = posf * inv[None, :]
        rest = 64 - n_rot
        cos = jnp.concatenate([jnp.cos(ang), jnp.cos(ang), jnp.ones((pos.shape[0], rest), F32)], axis=1)
        sin = jnp.concatenate([-jnp.sin(ang), jnp.sin(ang), jnp.zeros((pos.shape[0], rest), F32)], axis=1)
        return jnp.tile(cos, (reps, 2)), jnp.tile(sin, (reps, 2))

    cosa, sina = tab(A_ROPE)
    cosc, sinc = tab(C_ROT)
    return cosa, sina, cosc, sinc


def _block_diag(width, seg):
    r = jnp.arange(width)[:, None] // seg
    c = jnp.arange(width)[None, :] // seg
    return (r == c).astype(BF16)


def _layer_weights(l, attn_norm, w_in, b_forget, a_kv_norm, a_w_ukv, a_qn_nope, a_qn_rope, a_kn_nope,
                   a_kn_rope, b_qn, b_kn, c_qn, c_kn, c_lq1, c_lk1, c_lq2, c_lk2, c_out_norm, w_out,
                   ffn_norm, w_up, conv_w, conv_b, w_down):
    w_in_p = _permute_w_in(w_in.astype(BF16), l, 256)
    ukv = a_w_ukv[l].reshape(A_KV_RANK, A_HEADS, A_NOPE + A_V)
    wk = ukv[:, :, :A_NOPE].reshape(A_KV_RANK, 1024).astype(BF16)
    wv = ukv[:, :, A_NOPE:].reshape(A_KV_RANK, 1024).astype(BF16)
    row = lambda v: v.reshape(1, -1).astype(F32)
    tile = lambda v, n: jnp.tile(v.astype(F32), n).reshape(1, -1)
    bd_kr = (jnp.arange(128)[:, None] < 64).astype(BF16) * jnp.ones((1, 128), BF16)
    post_consts = [
        tile(a_qn_nope[l], 8), tile(a_qn_rope[l], 8), row(a_kv_norm[l]),
        jnp.concatenate([a_kn_rope[l], jnp.zeros((64,), F32)]).reshape(1, 128),
        tile(a_kn_nope[l], 8), tile(b_qn[l], 4), tile(b_kn[l], 4), tile(c_qn[l], 8), tile(c_kn[l], 8),
        jnp.concatenate([b_forget[l], jnp.zeros((128 - B_HEADS,), F32)]).reshape(1, 128),
        _block_diag(512, 64), _block_diag(512, 128), jnp.ones((512, 512), BF16), bd_kr, wk, wv]
    wo = w_out[l].astype(BF16)
    return dict(
        attn_norm=row(attn_norm[l]), w_in=w_in_p, post_consts=post_consts,
        w_ukv=a_w_ukv[l].astype(BF16), gkn=row(a_kn_nope[l]),
        diff_extras=[row(c_lq1[l]), row(c_lk1[l]), row(c_lq2[l]), row(c_lk2[l]), row(c_out_norm[l])],
        wo_a=wo[:1024], wo_b=wo[1024:1536], wo_c=wo[1536:],
        ffn_norm=row(ffn_norm[l]), w_up=w_up[l].astype(BF16), conv_w=conv_w[l].astype(F32),
        conv_b=row(conv_b[l]), w_down=w_down[l].astype(BF16))


def _run_layer(x, lw, tabs, past, *, n_streams, t, lam_init, post_tm, ffn_tm):
    m = x.shape[0]
    z = _norm_mm(x, lw["attn_norm"], lw["w_in"], DENSE_TM, Z_TN)
    (aq, kf, va, ckv, kr, bq, bk32, bk16, bv32, bv16, lf, cq, ck32, ck16, cv32, cv16) = _post(
        z, tabs, lw["post_consts"], post_tm)

    if past is None:
        qa, ka = _fox_prep(lf.reshape(n_streams, t, B_HEADS), bq, bk16, t, 1024, ((0, 512), (512, 512)))
        oa = _attn_prompt(aq, kf, va, [], batch=n_streams, heads=A_HEADS, t=t, dqk=256, n_maps=1,
                          mask="chunk")
        ob = _attn_prompt(qa, ka, bv16, [], batch=n_streams, heads=B_HEADS, t=t, dqk=256, n_maps=1,
                          mask="causal")
        oc = _attn_prompt(cq, ck16, cv16, lw["diff_extras"], batch=n_streams, heads=C_HEADS, t=t, dqk=128,
                          n_maps=2, mask="chunk", lam_init=lam_init)
        conv_state, conv_layer = jnp.zeros((1, n_streams, CONV_W - 1, 2 * D_FF), F32), 0
    else:
        layer, c_ckv, c_kr, c_bk, c_bv, c_lf, c_ck, c_cv, conv_state = past
        p_len = c_ckv.shape[2]
        conv_state, conv_layer = conv_state[layer][None], 0
        lf_all = jnp.concatenate([c_lf[layer], lf.reshape(n_streams, t, B_HEADS)], axis=1)
        blocks = tuple((r, 512) for r in range(0, p_len, 512)) + ((p_len, t),)
        qa, ka, kaug = _fox_prep(lf_all, bq, bk16, t, p_len + t, blocks)
        oa = _mla_sample(aq, kf, va, c_ckv, c_kr, lw["w_ukv"], lw["gkn"], layer=layer, t=t)
        ob = _bc_sample(qa, ka, bv16, c_bk, c_bv, kaug, [], layer=layer, heads=B_HEADS,
                        n_maps=1, mask="causal", t=t)
        oc = _bc_sample(cq, ck16, cv16, c_ck, c_cv, None, lw["diff_extras"], layer=layer,
                        heads=C_HEADS, n_maps=2, mask="chunk", t=t, lam_init=lam_init)

    x1 = _mm_res(x, [oa, ob, oc], [lw["wo_a"], lw["wo_b"], lw["wo_c"]], 512, 1024)
    a, conv_g, conv_v = _ffn_up(x1, lw["ffn_norm"], lw["w_up"], lw["conv_w"], lw["conv_b"], conv_state,
                                layer=conv_layer, stream_len=t, tm=ffn_tm, tn=512)
    x2 = _mm_res(x1, [a], [lw["w_down"]], 512, 1024)
    states = (ckv.reshape(n_streams, t, A_KV_RANK), kr.reshape(n_streams, t, A_ROPE),
              bk32.reshape(n_streams, t, B_HEADS, B_DIM), bv32.reshape(n_streams, t, B_HEADS, B_DIM),
              lf.reshape(n_streams, t, B_HEADS),
              ck32.reshape(n_streams, t, C_HEADS, 2 * C_QK), cv32.reshape(n_streams, t, C_HEADS, C_V),
              jnp.concatenate([conv_g, conv_v], axis=-1))
    return x2, states


def kernel(x_prompt, x_sample, cache_a_ckv, cache_a_krope, cache_b_k, cache_b_v, cache_b_logf, cache_c_k, cache_c_v, state_ffn_conv, attn_norm, w_in, b_forget, a_kv_norm, a_w_ukv, a_qn_nope, a_qn_rope, a_kn_nope, a_kn_rope, b_qn, b_kn, c_qn, c_kn, c_lq1, c_lk1, c_lq2, c_lk2, c_out_norm, w_out, ffn_norm, w_up, conv_w, conv_b, w_down):
    bp, tp, d = x_prompt.shape
    bs, ts, _ = x_sample.shape
    depth = w_in.shape[0]
    past_len = cache_a_ckv.shape[2]
    post_tm = 256
    tabs_p = _rope_tables(jnp.arange(tp, dtype=jnp.int32), 1)
    tabs_s = _rope_tables(past_len + jnp.arange(ts, dtype=jnp.int32), post_tm // ts)

    merge_heads = lambda c: c.reshape(c.shape[:2] + (c.shape[2] * c.shape[3], c.shape[4]))
    caches = (cache_a_ckv.astype(BF16), cache_a_krope.astype(BF16), merge_heads(cache_b_k),
              merge_heads(cache_b_v), cache_b_logf, merge_heads(cache_c_k), merge_heads(cache_c_v),
              state_ffn_conv)
    yp = x_prompt.reshape(bp * tp, d)
    ys = x_sample.reshape(bs * ts, d)
    states_p, states_s = [], []
    for l in range(depth):
        lw = _layer_weights(l, attn_norm, w_in, b_forget, a_kv_norm, a_w_ukv, a_qn_nope, a_qn_rope,
                            a_kn_nope, a_kn_rope, b_qn, b_kn, c_qn, c_kn, c_lq1, c_lk1, c_lq2, c_lk2,
                            c_out_norm, w_out, ffn_norm, w_up, conv_w, conv_b, w_down)
        lam_init = 0.8 - 0.6 * math.exp(-0.3 * l)
        yp, st_p = _run_layer(yp, lw, tabs_p, None, n_streams=bp, t=tp, lam_init=lam_init,
                              post_tm=post_tm, ffn_tm=DENSE_TM)
        ys, st_s = _run_layer(ys, lw, tabs_s, (l,) + caches, n_streams=bs, t=ts, lam_init=lam_init,
                              post_tm=post_tm, ffn_tm=DENSE_TM)
        states_p.append(st_p)
        states_s.append(st_s)
    outs_p = [jnp.stack(s) for s in zip(*states_p)]
    outs_s = [jnp.stack(s) for s in zip(*states_s)]
    return (yp.reshape(bp, tp, d), ys.reshape(bs, ts, d), *outs_p, *outs_s)
```

```python
import functools
import math

import jax
import jax.numpy as jnp
from jax import lax
from jax.experimental import pallas as pl
from jax.experimental.pallas import tpu as pltpu

F32 = jnp.float32
BF16 = jnp.bfloat16

D_MODEL = 2048
CHUNK = 64
ROPE_THETA = 500000.0
EPS = 1e-6
NEG_INF = -1e30
LOG2E = 1.4426950408889634

A_HEADS = 8
A_NOPE = 128
A_ROPE = 64
A_V = 128
A_QK = A_NOPE + A_ROPE
A_KV_RANK = 512
B_HEADS = 4
B_DIM = 128
C_HEADS = 4
C_QK = 64
C_V = 128
C_ROT = 16
D_FF = 5632
CONV_W = 3

LANES = 128
HALO_ROW = 8
VMEM_LIMIT = 56 * 1024 * 1024

Z_QN, Z_QR, Z_CKV, Z_BQ, Z_BK, Z_BV, Z_CQ, Z_CK, Z_CV, Z_KR, Z_BF, Z_COLS = (
    0, 1024, 1536, 2048, 2560, 3072, 3584, 4096, 4608, 5120, 5248, 5376)
Z_TN = 896
DENSE_TM = 1024
FFN_CHUNK = 256
FFN_PARTS = 4
FFN_ROWS = 32
ATTN_TQ = 1024
ATTN_TK = 512
ATTN_SUB = 256


def _params(sem):
    return pltpu.CompilerParams(dimension_semantics=sem, vmem_limit_bytes=VMEM_LIMIT)


def _dot(a, b):
    return jnp.dot(a, b, preferred_element_type=F32)


def _dot_nt(a, b):
    return lax.dot_general(a, b, (((1,), (1,)), ((), ())), preferred_element_type=F32)


def _lane_iota(shape):
    return lax.broadcasted_iota(jnp.int32, shape, len(shape) - 1)


def _permute_w_in_kernel(w_ref, o_ref):
    def put(dst, src, n):
        o_ref[:, dst:dst + n] = w_ref[0, :, src:src + n]

    def clear(dst, n):
        o_ref[:, dst:dst + n] = jnp.zeros((o_ref.shape[0], n), BF16)

    for h in range(A_HEADS):
        put(Z_QN + h * A_NOPE, h * A_QK, A_NOPE)
        put(Z_QR + h * A_ROPE, h * A_QK + A_NOPE, A_ROPE)
    ckv0 = A_HEADS * A_QK
    kr0 = ckv0 + A_KV_RANK
    bq0 = kr0 + A_ROPE
    bf0 = bq0 + 3 * B_HEADS * B_DIM
    cq0 = bf0 + B_HEADS
    put(Z_CKV, ckv0, A_KV_RANK)
    put(Z_BQ, bq0, 3 * B_HEADS * B_DIM)
    put(Z_CQ, cq0, 3 * C_HEADS * C_V)
    put(Z_KR, kr0, A_ROPE)
    clear(Z_KR + A_ROPE, LANES - A_ROPE)
    put(Z_BF, bf0, B_HEADS)
    clear(Z_BF + B_HEADS, Z_COLS - Z_BF - B_HEADS)


def _permute_w_in(w_in, layer, tr):
    _, d, cols = w_in.shape
    return pl.pallas_call(
        _permute_w_in_kernel,
        grid=(d // tr,),
        in_specs=[pl.BlockSpec((1, tr, cols), lambda i: (layer, i, 0))],
        out_specs=pl.BlockSpec((tr, Z_COLS), lambda i: (i, 0)),
        out_shape=jax.ShapeDtypeStruct((d, Z_COLS), BF16),
        compiler_params=_params(("arbitrary",)),
        name="permute_w_in",
    )(w_in)


def _norm_mm_kernel(x_ref, g_ref, w_ref, o_ref, h_scr):
    @pl.when(pl.program_id(1) == 0)
    def _():
        x = x_ref[...]
        ms = jnp.mean(x * x, axis=-1, keepdims=True)
        h_scr[...] = (x * lax.rsqrt(ms + EPS) * g_ref[...]).astype(BF16)

    o_ref[...] = _dot(h_scr[...], w_ref[...])


def _norm_mm(x, g, w, tm, tn):
    m, d = x.shape
    n = w.shape[1]
    return pl.pallas_call(
        _norm_mm_kernel,
        grid=(m // tm, n // tn),
        in_specs=[pl.BlockSpec((tm, d), lambda i, j: (i, 0)),
                  pl.BlockSpec((1, d), lambda i, j: (0, 0)),
                  pl.BlockSpec((d, tn), lambda i, j: (0, j))],
        out_specs=pl.BlockSpec((tm, tn), lambda i, j: (i, j)),
        out_shape=jax.ShapeDtypeStruct((m, n), F32),
        scratch_shapes=[pltpu.VMEM((tm, d), BF16)],
        compiler_params=_params(("arbitrary", "arbitrary")),
        name="norm_mm",
    )(x, g, w)


def _seg_sumsq(x, bd):
    sq = x * x
    hi = sq.astype(BF16)
    lo = (sq - hi.astype(F32)).astype(BF16)
    return _dot(hi, bd) + _dot(lo, bd)


def _seg_norm(x, bd, seg, g):
    return x * lax.rsqrt(_seg_sumsq(x, bd) * (1.0 / seg) + EPS) * g


def _rope(x, cos, sin, half):
    n = x.shape[-1]
    first = (_lane_iota(x.shape) % 64) < half
    partner = jnp.where(first, pltpu.roll(x, n - half, 1), pltpu.roll(x, half, 1))
    return x * cos + partner * sin


def _store_heads(o_ref, x):
    rows = x.shape[0]
    heads = x.shape[1] // LANES
    for h in range(heads):
        o_ref[pl.ds(h, rows, stride=heads), :] = x[:, h * LANES:(h + 1) * LANES]


def _post_kernel(z_ref, cosa_ref, sina_ref, cosc_ref, sinc_ref,
                 gqn_ref, gqr_ref, gckv_ref, gkr_ref, gkn_ref, gbq_ref, gbk_ref, gcq_ref, gck_ref,
                 bfg_ref, bd64_ref, bd128_ref, ones_ref, bdkr_ref, wk_ref, wv_ref, *refs):
    (aq_ref, kf_ref, va_ref, ckv_ref, kr_ref, bq_ref, bk32_ref, bk16_ref,
     bv32_ref, bv16_ref, lf_ref, cq_ref, ck32_ref, ck16_ref, cv32_ref, cv16_ref) = refs[-16:]
    bd64 = bd64_ref[...]
    bd128 = bd128_ref[...]
    tm = z_ref.shape[0]
    lane = _lane_iota((tm, LANES))
    low = lane < 64

    cosa4 = jnp.concatenate([cosa_ref[...]] * 4, axis=1)
    sina4 = jnp.concatenate([sina_ref[...]] * 4, axis=1)
    cosc4 = jnp.concatenate([cosc_ref[...]] * 4, axis=1)
    sinc4 = jnp.concatenate([sinc_ref[...]] * 4, axis=1)

    a_scale = A_QK ** -0.5 * LOG2E
    qr = _seg_norm(z_ref[:, Z_QR:Z_QR + 512], bd64, 64, gqr_ref[...])
    qr = _rope(qr, cosa4, sina4, A_ROPE // 2) * a_scale
    for half in range(2):
        c0 = Z_QN + half * 512
        qn = _seg_norm(z_ref[:, c0:c0 + 512], bd128, 128, gqn_ref[:, half * 512:(half + 1) * 512]) * a_scale
        for hh in range(4):
            h = half * 4 + hh
            aq_ref[:, h * 256:h * 256 + 128] = qn[:, hh * 128:(hh + 1) * 128].astype(BF16)
    for h in range(A_HEADS):
        pair = qr[:, (h // 2) * 128:(h // 2 + 1) * 128]
        keep = low if h % 2 == 0 else jnp.logical_not(low)
        aq_ref[:, h * 256 + 128:h * 256 + 256] = jnp.where(keep, pair, 0.0).astype(BF16)

    ckv = _seg_norm(z_ref[:, Z_CKV:Z_CKV + 512], ones_ref[...], 512, gckv_ref[...])
    ckv_ref[...] = ckv
    ckv16 = ckv.astype(BF16)
    va_ref[...] = _dot(ckv16, wv_ref[...]).astype(BF16)

    kr = _seg_norm(z_ref[:, Z_KR:Z_KR + 128], bdkr_ref[...], 64, gkr_ref[...])
    kr = _rope(kr, cosa_ref[...], sina_ref[...], A_ROPE // 2)
    kr_ref[...] = kr[:, :A_ROPE]
    kr2 = kr + pltpu.roll(kr, 64, 1)
    kr_even = jnp.where(low, kr2, 0.0).astype(BF16)
    kr_odd = jnp.where(low, 0.0, kr2).astype(BF16)
    for half in range(2):
        kk = _dot(ckv16, wk_ref[:, half * 512:(half + 1) * 512])
        kn = _seg_norm(kk, bd128, 128, gkn_ref[:, half * 512:(half + 1) * 512])
        for hh in range(4):
            h = half * 4 + hh
            kf_ref[:, h * 256:h * 256 + 128] = kn[:, hh * 128:(hh + 1) * 128].astype(BF16)
            kf_ref[:, h * 256 + 128:h * 256 + 256] = kr_even if h % 2 == 0 else kr_odd

    bq = _seg_norm(z_ref[:, Z_BQ:Z_BQ + 512], bd128, 128, gbq_ref[...]) * (B_DIM ** -0.5 * LOG2E)
    bq_ref[...] = bq.astype(BF16)
    bk = _seg_norm(z_ref[:, Z_BK:Z_BK + 512], bd128, 128, gbk_ref[...])
    _store_heads(bk32_ref, bk)
    bk16_ref[...] = bk.astype(BF16)
    bv = z_ref[:, Z_BV:Z_BV + 512]
    _store_heads(bv32_ref, bv)
    bv16_ref[...] = bv.astype(BF16)
    f = z_ref[:, Z_BF:Z_BF + 128][:, 0:B_HEADS] + bfg_ref[:, 0:B_HEADS]
    lf_ref[...] = jnp.minimum(f, 0.0) - jnp.log1p(jnp.exp(-jnp.abs(f)))

    cq = _seg_norm(z_ref[:, Z_CQ:Z_CQ + 512], bd64, 64, gcq_ref[...])
    cq_ref[...] = (_rope(cq, cosc4, sinc4, C_ROT // 2) * (C_QK ** -0.5 * LOG2E)).astype(BF16)
    ck = _seg_norm(z_ref[:, Z_CK:Z_CK + 512], bd64, 64, gck_ref[...])
    ck = _rope(ck, cosc4, sinc4, C_ROT // 2)
    _store_heads(ck32_ref, ck)
    ck16_ref[...] = ck.astype(BF16)
    cv = z_ref[:, Z_CV:Z_CV + 512]
    _store_heads(cv32_ref, cv)
    cv16_ref[...] = cv.astype(BF16)


STACKED = (3, 6, 8, 12, 14)


def _post(z, tabs, consts, tm, layer, depth, prev):
    m = z.shape[0]
    nblk = m // tm
    tab_rows = tabs[0].shape[0]
    tab_blocks = tab_rows // tm

    def row(w):
        return pl.BlockSpec((tm, w), lambda i: (i, 0))

    def full(a):
        return pl.BlockSpec(a.shape, lambda i: (0, 0))

    tab_spec = pl.BlockSpec((tm, LANES), lambda i: (i % tab_blocks, 0))
    widths = [(2048, BF16, 0), (2048, BF16, 0), (1024, BF16, 0), (512, F32, 0), (64, F32, 0), (512, BF16, 0),
              (512, F32, B_HEADS), (512, BF16, 0), (512, F32, B_HEADS), (512, BF16, 0), (B_HEADS, F32, 0),
              (512, BF16, 0), (512, F32, C_HEADS), (512, BF16, 0), (512, F32, C_HEADS), (512, BF16, 0)]
    def out_spec(k, w, hd):
        first = layer * nblk if k in STACKED else 0
        return pl.BlockSpec((tm * max(hd, 1), w // max(hd, 1)), lambda i: (first + i, 0))

    def out_sds(k, w, dt, hd):
        reps = depth if k in STACKED else 1
        return jax.ShapeDtypeStruct((reps * m * max(hd, 1), w // max(hd, 1)), dt)

    prev = list(prev) if prev is not None else []
    n_in = 5 + len(consts)
    return pl.pallas_call(
        _post_kernel,
        grid=(nblk,),
        in_specs=([row(Z_COLS)] + [tab_spec] * 4 + [full(c) for c in consts]
                  + [pl.BlockSpec(memory_space=pl.ANY)] * len(prev)),
        out_specs=[out_spec(k, w, hd) for k, (w, _, hd) in enumerate(widths)],
        out_shape=[out_sds(k, w, dt, hd) for k, (w, dt, hd) in enumerate(widths)],
        input_output_aliases={n_in + p: k for p, k in enumerate(STACKED[:len(prev)])},
        compiler_params=_params(("arbitrary",)),
        name="post",
    )(z, *tabs, *consts, *prev)


def _split3(c):
    hi = c.astype(BF16).astype(F32)
    r1 = c - hi
    mid = r1.astype(BF16).astype(F32)
    lo = r1 - mid
    return hi, mid, lo


def _cumsum_block(lf, carry):
    n = lf.shape[0]
    r = lax.broadcasted_iota(jnp.int32, (n, n), 0)
    c = lax.broadcasted_iota(jnp.int32, (n, n), 1)
    tri = jnp.where(c <= r, 1.0, 0.0).astype(BF16)
    hi, mid, lo = _split3(lf)
    cum = _dot(tri, hi.astype(BF16)) + _dot(tri, mid.astype(BF16)) + _dot(tri, lo.astype(BF16))
    return cum + carry


def _aug_q(c, lane):
    hi, mid, lo = _split3(c)
    return jnp.where(lane == 0, hi, jnp.where(lane == 1, mid, jnp.where(lane == 2, lo,
                     jnp.where(lane < 6, 1.0, 0.0)))).astype(BF16)


def _aug_k(c, lane):
    hi, mid, lo = _split3(c)
    return jnp.where(lane < 3, 1.0, jnp.where(lane == 3, -hi, jnp.where(lane == 4, -mid,
                     jnp.where(lane == 5, -lo, 0.0)))).astype(BF16)


def _fox_prep_kernel(lf_ref, bq_ref, bk_ref, *out_refs, past, blocks):
    carry_scr = out_refs[-1]
    if past:
        qa_ref, ka_ref, kpast_ref = out_refs[:-1]
    else:
        qa_ref, ka_ref = out_refs[:-1]

    @pl.when(pl.program_id(1) == 0)
    def _():
        carry_scr[...] = jnp.zeros(carry_scr.shape, F32)

    carry = carry_scr[...]
    for r0, n in blocks:
        cum = _cumsum_block(lf_ref[0, r0:r0 + n, :], carry)
        carry = cum[n - 1:n, :]
        lane = _lane_iota((n, LANES))
        for h in range(B_HEADS):
            c = cum[:, h:h + 1] * LOG2E
            if r0 < past:
                kpast_ref[0, r0:r0 + n, h * 128:(h + 1) * 128] = _aug_k(c, lane)
            else:
                t0 = r0 - past
                qa_ref[t0:t0 + n, h * 256:h * 256 + 128] = bq_ref[t0:t0 + n, h * 128:(h + 1) * 128]
                qa_ref[t0:t0 + n, h * 256 + 128:h * 256 + 256] = _aug_q(c, lane)
                ka_ref[t0:t0 + n, h * 256:h * 256 + 128] = bk_ref[t0:t0 + n, h * 128:(h + 1) * 128]
                ka_ref[t0:t0 + n, h * 256 + 128:h * 256 + 256] = _aug_k(c, lane)
    carry_scr[...] = carry


def _fox_prep(lf_all, bq, bk, t_new, rows, blocks):
    s, t_tot, _ = lf_all.shape
    past = t_tot - t_new
    nb = t_tot // rows
    new_rows = rows - past
    m = bq.shape[0]
    new_spec = lambda w: pl.BlockSpec((new_rows, w), lambda i, j: (i * nb + j, 0))
    out_specs = [new_spec(1024), new_spec(1024)]
    out_shape = [jax.ShapeDtypeStruct((m, 1024), BF16), jax.ShapeDtypeStruct((m, 1024), BF16)]
    if past:
        out_specs.append(pl.BlockSpec((1, past, 512), lambda i, j: (i, 0, 0)))
        out_shape.append(jax.ShapeDtypeStruct((s, past, 512), BF16))
    return pl.pallas_call(
        functools.partial(_fox_prep_kernel, past=past, blocks=blocks),
        grid=(s, nb),
        in_specs=[pl.BlockSpec((1, rows, B_HEADS), lambda i, j: (i, j, 0)), new_spec(512), new_spec(512)],
        out_specs=out_specs,
        out_shape=out_shape,
        scratch_shapes=[pltpu.VMEM((1, B_HEADS), F32)],
        compiler_params=_params(("arbitrary", "arbitrary")),
        name="fox_prep",
    )(lf_all, bq, bk)


def _lanes(x, n):
    return x if n == 1 else jnp.concatenate([x] * n, axis=1)


def _with_ones(v):
    return jnp.concatenate([v, jnp.ones_like(v)], axis=1)


def _softmax_start(rows):
    return jnp.full((rows, LANES), NEG_INF, F32), jnp.zeros((rows, 2 * LANES), F32)


def _softmax_step(state, s, v1):
    m_old, acc = state
    tk = s.shape[1]
    m_new = jnp.maximum(m_old, jnp.max(s, axis=-1, keepdims=True))
    alpha = jnp.exp2(m_old - m_new)
    m_full = _lanes(m_new, tk // LANES) if tk >= LANES else m_new[:, :tk]
    p = jnp.exp2(s - m_full)
    return m_new, _lanes(alpha, 2) * acc + _dot(p.astype(BF16), v1)


def _softmax_out(state):
    acc = state[1]
    return acc[:, :LANES] / acc[:, LANES:]


def _visible(kind, q0, k0, tq, tk):
    qp = q0 + lax.broadcasted_iota(jnp.int32, (tq, tk), 0)
    kp = k0 + lax.broadcasted_iota(jnp.int32, (tq, tk), 1)
    if kind == "chunk":
        return (kp // CHUNK) <= (qp // CHUNK)
    return kp <= qp


def _split_maps(q, n_maps):
    if n_maps == 1:
        return [q]
    low = _lane_iota(q.shape) < C_QK
    zero = jnp.zeros_like(q)
    return [jnp.where(low, q, zero), jnp.where(low, zero, q)]


def _diff_lambda(lq1_ref, lk1_ref, lq2_ref, lk2_ref, lam_init):
    s1 = jnp.sum(lq1_ref[...] * lk1_ref[...], axis=-1, keepdims=True)
    s2 = jnp.sum(lq2_ref[...] * lk2_ref[...], axis=-1, keepdims=True)
    return jnp.exp(s1) - jnp.exp(s2) + lam_init


def _attn_finish(states, extra_refs, lam_init):
    if len(states) == 1:
        return _softmax_out(states[0])
    lq1_ref, lk1_ref, lq2_ref, lk2_ref, gout_ref = extra_refs
    lam = _diff_lambda(lq1_ref, lk1_ref, lq2_ref, lk2_ref, lam_init)
    o = _softmax_out(states[0]) - lam * _softmax_out(states[1])
    ms = jnp.mean(o * o, axis=-1, keepdims=True)
    return o * lax.rsqrt(ms + EPS) * gout_ref[...] * (1.0 - lam_init)


def _attn_prompt_kernel(*refs, n_maps, mask, tq, tk, sub, lam_init):
    q_ref, k_ref, v_ref = refs[:3]
    extra_refs = refs[3:-3]
    o_ref, m_scr, acc_scr = refs[-3:]
    qi = pl.program_id(2)
    n_sub = tq // sub
    m0, acc0 = _softmax_start(sub)
    for i in range(n_maps):
        for r in range(n_sub):
            m_scr[i, r] = m0
            acc_scr[i, r] = acc0

    def blocks(key_blocks):
        todo = []
        for k0, nk, diag in key_blocks:
            k = k_ref[pl.ds(k0, nk), :]
            v1 = _with_ones(v_ref[pl.ds(k0, nk), :])
            for r in range(n_sub):
                vis = None
                if diag is not None:
                    if diag > r * sub + sub - 1:
                        continue
                    if diag + nk - 1 > r * sub:
                        vis = _visible(mask, r * sub, diag, sub, nk)
                scores = [_dot_nt(qm, k) for qm in _split_maps(q_ref[r * sub:(r + 1) * sub, :], n_maps)]
                todo.append((r, vis, scores, v1))
        for r, vis, scores, v1 in todo:
            for i in range(n_maps):
                s = scores[i] if vis is None else jnp.where(vis, scores[i], NEG_INF)
                m_scr[i, r], acc_scr[i, r] = _softmax_step((m_scr[i, r], acc_scr[i, r]), s, v1)

    def full_block(kb, carry):
        blocks([(pl.multiple_of(kb * tq, tq), tq, None)])
        return carry

    lax.fori_loop(0, qi, full_block, 0)
    blocks([(pl.multiple_of(qi * tq + d * tk, tk), tk, d * tk) for d in range(tq // tk)])
    for r in range(n_sub):
        states = [(m_scr[i, r], acc_scr[i, r]) for i in range(n_maps)]
        o_ref[r * sub:(r + 1) * sub, :] = _attn_finish(states, extra_refs, lam_init).astype(o_ref.dtype)


def _attn_prompt(q, k, v, extras, *, batch, heads, t, dqk, n_maps, mask, lam_init=0.0):
    tq, tk, sub = ATTN_TQ, ATTN_TK, ATTN_SUB
    nq = t // tq
    dv = 128
    extra_specs = [pl.BlockSpec(e.shape, lambda b, h, i: (0, 0)) for e in extras]
    return pl.pallas_call(
        functools.partial(_attn_prompt_kernel, n_maps=n_maps, mask=mask, tq=tq, tk=tk, sub=sub,
                          lam_init=lam_init),
        grid=(batch, heads, nq),
        in_specs=[pl.BlockSpec((tq, dqk), lambda b, h, i: (b * nq + i, h)),
                  pl.BlockSpec((t, dqk), lambda b, h, i: (b, h)),
                  pl.BlockSpec((t, dv), lambda b, h, i: (b, h))] + extra_specs,
        out_specs=pl.BlockSpec((tq, dv), lambda b, h, i: (b * nq + i, h)),
        out_shape=jax.ShapeDtypeStruct((batch * t, heads * dv), BF16),
        scratch_shapes=[pltpu.VMEM((n_maps, tq // sub, sub, LANES), F32),
                        pltpu.VMEM((n_maps, tq // sub, sub, 2 * LANES), F32)],
        compiler_params=_params(("arbitrary", "arbitrary", "arbitrary")),
        name="attn_prompt_" + mask + str(n_maps),
    )(q, k, v, *extras)


def _mla_sample_kernel(q_ref, kf_ref, va_ref, ckv_ref, kr_ref, wukv_ref, gkn_ref, o_ref, *, past, t):
    ckv_p = ckv_ref[0, 0]
    kr_p = kr_ref[0, 0]
    vis = _visible("chunk", past, past, t, t)
    heads = range(A_HEADS)
    kvs = [_dot(ckv_p, wukv_ref[:, h * 256:(h + 1) * 256]) for h in heads]
    kns, vps = [], []
    for h in heads:
        kk = kvs[h][:, :A_NOPE]
        ms = jnp.mean(kk * kk, axis=-1, keepdims=True)
        kns.append((kk * lax.rsqrt(ms + EPS) * gkn_ref[...]).astype(BF16))
        vps.append(_with_ones(kvs[h][:, A_NOPE:].astype(BF16)))
    s_past, s_new = [], []
    for h in heads:
        q = q_ref[:, h * 256:(h + 1) * 256]
        r0 = 128 + (h % 2) * A_ROPE
        s_past.append(_dot_nt(q[:, :128], kns[h]) + _dot_nt(q[:, r0:r0 + A_ROPE], kr_p))
        s_new.append(jnp.where(vis, _dot_nt(q, kf_ref[:, h * 256:(h + 1) * 256]), NEG_INF))
    for h in heads:
        state = _softmax_step(_softmax_start(t), s_past[h], vps[h])
        state = _softmax_step(state, s_new[h], _with_ones(va_ref[:, h * 128:(h + 1) * 128]))
        o_ref[:, h * 128:(h + 1) * 128] = _softmax_out(state).astype(o_ref.dtype)


def _mla_sample(q, kf, va, cache_ckv, cache_kr, w_ukv, gkn, *, layer, t):
    _, s, past, rank = cache_ckv.shape
    return pl.pallas_call(
        functools.partial(_mla_sample_kernel, past=past, t=t),
        grid=(s,),
        in_specs=[pl.BlockSpec((t, 2048), lambda i: (i, 0)),
                  pl.BlockSpec((t, 2048), lambda i: (i, 0)),
                  pl.BlockSpec((t, 1024), lambda i: (i, 0)),
                  pl.BlockSpec((1, 1, past, rank), lambda i: (layer, i, 0, 0)),
                  pl.BlockSpec((1, 1, past, A_ROPE), lambda i: (layer, i, 0, 0)),
                  pl.BlockSpec(w_ukv.shape, lambda i: (0, 0)),
                  pl.BlockSpec(gkn.shape, lambda i: (0, 0))],
        out_specs=pl.BlockSpec((t, 1024), lambda i: (i, 0)),
        out_shape=jax.ShapeDtypeStruct((s * t, 1024), BF16),
        compiler_params=_params(("arbitrary",)),
        name="mla_sample",
    )(q, kf, va, cache_ckv, cache_kr, w_ukv, gkn)


def _bc_sample_kernel(*refs, n_maps, mask, heads, past, t, lam_init, with_aug):
    q_ref, kn_ref, vn_ref, kp_ref, vp_ref = refs[:5]
    rest = refs[5:-1]
    o_ref = refs[-1]
    if with_aug:
        kaug_ref, extra_refs = rest[0], rest[1:]
    else:
        extra_refs = rest
    dq = q_ref.shape[1] // heads
    vis = _visible(mask, past, past, t, t)
    scores, vps = [], []
    for h in range(heads):
        q = q_ref[:, h * dq:(h + 1) * dq]
        kp = kp_ref[0, 0, pl.ds(h, past, stride=heads), :].astype(BF16)
        vps.append(_with_ones(vp_ref[0, 0, pl.ds(h, past, stride=heads), :].astype(BF16)))
        kn = kn_ref[:, h * dq:(h + 1) * dq]
        if with_aug:
            s_past = _dot_nt(q[:, :128], kp) + _dot_nt(q[:, 128:], kaug_ref[0, :, h * 128:(h + 1) * 128])
            scores.append([(s_past, jnp.where(vis, _dot_nt(q, kn), NEG_INF))])
        else:
            scores.append([(_dot_nt(qm, kp), jnp.where(vis, _dot_nt(qm, kn), NEG_INF))
                           for qm in _split_maps(q, n_maps)])
    for h in range(heads):
        vn = _with_ones(vn_ref[:, h * 128:(h + 1) * 128])
        states = [_softmax_step(_softmax_step(_softmax_start(t), s_past, vps[h]), s_new, vn)
                  for s_past, s_new in scores[h]]
        o_ref[:, h * 128:(h + 1) * 128] = _attn_finish(states, extra_refs, lam_init).astype(o_ref.dtype)


def _bc_sample(q, kn, vn, cache_k, cache_v, kaug, extras, *, layer, heads, n_maps, mask, t, lam_init=0.0):
    _, s, rows, _ = cache_k.shape
    past = rows // heads
    dq = q.shape[1]
    cache_spec = pl.BlockSpec((1, 1, rows, 128), lambda i: (layer, i, 0, 0))
    in_specs = [pl.BlockSpec((t, dq), lambda i: (i, 0)),
                pl.BlockSpec((t, dq), lambda i: (i, 0)),
                pl.BlockSpec((t, heads * 128), lambda i: (i, 0)),
                cache_spec, cache_spec]
    args = [q, kn, vn, cache_k, cache_v]
    if kaug is not None:
        in_specs.append(pl.BlockSpec((1, past, heads * 128), lambda i: (i, 0, 0)))
        args.append(kaug)
    in_specs += [pl.BlockSpec(e.shape, lambda i: (0, 0)) for e in extras]
    return pl.pallas_call(
        functools.partial(_bc_sample_kernel, n_maps=n_maps, mask=mask, heads=heads, past=past, t=t,
                          lam_init=lam_init, with_aug=kaug is not None),
        grid=(s,),
        in_specs=in_specs,
        out_specs=pl.BlockSpec((t, heads * 128), lambda i: (i, 0)),
        out_shape=jax.ShapeDtypeStruct((s * t, heads * 128), BF16),
        compiler_params=_params(("arbitrary",)),
        name="sample_" + mask + str(n_maps),
    )(*args, *extras)


def _mm_res_kernel(*refs, n_in):
    res_ref = refs[2 * n_in]
    o_ref = refs[2 * n_in + 1]
    acc = res_ref[...]
    for i in range(n_in):
        acc = acc + _dot(refs[i][...], refs[n_in + i][...])
    o_ref[...] = acc


def _mm_res(res, a_list, w_list, tm, tn):
    m, n = res.shape
    n_in = len(a_list)
    in_specs = ([pl.BlockSpec((tm, a.shape[1]), lambda j, i: (i, 0)) for a in a_list]
                + [pl.BlockSpec((w.shape[0], tn), lambda j, i: (0, j)) for w in w_list]
                + [pl.BlockSpec((tm, tn), lambda j, i: (i, j))])
    return pl.pallas_call(
        functools.partial(_mm_res_kernel, n_in=n_in),
        grid=(n // tn, m // tm),
        in_specs=in_specs,
        out_specs=pl.BlockSpec((tm, tn), lambda j, i: (i, j)),
        out_shape=jax.ShapeDtypeStruct((m, n), F32),
        compiler_params=_params(("arbitrary", "arbitrary")),
        name="mm_res%d" % n_in,
    )(*a_list, *w_list, res)


def _ffn_up_kernel(x_ref, g_ref, wg_ref, wv_ref, cwg_ref, cwv_ref, cbg_ref, cbv_ref, sg_ref, sv_ref,
                   a_ref, ng_ref, nv_ref, h_scr, rawg, rawv, carg, carv, *, rows, tiles_per_stream):
    i = pl.program_id(0)
    j = pl.program_id(1)
    tm = x_ref.shape[0]
    n_sub = tm // rows
    tn = a_ref.shape[1]
    chunks = [(c, c + FFN_CHUNK) for c in range(0, tn, FFN_CHUNK)]

    @pl.when(j == 0)
    def _():
        x = x_ref[...]
        ms = jnp.mean(x * x, axis=-1, keepdims=True)
        h_scr[...] = (x * lax.rsqrt(ms + EPS) * g_ref[...]).astype(BF16)

    part = tm // FFN_PARTS
    items = [(c0, c1, lo, lo + part) for c0, c1 in chunks for lo in range(0, tm, part)]
    dots = [(_dot(h_scr[lo:hi, :], wg_ref[:, c0:c1]), _dot(h_scr[lo:hi, :], wv_ref[:, c0:c1]))
            for c0, c1, lo, hi in items]
    for (c0, c1, lo, hi), us in zip(items, dots):
        sides = ((us[0], rawg, carg, sg_ref, cwg_ref, cbg_ref, ng_ref),
                 (us[1], rawv, carv, sv_ref, cwv_ref, cbv_ref, nv_ref))
        segs = [(s, max(lo, s * rows), min(hi, (s + 1) * rows)) for s in range(n_sub)
                if s * rows < hi and (s + 1) * rows > lo]
        for u, raw, car, s_ref, _, _, n_ref in sides:
            for s, a, b in segs:
                raw[s, HALO_ROW + a - s * rows:HALO_ROW + b - s * rows, c0:c1] = u[a - lo:b - lo]
                if a == s * rows:
                    if tiles_per_stream == 1:
                        halo = s_ref[0, s, :, c0:c1]
                    else:
                        halo = jnp.where((i % tiles_per_stream) == 0, s_ref[0, 0, :, c0:c1], car[j, :, c0:c1])
                    raw[s, HALO_ROW - 2:HALO_ROW, c0:c1] = halo
                if b == (s + 1) * rows:
                    n_ref[s, :, c0:c1] = u[b - lo - 2:b - lo]
            if tiles_per_stream > 1 and hi == tm:
                car[j, :, c0:c1] = u[hi - lo - 2:hi - lo]

        def conv(side, s, r0):
            _, raw, _, _, cw_ref, cb_ref, _ = side
            base = HALO_ROW + r0
            return (cb_ref[:, c0:c1]
                    + raw[s, base - 2:base - 2 + FFN_ROWS, c0:c1] * cw_ref[0:1, c0:c1]
                    + raw[s, base - 1:base - 1 + FFN_ROWS, c0:c1] * cw_ref[1:2, c0:c1]
                    + raw[s, base:base + FFN_ROWS, c0:c1] * cw_ref[2:3, c0:c1])

        for s, a, b in segs:
            for r in range(a, b, FFN_ROWS):
                gate = conv(sides[0], s, r - s * rows)
                val = conv(sides[1], s, r - s * rows)
                a_ref[r:r + FFN_ROWS, c0:c1] = (gate * (1.0 / (1.0 + jnp.exp(-gate))) * val).astype(BF16)


def _ffn_up(x, g, w_up, conv_w, conv_b, state, *, layer, stream_len, tm, tn):
    m, d = x.shape
    nj = D_FF // tn
    if stream_len >= tm:
        rows, tiles_per_stream, n_sub = tm, stream_len // tm, 1
        stream_of = lambda i: i // tiles_per_stream
    else:
        rows, tiles_per_stream, n_sub = stream_len, 1, tm // stream_len
        stream_of = lambda i: i
    gcol = lambda i, j: (0, j)
    vcol = lambda i, j: (0, nj + j)
    raw = pltpu.VMEM((n_sub, HALO_ROW + rows, tn), F32)
    outs = pl.pallas_call(
        functools.partial(_ffn_up_kernel, rows=rows, tiles_per_stream=tiles_per_stream),
        grid=(m // tm, nj),
        in_specs=[pl.BlockSpec((tm, d), lambda i, j: (i, 0)),
                  pl.BlockSpec((1, d), lambda i, j: (0, 0)),
                  pl.BlockSpec((d, tn), gcol), pl.BlockSpec((d, tn), vcol),
                  pl.BlockSpec((CONV_W, tn), gcol), pl.BlockSpec((CONV_W, tn), vcol),
                  pl.BlockSpec((1, tn), gcol), pl.BlockSpec((1, tn), vcol),
                  pl.BlockSpec((1, n_sub, 2, tn), lambda i, j: (layer, stream_of(i), 0, j)),
                  pl.BlockSpec((1, n_sub, 2, tn), lambda i, j: (layer, stream_of(i), 0, nj + j))],
        out_specs=[pl.BlockSpec((tm, tn), lambda i, j: (i, j)),
                   pl.BlockSpec((n_sub, 2, tn), lambda i, j: (i, 0, j)),
                   pl.BlockSpec((n_sub, 2, tn), lambda i, j: (i, 0, j))],
        out_shape=[jax.ShapeDtypeStruct((m, D_FF), BF16),
                   jax.ShapeDtypeStruct((m // rows, 2, D_FF), F32),
                   jax.ShapeDtypeStruct((m // rows, 2, D_FF), F32)],
        scratch_shapes=[pltpu.VMEM((tm, d), BF16), raw, raw,
                        pltpu.VMEM((nj, 2, tn), F32), pltpu.VMEM((nj, 2, tn), F32)],
        compiler_params=_params(("arbitrary", "arbitrary")),
        name="ffn_up",
    )(x, g, w_up, w_up, conv_w, conv_w, conv_b, conv_b, state, state)
    a, tail_g, tail_v = outs
    last = slice(tiles_per_stream - 1, None, tiles_per_stream)
    return a, tail_g[last], tail_v[last]


def _rope_tables(pos, reps):
    posf = pos.astype(F32)[:, None]

    def tab(n_rot):
        half = n_rot // 2
        inv = jnp.power(jnp.float32(ROPE_THETA), -jnp.arange(half, dtype=F32) * (2.0 / n_rot))
        ang = posf * inv[None, :]
        rest = 64 - n_rot
        cos = jnp.concatenate([jnp.cos(ang), jnp.cos(ang), jnp.ones((pos.shape[0], rest), F32)], axis=1)
        sin = jnp.concatenate([-jnp.sin(ang), jnp.sin(ang), jnp.zeros((pos.shape[0], rest), F32)], axis=1)
        return jnp.tile(cos, (reps, 2)), jnp.tile(sin, (reps, 2))

    cosa, sina = tab(A_ROPE)
    cosc, sinc = tab(C_ROT)
    return cosa, sina, cosc, sinc


def _block_diag(width, seg):
    r = jnp.arange(width)[:, None] // seg
    c = jnp.arange(width)[None, :] // seg
    return (r == c).astype(BF16)


def _layer_weights(l, attn_norm, w_in, b_forget, a_kv_norm, a_w_ukv, a_qn_nope, a_qn_rope, a_kn_nope,
                   a_kn_rope, b_qn, b_kn, c_qn, c_kn, c_lq1, c_lk1, c_lq2, c_lk2, c_out_norm, w_out,
                   ffn_norm, w_up, conv_w, conv_b, w_down):
    w_in_p = _permute_w_in(w_in.astype(BF16), l, 256)
    ukv = a_w_ukv[l].reshape(A_KV_RANK, A_HEADS, A_NOPE + A_V)
    wk = ukv[:, :, :A_NOPE].reshape(A_KV_RANK, 1024).astype(BF16)
    wv = ukv[:, :, A_NOPE:].reshape(A_KV_RANK, 1024).astype(BF16)
    row = lambda v: v.reshape(1, -1).astype(F32)
    tile = lambda v, n: jnp.tile(v.astype(F32), n).reshape(1, -1)
    bd_kr = (jnp.arange(128)[:, None] < 64).astype(BF16) * jnp.ones((1, 128), BF16)
    post_consts = [
        tile(a_qn_nope[l], 8), tile(a_qn_rope[l], 8), row(a_kv_norm[l]),
        jnp.concatenate([a_kn_rope[l], jnp.zeros((64,), F32)]).reshape(1, 128),
        tile(a_kn_nope[l], 8), tile(b_qn[l], 4), tile(b_kn[l], 4), tile(c_qn[l], 8), tile(c_kn[l], 8),
        jnp.concatenate([b_forget[l], jnp.zeros((128 - B_HEADS,), F32)]).reshape(1, 128),
        _block_diag(512, 64), _block_diag(512, 128), jnp.ones((512, 512), BF16), bd_kr, wk, wv]
    wo = w_out[l].astype(BF16)
    return dict(
        attn_norm=row(attn_norm[l]), w_in=w_in_p, post_consts=post_consts,
        w_ukv=a_w_ukv[l].astype(BF16), gkn=row(a_kn_nope[l]),
        diff_extras=[row(c_lq1[l]), row(c_lk1[l]), row(c_lq2[l]), row(c_lk2[l]), row(c_out_norm[l])],
        wo_a=wo[:1024], wo_b=wo[1024:1536], wo_c=wo[1536:],
        ffn_norm=row(ffn_norm[l]), w_up=w_up[l].astype(BF16), conv_w=conv_w[l].astype(F32),
        conv_b=row(conv_b[l]), w_down=w_down[l].astype(BF16))


def _run_layer(x, lw, tabs, past, stacked, *, layer_index, depth, n_streams, t, lam_init, post_tm, ffn_tm):
    m = x.shape[0]
    z = _norm_mm(x, lw["attn_norm"], lw["w_in"], DENSE_TM, Z_TN)
    (aq, kf, va, ckv, kr, bq, bk32, bk16, bv32, bv16, lf, cq, ck32, ck16, cv32, cv16) = _post(
        z, tabs, lw["post_consts"], post_tm, layer_index, depth, stacked)

    if past is None:
        qa, ka = _fox_prep(lf.reshape(n_streams, t, B_HEADS), bq, bk16, t, 1024, ((0, 512), (512, 512)))
        oa = _attn_prompt(aq, kf, va, [], batch=n_streams, heads=A_HEADS, t=t, dqk=256, n_maps=1,
                          mask="chunk")
        ob = _attn_prompt(qa, ka, bv16, [], batch=n_streams, heads=B_HEADS, t=t, dqk=256, n_maps=1,
                          mask="causal")
        oc = _attn_prompt(cq, ck16, cv16, lw["diff_extras"], batch=n_streams, heads=C_HEADS, t=t, dqk=128,
                          n_maps=2, mask="chunk", lam_init=lam_init)
        conv_state, conv_layer = jnp.zeros((1, n_streams, CONV_W - 1, 2 * D_FF), F32), 0
    else:
        layer, c_ckv, c_kr, c_bk, c_bv, c_lf, c_ck, c_cv, conv_state = past
        p_len = c_ckv.shape[2]
        conv_state, conv_layer = conv_state[layer][None], 0
        lf_all = jnp.concatenate([c_lf[layer], lf.reshape(n_streams, t, B_HEADS)], axis=1)
        blocks = tuple((r, 512) for r in range(0, p_len, 512)) + ((p_len, t),)
        qa, ka, kaug = _fox_prep(lf_all, bq, bk16, t, p_len + t, blocks)
        oa = _mla_sample(aq, kf, va, c_ckv, c_kr, lw["w_ukv"], lw["gkn"], layer=layer, t=t)
        ob = _bc_sample(qa, ka, bv16, c_bk, c_bv, kaug, [], layer=layer, heads=B_HEADS,
                        n_maps=1, mask="causal", t=t)
        oc = _bc_sample(cq, ck16, cv16, c_ck, c_cv, None, lw["diff_extras"], layer=layer,
                        heads=C_HEADS, n_maps=2, mask="chunk", t=t, lam_init=lam_init)

    x1 = _mm_res(x, [oa, ob, oc], [lw["wo_a"], lw["wo_b"], lw["wo_c"]], 512, 1024)
    a, conv_g, conv_v = _ffn_up(x1, lw["ffn_norm"], lw["w_up"], lw["conv_w"], lw["conv_b"], conv_state,
                                layer=conv_layer, stream_len=t, tm=ffn_tm, tn=512)
    x2 = _mm_res(x1, [a], [lw["w_down"]], 512, 1024)
    small = (kr.reshape(n_streams, t, A_ROPE), lf.reshape(n_streams, t, B_HEADS),
             jnp.concatenate([conv_g, conv_v], axis=-1))
    return x2, small, (ckv, bk32, bv32, ck32, cv32)


def kernel(x_prompt, x_sample, cache_a_ckv, cache_a_krope, cache_b_k, cache_b_v, cache_b_logf, cache_c_k, cache_c_v, state_ffn_conv, attn_norm, w_in, b_forget, a_kv_norm, a_w_ukv, a_qn_nope, a_qn_rope, a_kn_nope, a_kn_rope, b_qn, b_kn, c_qn, c_kn, c_lq1, c_lk1, c_lq2, c_lk2, c_out_norm, w_out, ffn_norm, w_up, conv_w, conv_b, w_down):
    bp, tp, d = x_prompt.shape
    bs, ts, _ = x_sample.shape
    depth = w_in.shape[0]
    past_len = cache_a_ckv.shape[2]
    post_tm = 256
    tabs_p = _rope_tables(jnp.arange(tp, dtype=jnp.int32), 1)
    tabs_s = _rope_tables(past_len + jnp.arange(ts, dtype=jnp.int32), post_tm // ts)

    merge_heads = lambda c: c.reshape(c.shape[:2] + (c.shape[2] * c.shape[3], c.shape[4]))
    caches = (cache_a_ckv.astype(BF16), cache_a_krope.astype(BF16), merge_heads(cache_b_k),
              merge_heads(cache_b_v), cache_b_logf, merge_heads(cache_c_k), merge_heads(cache_c_v),
              state_ffn_conv)
    yp = x_prompt.reshape(bp * tp, d)
    ys = x_sample.reshape(bs * ts, d)
    states_p, states_s = [], []
    big_p = big_s = None
    for l in range(depth):
        lw = _layer_weights(l, attn_norm, w_in, b_forget, a_kv_norm, a_w_ukv, a_qn_nope, a_qn_rope,
                            a_kn_nope, a_kn_rope, b_qn, b_kn, c_qn, c_kn, c_lq1, c_lk1, c_lq2, c_lk2,
                            c_out_norm, w_out, ffn_norm, w_up, conv_w, conv_b, w_down)
        lam_init = 0.8 - 0.6 * math.exp(-0.3 * l)
        yp, st_p, big_p = _run_layer(yp, lw, tabs_p, None, big_p, layer_index=l, depth=depth, n_streams=bp,
                                     t=tp, lam_init=lam_init, post_tm=post_tm, ffn_tm=DENSE_TM)
        ys, st_s, big_s = _run_layer(ys, lw, tabs_s, (l,) + caches, big_s, layer_index=l, depth=depth,
                                     n_streams=bs, t=ts, lam_init=lam_init, post_tm=post_tm, ffn_tm=DENSE_TM)
        states_p.append(st_p)
        states_s.append(st_s)

    def assemble(small, big, n, t):
        kr, lf, conv = [jnp.stack(s) for s in zip(*small)]
        ckv, bk, bv, ck, cv = big
        heads = lambda a, h, w: a.reshape(depth, n, t, h, w)
        return (ckv.reshape(depth, n, t, A_KV_RANK), kr, heads(bk, B_HEADS, B_DIM), heads(bv, B_HEADS, B_DIM),
                lf, heads(ck, C_HEADS, 2 * C_QK), heads(cv, C_HEADS, C_V), conv)

    return (yp.reshape(bp, tp, d), ys.reshape(bs, ts, d), *assemble(states_p, big_p, bp, tp),
            *assemble(states_s, big_s, bs, ts))
```
